```python
import math
import jax, jax.numpy as jnp
from jax import lax
import numpy as np

D_MODEL = 4096
BATCH = 4
SEQ = 4096
DEPTH = 1

CHUNK = 64
Q_BLOCK = 128
HEAD_DIM = 128
N_HEADS_DSA = D_MODEL // 2 // HEAD_DIM
N_HEADS_FOX = D_MODEL // 2 // HEAD_DIM
D_LATENT = 512
N_IDX_HEADS = 32
IDX_DIM = 128
TOPK_MAX = 256
D_FF = 11008
N_BUCKETS = 32
MAX_DISTANCE = 128
ALPHA = (2.0 * DEPTH) ** 0.25
BETA = (8.0 * DEPTH) ** -0.25
LN_EPS = 1e-5
RMS_EPS = 1e-6
NEG = -1e30

W_DSA = N_HEADS_DSA * HEAD_DIM
W_FOX = N_HEADS_FOX * HEAD_DIM
MIX_WIDTH = W_DSA + W_FOX
SPLITS = (W_DSA,
          D_LATENT,
          N_IDX_HEADS * IDX_DIM,
          IDX_DIM,
          N_IDX_HEADS,
          W_FOX, W_FOX, W_FOX,
          N_HEADS_FOX)
D_IN = W_DSA + D_LATENT + N_IDX_HEADS * IDX_DIM + IDX_DIM + N_IDX_HEADS + 3 * W_FOX + N_HEADS_FOX

kernel_name = "hybrid_dsa_fox_macaron_deepnorm"


def split_offsets():
    offs, acc = [], 0
    for w in SPLITS[:-1]:
        acc += w
        offs.append(acc)
    return offs


def layer_norm(x, g, b):
    xf = x.astype(jnp.float32)
    mu = jnp.mean(xf, axis=-1, keepdims=True)
    var = jnp.mean(jnp.square(xf - mu), axis=-1, keepdims=True)
    y = (xf - mu) * lax.rsqrt(var + LN_EPS) * g.astype(jnp.float32) + b.astype(jnp.float32)
    return y.astype(x.dtype)


def rms_norm(x, g):
    xf = x.astype(jnp.float32)
    y = xf * lax.rsqrt(jnp.mean(jnp.square(xf), axis=-1, keepdims=True) + RMS_EPS) * g.astype(jnp.float32)
    return y.astype(x.dtype)


def swiglu(x, w_gate, w_up, w_down):
    return (jax.nn.silu(x @ w_gate) * (x @ w_up)) @ w_down


def t5_bucket(rel):
    nb = N_BUCKETS // 2
    max_exact = nb // 2
    ret = jnp.where(rel > 0, nb, 0)
    n = jnp.abs(rel)
    nf = jnp.maximum(n, 1).astype(jnp.float32)
    large = max_exact + (jnp.log(nf / max_exact) / math.log(MAX_DISTANCE / max_exact)
                         * (nb - max_exact)).astype(jnp.int32)
    large = jnp.minimum(large, nb - 1)
    return ret + jnp.where(n < max_exact, n, large)


def to_blocks(a):
    b, s = a.shape[:2]
    return jnp.moveaxis(a.reshape((b, s // Q_BLOCK, Q_BLOCK) + a.shape[2:]), 1, 0)


def from_blocks(a):
    a = jnp.moveaxis(a, 0, 1)
    b, nblk, qb = a.shape[:3]
    return a.reshape(b, nblk * qb, -1)


def dsa_mixer(q, c_kv, q_idx, k_idx, w_idx, w_uk, w_uv, rel_bias):
    b, s = q.shape[:2]
    k_sel = min(TOPK_MAX, s // 4)
    key_chunk = jnp.arange(s) // CHUNK
    scale = HEAD_DIM ** -0.5
    idx_scale = IDX_DIM ** -0.5
    w_scale = N_IDX_HEADS ** -0.5

    def block(args):
        qb, qi, wi, start = args
        t = start + jnp.arange(Q_BLOCK)
        t_chunk = t // CHUNK
        s_h = jnp.einsum('bthd,bsd->bths', qi, k_idx).astype(jnp.float32) * idx_scale
        score = jnp.einsum('bth,bths->bts', wi.astype(jnp.float32) * w_scale, jax.nn.relu(s_h))
        admissible = key_chunk[None, :] <= t_chunk[:, None]
        score = jnp.where(admissible[None], score, -jnp.inf)
        _, idx = lax.top_k(score, k_sel)
        c_sel = jax.vmap(lambda c, i: c[i])(c_kv, idx)
        valid = (idx // CHUNK) <= t_chunk[None, :, None]
        bias = rel_bias[t5_bucket(idx - t[None, :, None])]
        q_lat = jnp.einsum('bthd,hdc->bthc', qb, w_uk)
        logits = (jnp.einsum('bthc,btkc->bthk', q_lat, c_sel).astype(jnp.float32) * scale
                  + jnp.moveaxis(bias, -1, 2).astype(jnp.float32))
        logits = jnp.where(valid[:, :, None, :], logits, NEG)
        p = jax.nn.softmax(logits, axis=-1).astype(c_sel.dtype)
        o_lat = jnp.einsum('bthk,btkc->bthc', p, c_sel)
        return jnp.einsum('bthc,hcd->bthd', o_lat, w_uv)

    starts = jnp.arange(s // Q_BLOCK) * Q_BLOCK
    out = lax.map(block, (to_blocks(q), to_blocks(q_idx), to_blocks(w_idx), starts))
    return from_blocks(out)


def fox_mixer(q, k, v, log_f):
    s = q.shape[1]
    scale = HEAD_DIM ** -0.5
    cum = jnp.cumsum(log_f, axis=1)
    cum_k = jnp.moveaxis(cum, -1, 1)
    key_pos = jnp.arange(s)

    def block(args):
        qb, cq, start = args
        t = start + jnp.arange(Q_BLOCK)
        logits = jnp.einsum('bthd,bshd->bhts', qb, k).astype(jnp.float32) * scale
        logits = logits + (jnp.moveaxis(cq, -1, 1)[..., None] - cum_k[:, :, None, :])
        mask = key_pos[None, :] <= t[:, None]
        logits = jnp.where(mask[None, None], logits, NEG)
        p = jax.nn.softmax(logits, axis=-1).astype(v.dtype)
        return jnp.einsum('bhts,bshd->bthd', p, v)

    starts = jnp.arange(s // Q_BLOCK) * Q_BLOCK
    out = lax.map(block, (to_blocks(q), to_blocks(cum), starts))
    return from_blocks(out)


def hybrid_mixer(h, w_in, b_f, kv_norm_g, idx_k_g, idx_k_b, w_uk, w_uv, rel_bias, w_out):
    b, s, _ = h.shape
    proj = h @ w_in
    q_a, c_kv, q_i, k_i, w_i, q_b, k_b, v_b, f_b = jnp.split(proj, split_offsets(), axis=-1)
    c_kv = rms_norm(c_kv, kv_norm_g)
    k_i = layer_norm(k_i, idx_k_g, idx_k_b)
    o_a = dsa_mixer(q_a.reshape(b, s, N_HEADS_DSA, HEAD_DIM), c_kv,
                    q_i.reshape(b, s, N_IDX_HEADS, IDX_DIM), k_i, w_i, w_uk, w_uv, rel_bias)
    log_f = jax.nn.log_sigmoid(f_b.astype(jnp.float32) + b_f.astype(jnp.float32))
    o_b = fox_mixer(q_b.reshape(b, s, N_HEADS_FOX, HEAD_DIM),
                    k_b.reshape(b, s, N_HEADS_FOX, HEAD_DIM),
                    v_b.reshape(b, s, N_HEADS_FOX, HEAD_DIM), log_f)
    return jnp.concatenate([o_a, o_b], axis=-1) @ w_out


def setup_inputs(seed: int = 0) -> dict:
    key = jax.random.key(seed)
    ks = jax.random.split(key, 24)
    f32 = jnp.float32

    def nrm(k, shape, scale):
        return jax.random.normal(k, shape, f32) * scale

    def gain(k, shape):
        return 1.0 + 0.02 * jax.random.normal(k, shape, f32)

    L = DEPTH
    b_f = (jnp.linspace(1.0, 6.0, N_HEADS_FOX, dtype=f32)[None, :]
           + 0.1 * jax.random.normal(ks[8], (L, N_HEADS_FOX), f32))
    return {
        "x": jax.random.normal(ks[0], (BATCH, SEQ, D_MODEL), f32),
        "ffn1_w_gate": nrm(ks[1], (L, D_MODEL, D_FF), D_MODEL ** -0.5),
        "ffn1_w_up": nrm(ks[2], (L, D_MODEL, D_FF), D_MODEL ** -0.5),
        "ffn1_w_down": nrm(ks[3], (L, D_FF, D_MODEL), BETA * D_FF ** -0.5),
        "ln1_g": gain(ks[4], (L, D_MODEL)),
        "ln1_b": nrm(ks[5], (L, D_MODEL), 0.02),
        "w_in": nrm(ks[6], (L, D_MODEL, D_IN), D_MODEL ** -0.5),
        "b_f": b_f,
        "kv_norm_g": gain(ks[9], (L, D_LATENT)),
        "idx_k_g": gain(ks[10], (L, IDX_DIM)),
        "idx_k_b": nrm(ks[11], (L, IDX_DIM), 0.02),
        "w_uk": nrm(ks[12], (L, N_HEADS_DSA, HEAD_DIM, D_LATENT), D_LATENT ** -0.5),
        "w_uv": nrm(ks[13], (L, N_HEADS_DSA, D_LATENT, HEAD_DIM), D_LATENT ** -0.5),
        "rel_bias": nrm(ks[14], (N_BUCKETS, N_HEADS_DSA), 0.5),
        "w_out": nrm(ks[15], (L, MIX_WIDTH, D_MODEL), BETA * MIX_WIDTH ** -0.5),
        "ln2_g": gain(ks[16], (L, D_MODEL)),
        "ln2_b": nrm(ks[17], (L, D_MODEL), 0.02),
        "ffn2_w_gate": nrm(ks[18], (L, D_MODEL, D_FF), D_MODEL ** -0.5),
        "ffn2_w_up": nrm(ks[19], (L, D_MODEL, D_FF), D_MODEL ** -0.5),
        "ffn2_w_down": nrm(ks[20], (L, D_FF, D_MODEL), BETA * D_FF ** -0.5),
        "ln3_g": gain(ks[21], (L, D_MODEL)),
        "ln3_b": nrm(ks[22], (L, D_MODEL), 0.02),
    }


def reference(x, ffn1_w_gate, ffn1_w_up, ffn1_w_down, ln1_g, ln1_b, w_in, b_f, kv_norm_g,
              idx_k_g, idx_k_b, w_uk, w_uv, rel_bias, w_out, ln2_g, ln2_b,
              ffn2_w_gate, ffn2_w_up, ffn2_w_down, ln3_g, ln3_b):
    h = x
    for l in range(DEPTH):
        h = layer_norm(ALPHA * h + 0.5 * swiglu(h, ffn1_w_gate[l], ffn1_w_up[l], ffn1_w_down[l]),
                       ln1_g[l], ln1_b[l])
        h = layer_norm(ALPHA * h + hybrid_mixer(h, w_in[l], b_f[l], kv_norm_g[l], idx_k_g[l],
                                                idx_k_b[l], w_uk[l], w_uv[l], rel_bias, w_out[l]),
                       ln2_g[l], ln2_b[l])
        h = layer_norm(ALPHA * h + 0.5 * swiglu(h, ffn2_w_gate[l], ffn2_w_up[l], ffn2_w_down[l]),
                       ln3_g[l], ln3_b[l])
    return h
```

```python
import functools
import math

import jax
import jax.numpy as jnp
import numpy as np
from jax import lax
from jax.experimental import pallas as pl
from jax.experimental.pallas import tpu as pltpu

CHUNK = 64
Q_BLOCK = 128
HEAD_DIM = 128
IDX_DIM = 128
TOPK_MAX = 256
N_BUCKETS = 32
MAX_DISTANCE = 128
LN_EPS = 1e-5
RMS_EPS = 1e-6
NEG = -1e30

LANES = 128
VMEM_LIMIT_BYTES = 56 * 1024 * 1024

BF16 = jnp.bfloat16
F32 = jnp.float32

INT_MIN = -(2 ** 31)
CODE_NEG_INF = -(2 ** 31) + 0x7FFFFF
F32_LOWEST = float(np.finfo(np.float32).min)


def _tile(dim, pref):
    t = min(dim, pref)
    while dim % t:
        t //= 2
    return t


def _params(sem):
    return pltpu.CompilerParams(dimension_semantics=sem, vmem_limit_bytes=VMEM_LIMIT_BYTES)


def _layer_norm_rows(y, g, b):
    mu = jnp.mean(y, axis=-1, keepdims=True)
    d = y - mu
    var = jnp.mean(d * d, axis=-1, keepdims=True)
    return d * lax.rsqrt(var + LN_EPS) * g + b


def _ln_epilogue(h_ref, hb_ref, g_ref, b_ref, rows):
    g = g_ref[...]
    b = b_ref[...]

    def body(r, carry):
        r0 = pl.multiple_of(r * rows, rows)
        out = _layer_norm_rows(h_ref[pl.ds(r0, rows), :], g, b)
        h_ref[pl.ds(r0, rows), :] = out
        hb_ref[pl.ds(r0, rows), :] = out.astype(BF16)
        return carry

    lax.fori_loop(0, h_ref.shape[0] // rows, body, 0)


def _ffn_kernel(x_ref, wg_ref, wu_ref, wd_ref, g_ref, b_ref, h_ref, hb_ref, xb_ref, *,
                alpha, n_f, rows, col_chunk):
    f = pl.program_id(1)
    tm, d = x_ref.shape

    @pl.when(f == 0)
    def _():
        def body(r, carry):
            r0 = pl.multiple_of(r * rows, rows)
            xr = x_ref[pl.ds(r0, rows), :]
            xb_ref[pl.ds(r0, rows), :] = xr.astype(BF16)
            h_ref[pl.ds(r0, rows), :] = alpha * xr
            return carry
        lax.fori_loop(0, tm // rows, body, 0)

    xb = xb_ref[...]
    gate = jnp.dot(xb, wg_ref[...], preferred_element_type=F32)
    up = jnp.dot(xb, wu_ref[...], preferred_element_type=F32)
    act = (0.5 * (gate * jax.nn.sigmoid(gate)) * up).astype(BF16)
    for c in range(d // col_chunk):
        sl = slice(c * col_chunk, (c + 1) * col_chunk)
        h_ref[:, sl] += jnp.dot(act, wd_ref[:, sl], preferred_element_type=F32)

    @pl.when(f == n_f - 1)
    def _():
        _ln_epilogue(h_ref, hb_ref, g_ref, b_ref, rows)


def _ffn_ln(x, wg, wu, wd, g, b, alpha):
    m, d = x.shape
    f = wg.shape[1]
    tm = _tile(m, 512)
    tf = _tile(f, 256)
    n_f = f // tf
    kern = functools.partial(_ffn_kernel, alpha=alpha, n_f=n_f, rows=_tile(tm, 32),
                             col_chunk=_tile(d, 512))
    return pl.pallas_call(
        kern,
        grid=(m // tm, n_f),
        in_specs=[
            pl.BlockSpec((tm, d), lambda i, j: (i, 0), pipeline_mode=pl.Buffered(1)),
            pl.BlockSpec((d, tf), lambda i, j: (0, j)),
            pl.BlockSpec((d, tf), lambda i, j: (0, j)),
            pl.BlockSpec((tf, d), lambda i, j: (j, 0)),
            pl.BlockSpec((1, d), lambda i, j: (0, 0)),
            pl.BlockSpec((1, d), lambda i, j: (0, 0)),
        ],
        out_specs=[
            pl.BlockSpec((tm, d), lambda i, j: (i, 0)),
            pl.BlockSpec((tm, d), lambda i, j: (i, 0)),
        ],
        out_shape=[jax.ShapeDtypeStruct((m, d), F32), jax.ShapeDtypeStruct((m, d), BF16)],
        scratch_shapes=[pltpu.VMEM((tm, d), BF16)],
        compiler_params=_params(("parallel", "arbitrary")),
        name="ffn_ln",
    )(x, wg, wu, wd, g, b)


def _proj_kernel(x_ref, w_ref, o_ref):
    o_ref[...] = jnp.dot(x_ref[...], w_ref[...], preferred_element_type=F32).astype(o_ref.dtype)


def _proj(xb, w):
    m, d = xb.shape
    n = w.shape[1]
    tm = _tile(m, 1024)
    tn = _tile(n, 512)
    return pl.pallas_call(
        _proj_kernel,
        grid=(m // tm, n // tn),
        in_specs=[
            pl.BlockSpec((tm, d), lambda i, j: (i, 0)),
            pl.BlockSpec((d, tn), lambda i, j: (0, j)),
        ],
        out_specs=pl.BlockSpec((tm, tn), lambda i, j: (i, j)),
        out_shape=jax.ShapeDtypeStruct((m, n), BF16),
        compiler_params=_params(("parallel", "arbitrary")),
        name="proj_main",
    )(xb, w)


def _split3(v):
    hi = v.astype(BF16)
    r1 = v - hi.astype(F32)
    mid = r1.astype(BF16)
    lo = (r1 - mid.astype(F32)).astype(BF16)
    return hi, mid, lo


def _proj_small_kernel(x_ref, ws_ref, wft_ref, bf_ref, kvg_ref, kg_ref, kb_ref,
                       ckv_ref, ki_ref, wi_ref, cum_ref, carry_ref, *, c_lat, n_ih, tiles_per_seq):
    i = pl.program_id(0)
    x = x_ref[...]
    tm = x.shape[0]
    y = jnp.dot(x, ws_ref[...], preferred_element_type=F32)
    c = y[:, :c_lat]
    c = c * lax.rsqrt(jnp.mean(c * c, axis=-1, keepdims=True) + RMS_EPS) * kvg_ref[...]
    ckv_ref[...] = c.astype(BF16)
    k = y[:, c_lat:c_lat + IDX_DIM]
    ki_ref[...] = _layer_norm_rows(k, kg_ref[...], kb_ref[...]).astype(BF16)
    wi_ref[...] = y[:, c_lat + IDX_DIM:c_lat + IDX_DIM + n_ih]

    ft = lax.dot_general(wft_ref[...], x, (((1,), (1,)), ((), ())), preferred_element_type=F32)
    z = ft + bf_ref[...]
    log_f = jnp.minimum(z, 0.0) - jnp.log1p(jnp.exp(-jnp.abs(z)))

    @pl.when(i % tiles_per_seq == 0)
    def _():
        carry_ref[...] = jnp.zeros_like(carry_ref)

    row = lax.broadcasted_iota(jnp.int32, (tm, tm), 0)
    col = lax.broadcasted_iota(jnp.int32, (tm, tm), 1)
    tri = jnp.where(row <= col, 1.0, 0.0).astype(BF16)
    hi, mid, lo = _split3(log_f)
    cs = (jnp.dot(hi, tri, preferred_element_type=F32)
          + jnp.dot(mid, tri, preferred_element_type=F32)
          + jnp.dot(lo, tri, preferred_element_type=F32))
    cum = cs + carry_ref[:, 0:1]
    cum_ref[...] = cum
    carry_ref[...] = jnp.broadcast_to(cum[:, tm - 1:tm], carry_ref.shape)


def _proj_small(xb, ws, wft, bf, kvg, kg, kb, *, c_lat, n_ih, seq):
    m, d = xb.shape
    ns = ws.shape[1]
    hf = wft.shape[0]
    tm = _tile(seq, 512)
    kern = functools.partial(_proj_small_kernel, c_lat=c_lat, n_ih=n_ih, tiles_per_seq=seq // tm)
    return pl.pallas_call(
        kern,
        grid=(m // tm,),
        in_specs=[
            pl.BlockSpec((tm, d), lambda i: (i, 0)),
            pl.BlockSpec((d, ns), lambda i: (0, 0)),
            pl.BlockSpec((hf, d), lambda i: (0, 0)),
            pl.BlockSpec((hf, 1), lambda i: (0, 0)),
            pl.BlockSpec((1, c_lat), lambda i: (0, 0)),
            pl.BlockSpec((1, IDX_DIM), lambda i: (0, 0)),
            pl.BlockSpec((1, IDX_DIM), lambda i: (0, 0)),
        ],
        out_specs=[
            pl.BlockSpec((tm, c_lat), lambda i: (i, 0)),
            pl.BlockSpec((tm, IDX_DIM), lambda i: (i, 0)),
            pl.BlockSpec((tm, n_ih), lambda i: (i, 0)),
            pl.BlockSpec((hf, tm), lambda i: (0, i)),
        ],
        out_shape=[
            jax.ShapeDtypeStruct((m, c_lat), BF16),
            jax.ShapeDtypeStruct((m, IDX_DIM), BF16),
            jax.ShapeDtypeStruct((m, n_ih), F32),
            jax.ShapeDtypeStruct((hf, m), F32),
        ],
        scratch_shapes=[pltpu.VMEM((hf, LANES), F32)],
        compiler_params=_params(("arbitrary",)),
        name="proj_small",
    )(xb, ws, wft, bf, kvg, kg, kb)


def _dsa_kernel(qi_ref, qa_ref, wi_ref, ki_ref, ckv_ref, wuk_ref, wuv_ref, tab_ref, bfar_ref,
                o_ref, score_ref, qlat_ref, acc_ref, m_ref, l_ref, p_ref, alpha_ref, *,
                n_ih, n_h, k_sel, tk):
    i = pl.program_id(1)
    start = i * Q_BLOCK
    qb = Q_BLOCK
    bpc = tk // LANES
    scale = HEAD_DIM ** -0.5
    w_fold = (IDX_DIM ** -0.5) * (n_ih ** -0.5)

    wcol = wi_ref[...] * w_fold
    row = lax.broadcasted_iota(jnp.int32, (qb, tk), 0)
    col = lax.broadcasted_iota(jnp.int32, (qb, tk), 1)
    limit = start + jnp.where(row < CHUNK, CHUNK, 2 * CHUNK)
    n_chunks = (i + bpc) // bpc

    def score_chunk(c, carry):
        k0 = pl.multiple_of(c * tk, tk)
        kblk = ki_ref[pl.ds(k0, tk), :]
        sc = jnp.zeros((qb, tk), F32)
        for h in range(n_ih):
            s = lax.dot_general(qi_ref[:, h * IDX_DIM:(h + 1) * IDX_DIM], kblk,
                                (((1,), (1,)), ((), ())), preferred_element_type=F32)
            sc = sc + jnp.maximum(s, 0.0) * wcol[:, h:h + 1]
        sc = jnp.where(col + k0 < limit, sc, -jnp.inf)
        for j in range(bpc):
            score_ref[c * bpc + j] = sc[:, j * LANES:(j + 1) * LANES]
        return carry

    lax.fori_loop(0, n_chunks, score_chunk, 0)

    def code_to_float(code):
        bits = jnp.where(code >= 0, code, code ^ 0x7FFFFFFF)
        return lax.bitcast_convert_type(bits, F32)

    def bit_body(bi, code):
        cand = code + lax.shift_left(jnp.int32(1), 31 - bi)
        cf = code_to_float(cand)

        def count_block(blk, cnt):
            return cnt + jnp.where(score_ref[blk] >= cf, 1.0, 0.0)

        cnt = lax.fori_loop(0, i + 1, count_block, jnp.zeros((qb, LANES), F32))
        total = jnp.sum(cnt, axis=-1, keepdims=True)
        return jnp.where(total >= k_sel, cand, code)

    code = lax.fori_loop(0, 32, bit_body, jnp.full((qb, 1), INT_MIN, jnp.int32))
    thr = jnp.where(code <= CODE_NEG_INF, F32_LOWEST, code_to_float(code))

    for h in range(n_h):
        ql = jnp.dot(qa_ref[:, h * HEAD_DIM:(h + 1) * HEAD_DIM], wuk_ref[h],
                     preferred_element_type=F32)
        qlat_ref[h * qb:(h + 1) * qb, :] = ql.astype(BF16)

    m_ref[...] = jnp.full(m_ref.shape, NEG, F32)
    l_ref[...] = jnp.zeros(l_ref.shape, F32)
    acc_ref[...] = jnp.zeros(acc_ref.shape, F32)

    def flash_step(k0, width, sel, bias_of_head):
        cblk = ckv_ref[pl.ds(k0, width), :]
        s = lax.dot_general(qlat_ref[...], cblk, (((1,), (1,)), ((), ())),
                            preferred_element_type=F32)
        for h in range(n_h):
            rs = slice(h * qb, (h + 1) * qb)
            sh = jnp.where(sel, s[rs, :] * scale + bias_of_head(h), NEG)
            m_old = m_ref[rs, :]
            m_new = jnp.maximum(m_old, jnp.max(sh, axis=-1, keepdims=True))
            a = jnp.exp(m_old - m_new)
            p = jnp.exp(sh - m_new)
            l_ref[rs, :] = a * l_ref[rs, :] + jnp.sum(p, axis=-1, keepdims=True)
            m_ref[rs, :] = m_new
            alpha_ref[rs, :] = a
            p_ref[rs, 0:width] = p.astype(BF16)
        pv = jnp.dot(p_ref[:, 0:width], cblk, preferred_element_type=F32)
        acc_ref[...] = acc_ref[...] * alpha_ref[...] + pv

    far_end = (i - 1) * qb

    def far_chunk(c, carry):
        k0 = pl.multiple_of(c * tk, tk)
        sc = jnp.concatenate([score_ref[c * bpc + j] for j in range(bpc)], axis=-1)
        sel = (sc >= thr) & (col + k0 < far_end)
        flash_step(k0, tk, sel, lambda h: bfar_ref[h])
        return carry

    n_far = jnp.maximum((i - 1 + bpc - 1) // bpc, 0)
    lax.fori_loop(0, n_far, far_chunk, 0)

    wb = jnp.maximum(i - 1, 0)
    tsel = jnp.where(i == 0, 1, 0)
    scw = jnp.concatenate([score_ref[wb], score_ref[wb + 1]], axis=-1)
    selw = scw >= thr
    flash_step(pl.multiple_of(wb * LANES, LANES), 2 * LANES, selw, lambda h: tab_ref[tsel, h])

    for h in range(n_h):
        rs = slice(h * qb, (h + 1) * qb)
        o_lat = (acc_ref[rs, :] / l_ref[rs, :]).astype(BF16)
        o_ref[:, h * HEAD_DIM:(h + 1) * HEAD_DIM] = jnp.dot(
            o_lat, wuv_ref[h], preferred_element_type=F32).astype(o_ref.dtype)


def _dsa(qkv, wi, ki, ckv, wuk, wuv, tab, bfar, *, batch, seq, n_ih, n_h, qi_blk, qa_blk):
    m = qkv.shape[0]
    c_lat = ckv.shape[1]
    nq = seq // Q_BLOCK
    k_sel = min(TOPK_MAX, seq // 4)
    tk = _tile(seq, 512)
    rows = n_h * Q_BLOCK
    kern = functools.partial(_dsa_kernel, n_ih=n_ih, n_h=n_h, k_sel=k_sel, tk=tk)
    const = dict(pipeline_mode=pl.Buffered(1))
    return pl.pallas_call(
        kern,
        grid=(batch, nq),
        in_specs=[
            pl.BlockSpec((Q_BLOCK, n_ih * IDX_DIM), lambda b, i: (b * nq + i, qi_blk)),
            pl.BlockSpec((Q_BLOCK, n_h * HEAD_DIM), lambda b, i: (b * nq + i, qa_blk)),
            pl.BlockSpec((Q_BLOCK, n_ih), lambda b, i: (b * nq + i, 0)),
            pl.BlockSpec((seq, IDX_DIM), lambda b, i: (b, 0)),
            pl.BlockSpec((seq, c_lat), lambda b, i: (b, 0)),
            pl.BlockSpec((n_h, HEAD_DIM, c_lat), lambda b, i: (0, 0, 0), **const),
            pl.BlockSpec((n_h, c_lat, HEAD_DIM), lambda b, i: (0, 0, 0), **const),
            pl.BlockSpec((2, n_h, Q_BLOCK, 2 * LANES), lambda b, i: (0, 0, 0, 0), **const),
            pl.BlockSpec(memory_space=pltpu.SMEM),
        ],
        out_specs=pl.BlockSpec((Q_BLOCK, n_h * HEAD_DIM), lambda b, i: (b * nq + i, 0)),
        out_shape=jax.ShapeDtypeStruct((m, n_h * HEAD_DIM), BF16),
        scratch_shapes=[
            pltpu.VMEM((seq // LANES, Q_BLOCK, LANES), F32),
            pltpu.VMEM((rows, c_lat), BF16),
            pltpu.VMEM((rows, c_lat), F32),
            pltpu.VMEM((rows, 1), F32),
            pltpu.VMEM((rows, 1), F32),
            pltpu.VMEM((rows, tk), BF16),
            pltpu.VMEM((rows, 1), F32),
        ],
        compiler_params=_params(("parallel", "arbitrary")),
        name="dsa_mixer",
    )(qkv, qkv, wi, ki, ckv, wuk, wuv, tab, bfar)


def _fox_kernel(q_ref, k_ref, v_ref, cq_ref, ck_ref, o_ref, acc_ref, m_ref, l_ref, *, tq):
    h = pl.program_id(1)
    i = pl.program_id(2)
    scale = HEAD_DIM ** -0.5
    q = q_ref[...]
    hsel = lax.broadcasted_iota(jnp.int32, cq_ref.shape, 1) == h
    cq = jnp.sum(jnp.where(hsel, cq_ref[...], 0.0), axis=-1, keepdims=True)

    m_ref[...] = jnp.full(m_ref.shape, NEG, F32)
    l_ref[...] = jnp.zeros(l_ref.shape, F32)
    acc_ref[...] = jnp.zeros(acc_ref.shape, F32)

    def step(kb, causal):
        k0 = pl.multiple_of(kb * tq, tq)
        kblk = k_ref[pl.ds(k0, tq), :]
        vblk = v_ref[pl.ds(k0, tq), :]
        ck = ck_ref[pl.ds(h, 1), pl.ds(kb, 1), :].reshape(1, tq)
        s = lax.dot_general(q, kblk, (((1,), (1,)), ((), ())), preferred_element_type=F32)
        s = s * scale + (cq - ck)
        if causal:
            row = lax.broadcasted_iota(jnp.int32, (tq, tq), 0)
            col = lax.broadcasted_iota(jnp.int32, (tq, tq), 1)
            s = jnp.where(col <= row, s, NEG)
        m_old = m_ref[...]
        m_new = jnp.maximum(m_old, jnp.max(s, axis=-1, keepdims=True))
        a = jnp.exp(m_old - m_new)
        p = jnp.exp(s - m_new)
        l_ref[...] = a * l_ref[...] + jnp.sum(p, axis=-1, keepdims=True)
        m_ref[...] = m_new
        acc_ref[...] = acc_ref[...] * a + jnp.dot(p.astype(BF16), vblk, preferred_element_type=F32)

    def full_block(kb, carry):
        step(kb, False)
        return carry

    lax.fori_loop(0, i, full_block, 0)
    step(i, True)
    o_ref[...] = (acc_ref[...] / l_ref[...]).astype(o_ref.dtype)


def _fox(qkv, cum_tok, cum_blk, *, batch, seq, n_h, q_blk0, k_blk0, v_blk0):
    m = qkv.shape[0]
    tq = _tile(seq, 512)
    nq = seq // tq
    kern = functools.partial(_fox_kernel, tq=tq)
    return pl.pallas_call(
        kern,
        grid=(batch, n_h, nq),
        in_specs=[
            pl.BlockSpec((tq, HEAD_DIM), lambda b, h, i: (b * nq + i, q_blk0 + h)),
            pl.BlockSpec((seq, HEAD_DIM), lambda b, h, i: (b, k_blk0 + h)),
            pl.BlockSpec((seq, HEAD_DIM), lambda b, h, i: (b, v_blk0 + h)),
            pl.BlockSpec((tq, n_h), lambda b, h, i: (b * nq + i, 0)),
            pl.BlockSpec((n_h, nq, tq), lambda b, h, i: (0, b, 0)),
        ],
        out_specs=pl.BlockSpec((tq, HEAD_DIM), lambda b, h, i: (b * nq + i, h)),
        out_shape=jax.ShapeDtypeStruct((m, n_h * HEAD_DIM), BF16),
        scratch_shapes=[
            pltpu.VMEM((tq, HEAD_DIM), F32),
            pltpu.VMEM((tq, 1), F32),
            pltpu.VMEM((tq, 1), F32),
        ],
        compiler_params=_params(("parallel", "parallel", "arbitrary")),
        name="fox_mixer",
    )(qkv, qkv, qkv, cum_tok, cum_blk)


def _outproj_kernel(oa_ref, ob_ref, w_ref, res_ref, g_ref, b_ref, h_ref, hb_ref, *,
                    alpha, n_ka, n_k, rows, col_chunk):
    k = pl.program_id(1)
    tm, d = h_ref.shape

    @pl.when(k == 0)
    def _():
        def body(r, carry):
            r0 = pl.multiple_of(r * rows, rows)
            h_ref[pl.ds(r0, rows), :] = alpha * res_ref[pl.ds(r0, rows), :]
            return carry
        lax.fori_loop(0, tm // rows, body, 0)

    def accumulate(src_ref):
        src = src_ref[...]
        for c in range(d // col_chunk):
            sl = slice(c * col_chunk, (c + 1) * col_chunk)
            h_ref[:, sl] += jnp.dot(src, w_ref[:, sl], preferred_element_type=F32)

    @pl.when(k < n_ka)
    def _():
        accumulate(oa_ref)

    @pl.when(k >= n_ka)
    def _():
        accumulate(ob_ref)

    @pl.when(k == n_k - 1)
    def _():
        _ln_epilogue(h_ref, hb_ref, g_ref, b_ref, rows)


def _outproj_ln(oa, ob, w, res, g, b, alpha):
    m, wa = oa.shape
    wbw = ob.shape[1]
    d = w.shape[1]
    tm = _tile(m, 512)
    tk = _tile(math.gcd(wa, wbw), 512)
    n_ka, n_kb = wa // tk, wbw // tk
    n_k = n_ka + n_kb
    kern = functools.partial(_outproj_kernel, alpha=alpha, n_ka=n_ka, n_k=n_k,
                             rows=_tile(tm, 32), col_chunk=_tile(d, 512))
    return pl.pallas_call(
        kern,
        grid=(m // tm, n_k),
        in_specs=[
            pl.BlockSpec((tm, tk), lambda i, k: (i, jnp.minimum(k, n_ka - 1))),
            pl.BlockSpec((tm, tk), lambda i, k: (i, jnp.maximum(k - n_ka, 0))),
            pl.BlockSpec((tk, d), lambda i, k: (k, 0)),
            pl.BlockSpec((tm, d), lambda i, k: (i, 0), pipeline_mode=pl.Buffered(1)),
            pl.BlockSpec((1, d), lambda i, k: (0, 0)),
            pl.BlockSpec((1, d), lambda i, k: (0, 0)),
        ],
        out_specs=[
            pl.BlockSpec((tm, d), lambda i, k: (i, 0)),
            pl.BlockSpec((tm, d), lambda i, k: (i, 0)),
        ],
        out_shape=[jax.ShapeDtypeStruct((m, d), F32), jax.ShapeDtypeStruct((m, d), BF16)],
        compiler_params=_params(("parallel", "arbitrary")),
        name="outproj_ln",
    )(oa, ob, w, res, g, b)


def _t5_bucket(rel):
    nb = N_BUCKETS // 2
    max_exact = nb // 2
    ret = jnp.where(rel > 0, nb, 0)
    n = jnp.abs(rel)
    nf = jnp.maximum(n, 1).astype(jnp.float32)
    large = max_exact + (jnp.log(nf / max_exact) / math.log(MAX_DISTANCE / max_exact)
                         * (nb - max_exact)).astype(jnp.int32)
    large = jnp.minimum(large, nb - 1)
    return ret + jnp.where(n < max_exact, n, large)


def _bias_tables(rel_bias):
    r = jnp.arange(Q_BLOCK, dtype=jnp.int32)[:, None]
    j = jnp.arange(2 * LANES, dtype=jnp.int32)[None, :]
    rel = jnp.stack([j - LANES - r, j - r])
    tab = jnp.moveaxis(rel_bias[_t5_bucket(rel)], -1, 1).astype(F32)
    far = rel_bias[_t5_bucket(jnp.full((), -(MAX_DISTANCE + 1), jnp.int32))].astype(F32)
    return tab, far


def _hybrid_mixer_ln(h, hb, w_in, b_f, kv_norm_g, idx_k_g, idx_k_b, w_uk, w_uv, rel_bias, w_out,
                     ln_g, ln_b, alpha, batch, seq):
    m, d = h.shape
    n_h = w_uk.shape[0]
    c_lat = w_uk.shape[2]
    n_hf = b_f.shape[0]
    w_dsa = n_h * HEAD_DIM
    w_fox = n_hf * HEAD_DIM
    d_in = w_in.shape[1]
    n_ih = (d_in - w_dsa - c_lat - IDX_DIM - 3 * w_fox - n_hf) // (IDX_DIM + 1)
    w_qi = n_ih * IDX_DIM
    assert w_dsa + c_lat + w_qi + IDX_DIM + n_ih + 3 * w_fox + n_hf == d_in
    assert w_qi % w_dsa == 0 and seq % (2 * LANES) == 0

    o0 = 0
    w_qa = w_in[:, o0:o0 + w_dsa]; o0 += w_dsa
    w_ckv = w_in[:, o0:o0 + c_lat]; o0 += c_lat
    w_qidx = w_in[:, o0:o0 + w_qi]; o0 += w_qi
    w_kidx = w_in[:, o0:o0 + IDX_DIM]; o0 += IDX_DIM
    w_widx = w_in[:, o0:o0 + n_ih]; o0 += n_ih
    w_qkv_fox = w_in[:, o0:o0 + 3 * w_fox]; o0 += 3 * w_fox
    w_f = w_in[:, o0:o0 + n_hf]

    w_main = jnp.concatenate([w_qidx, w_qa, w_qkv_fox], axis=1).astype(BF16)
    qkv = _proj(hb, w_main)

    n_small = c_lat + IDX_DIM + n_ih
    pad = (-n_small) % LANES
    w_small = jnp.pad(jnp.concatenate([w_ckv, w_kidx, w_widx], axis=1), ((0, 0), (0, pad))).astype(BF16)
    ckv, ki, wi, cum_t = _proj_small(
        hb, w_small, w_f.T.astype(BF16), b_f.reshape(n_hf, 1).astype(F32),
        kv_norm_g.reshape(1, c_lat), idx_k_g.reshape(1, IDX_DIM), idx_k_b.reshape(1, IDX_DIM),
        c_lat=c_lat, n_ih=n_ih, seq=seq)

    tab, far = _bias_tables(rel_bias)
    o_a = _dsa(qkv, wi, ki, ckv, w_uk.astype(BF16), w_uv.astype(BF16), tab, far,
               batch=batch, seq=seq, n_ih=n_ih, n_h=n_h, qi_blk=0, qa_blk=w_qi // w_dsa)

    tq = _tile(seq, 512)
    blk0 = (w_qi + w_dsa) // HEAD_DIM
    o_b = _fox(qkv, cum_t.T, cum_t.reshape(n_hf, m // tq, tq), batch=batch, seq=seq, n_h=n_hf,
               q_blk0=blk0, k_blk0=blk0 + n_hf, v_blk0=blk0 + 2 * n_hf)

    return _outproj_ln(o_a, o_b, w_out.astype(BF16), h, ln_g.reshape(1, d), ln_b.reshape(1, d), alpha)


def kernel(x, ffn1_w_gate, ffn1_w_up, ffn1_w_down, ln1_g, ln1_b, w_in, b_f, kv_norm_g, idx_k_g,
           idx_k_b, w_uk, w_uv, rel_bias, w_out, ln2_g, ln2_b, ffn2_w_gate, ffn2_w_up,
           ffn2_w_down, ln3_g, ln3_b):
    batch, seq, d = x.shape
    depth = ffn1_w_gate.shape[0]
    alpha = (2.0 * depth) ** 0.25
    h = x.reshape(batch * seq, d)
    for l in range(depth):
        h, hb = _ffn_ln(h, ffn1_w_gate[l].astype(BF16), ffn1_w_up[l].astype(BF16),
                        ffn1_w_down[l].astype(BF16), ln1_g[l].reshape(1, d), ln1_b[l].reshape(1, d),
                        alpha)
        h, hb = _hybrid_mixer_ln(h, hb, w_in[l], b_f[l], kv_norm_g[l], idx_k_g[l], idx_k_b[l],
                                 w_uk[l], w_uv[l], rel_bias, w_out[l], ln2_g[l], ln2_b[l], alpha,
                                 batch, seq)
        h, hb = _ffn_ln(h, ffn2_w_gate[l].astype(BF16), ffn2_w_up[l].astype(BF16),
                        ffn2_w_down[l].astype(BF16), ln3_g[l].reshape(1, d), ln3_b[l].reshape(1, d),
                        alpha)
    return h.reshape(batch, seq, d)
```

```python
import functools
import math

import jax
import jax.numpy as jnp
import numpy as np
from jax import lax
from jax.experimental import pallas as pl
from jax.experimental.pallas import tpu as pltpu

CHUNK = 64
Q_BLOCK = 128
HEAD_DIM = 128
IDX_DIM = 128
TOPK_MAX = 256
N_BUCKETS = 32
MAX_DISTANCE = 128
LN_EPS = 1e-5
RMS_EPS = 1e-6
NEG = -1e30
LOG2E = math.log2(math.e)
DSA_HEADS_PER_GROUP = 4
FOX_HEADS_PER_BLOCK = 4

LANES = 128
VMEM_LIMIT_BYTES = 56 * 1024 * 1024

BF16 = jnp.bfloat16
F32 = jnp.float32

INT_MIN = -(2 ** 31)
CODE_NEG_INF = -(2 ** 31) + 0x7FFFFF
F32_LOWEST = float(np.finfo(np.float32).min)


def _tile(dim, pref):
    t = min(dim, pref)
    while dim % t:
        t //= 2
    return t


def _params(sem):
    return pltpu.CompilerParams(dimension_semantics=sem, vmem_limit_bytes=VMEM_LIMIT_BYTES)


def _widen(x, width):
    reps = width // x.shape[1]
    return x if reps == 1 else jnp.concatenate([x] * reps, axis=-1)


def _layer_norm_rows(y, g, b):
    mu = jnp.mean(y, axis=-1, keepdims=True)
    d = y - mu
    var = jnp.mean(d * d, axis=-1, keepdims=True)
    return d * lax.rsqrt(var + LN_EPS) * g + b


def _ln_epilogue(h_ref, hb_ref, g_ref, b_ref, rows):
    g = g_ref[...]
    b = b_ref[...]

    def body(r, carry):
        r0 = pl.multiple_of(r * rows, rows)
        out = _layer_norm_rows(h_ref[pl.ds(r0, rows), :], g, b)
        h_ref[pl.ds(r0, rows), :] = out
        hb_ref[pl.ds(r0, rows), :] = out.astype(BF16)
        return carry

    lax.fori_loop(0, h_ref.shape[0] // rows, body, 0)


def _ffn_kernel(x_hbm, wg_ref, wu_ref, wd_ref, g_ref, b_ref, h_ref, hb_ref, xb_ref, *,
                alpha, n_f, rows, col_chunk):
    i = pl.program_id(0)
    f = pl.program_id(1)
    tm, d = h_ref.shape

    @pl.when(f == 0)
    def _():
        pltpu.sync_copy(x_hbm.at[pl.ds(pl.multiple_of(i * tm, tm), tm), :], h_ref)

        def body(r, carry):
            r0 = pl.multiple_of(r * rows, rows)
            xr = h_ref[pl.ds(r0, rows), :]
            xb_ref[pl.ds(r0, rows), :] = xr.astype(BF16)
            h_ref[pl.ds(r0, rows), :] = alpha * xr
            return carry
        lax.fori_loop(0, tm // rows, body, 0)

    xb = xb_ref[...]
    gate = jnp.dot(xb, wg_ref[...], preferred_element_type=F32)
    up = jnp.dot(xb, wu_ref[...], preferred_element_type=F32)
    act = (0.5 * (gate * jax.nn.sigmoid(gate)) * up).astype(BF16)
    for c in range(d // col_chunk):
        sl = slice(c * col_chunk, (c + 1) * col_chunk)
        h_ref[:, sl] += jnp.dot(act, wd_ref[:, sl], preferred_element_type=F32)

    @pl.when(f == n_f - 1)
    def _():
        _ln_epilogue(h_ref, hb_ref, g_ref, b_ref, rows)


def _ffn_ln(x, wg, wu, wd, g, b, alpha):
    m, d = x.shape
    f = wg.shape[1]
    tm = _tile(m, 1024)
    tf = _tile(f, 256)
    n_f = f // tf
    kern = functools.partial(_ffn_kernel, alpha=alpha, n_f=n_f, rows=_tile(tm, 32),
                             col_chunk=_tile(d, 512))
    single = dict(pipeline_mode=pl.Buffered(1))
    return pl.pallas_call(
        kern,
        grid=(m // tm, n_f),
        in_specs=[
            pl.BlockSpec(memory_space=pl.ANY),
            pl.BlockSpec((d, tf), lambda i, j: (0, j)),
            pl.BlockSpec((d, tf), lambda i, j: (0, j)),
            pl.BlockSpec((tf, d), lambda i, j: (j, 0)),
            pl.BlockSpec((1, d), lambda i, j: (0, 0)),
            pl.BlockSpec((1, d), lambda i, j: (0, 0)),
        ],
        out_specs=[
            pl.BlockSpec((tm, d), lambda i, j: (i, 0), **single),
            pl.BlockSpec((tm, d), lambda i, j: (i, 0), **single),
        ],
        out_shape=[jax.ShapeDtypeStruct((m, d), F32), jax.ShapeDtypeStruct((m, d), BF16)],
        scratch_shapes=[pltpu.VMEM((tm, d), BF16)],
        compiler_params=_params(("parallel", "arbitrary")),
        name="ffn_ln",
    )(x, wg, wu, wd, g, b)


def _proj_kernel(x_ref, w_ref, o_ref):
    o_ref[...] = jnp.dot(x_ref[...], w_ref[...], preferred_element_type=F32).astype(o_ref.dtype)


def _proj(xb, w):
    m, d = xb.shape
    n = w.shape[1]
    tm = _tile(m, 1024)
    tn = _tile(n, 512)
    return pl.pallas_call(
        _proj_kernel,
        grid=(m // tm, n // tn),
        in_specs=[
            pl.BlockSpec((tm, d), lambda i, j: (i, 0)),
            pl.BlockSpec((d, tn), lambda i, j: (0, j)),
        ],
        out_specs=pl.BlockSpec((tm, tn), lambda i, j: (i, j)),
        out_shape=jax.ShapeDtypeStruct((m, n), BF16),
        compiler_params=_params(("parallel", "arbitrary")),
        name="proj_main",
    )(xb, w)


def _split3(v):
    hi = v.astype(BF16)
    r1 = v - hi.astype(F32)
    mid = r1.astype(BF16)
    lo = (r1 - mid.astype(F32)).astype(BF16)
    return hi, mid, lo


def _proj_small_kernel(x_ref, ws_ref, wft_ref, bf_ref, kvg_ref, kg_ref, kb_ref,
                       ckv_ref, ki_ref, wi_ref, cum_ref, carry_ref, *, c_lat, n_ih, tiles_per_seq):
    i = pl.program_id(0)
    x = x_ref[...]
    tm = x.shape[0]
    y = jnp.dot(x, ws_ref[...], preferred_element_type=F32)
    c = y[:, :c_lat]
    c = c * lax.rsqrt(jnp.mean(c * c, axis=-1, keepdims=True) + RMS_EPS) * kvg_ref[...]
    ckv_ref[...] = c.astype(BF16)
    k = y[:, c_lat:c_lat + IDX_DIM]
    ki_ref[...] = _layer_norm_rows(k, kg_ref[...], kb_ref[...]).astype(BF16)
    wi_ref[...] = y[:, c_lat + IDX_DIM:c_lat + IDX_DIM + n_ih]

    ft = lax.dot_general(wft_ref[...], x, (((1,), (1,)), ((), ())), preferred_element_type=F32)
    z = ft + bf_ref[...]
    log_f = jnp.minimum(z, 0.0) - jnp.log1p(jnp.exp(-jnp.abs(z)))

    @pl.when(i % tiles_per_seq == 0)
    def _():
        carry_ref[...] = jnp.zeros_like(carry_ref)

    row = lax.broadcasted_iota(jnp.int32, (tm, tm), 0)
    col = lax.broadcasted_iota(jnp.int32, (tm, tm), 1)
    tri = jnp.where(row <= col, 1.0, 0.0).astype(BF16)
    hi, mid, lo = _split3(log_f)
    cs = (jnp.dot(hi, tri, preferred_element_type=F32)
          + jnp.dot(mid, tri, preferred_element_type=F32)
          + jnp.dot(lo, tri, preferred_element_type=F32))
    cum = cs + carry_ref[:, 0:1]
    cum_ref[...] = cum
    carry_ref[...] = jnp.broadcast_to(cum[:, tm - 1:tm], carry_ref.shape)


def _proj_small(xb, ws, wft, bf, kvg, kg, kb, *, c_lat, n_ih, seq):
    m, d = xb.shape
    ns = ws.shape[1]
    hf = wft.shape[0]
    tm = _tile(seq, 512)
    kern = functools.partial(_proj_small_kernel, c_lat=c_lat, n_ih=n_ih, tiles_per_seq=seq // tm)
    return pl.pallas_call(
        kern,
        grid=(m // tm,),
        in_specs=[
            pl.BlockSpec((tm, d), lambda i: (i, 0)),
            pl.BlockSpec((d, ns), lambda i: (0, 0)),
            pl.BlockSpec((hf, d), lambda i: (0, 0)),
            pl.BlockSpec((hf, 1), lambda i: (0, 0)),
            pl.BlockSpec((1, c_lat), lambda i: (0, 0)),
            pl.BlockSpec((1, IDX_DIM), lambda i: (0, 0)),
            pl.BlockSpec((1, IDX_DIM), lambda i: (0, 0)),
        ],
        out_specs=[
            pl.BlockSpec((tm, c_lat), lambda i: (i, 0)),
            pl.BlockSpec((tm, IDX_DIM), lambda i: (i, 0)),
            pl.BlockSpec((tm, n_ih), lambda i: (i, 0)),
            pl.BlockSpec((hf, tm), lambda i: (0, i)),
        ],
        out_shape=[
            jax.ShapeDtypeStruct((m, c_lat), BF16),
            jax.ShapeDtypeStruct((m, IDX_DIM), BF16),
            jax.ShapeDtypeStruct((m, n_ih), F32),
            jax.ShapeDtypeStruct((hf, m), F32),
        ],
        scratch_shapes=[pltpu.VMEM((hf, LANES), F32)],
        compiler_params=_params(("arbitrary",)),
        name="proj_small",
    )(xb, ws, wft, bf, kvg, kg, kb)


def _dsa_kernel(qi_ref, qa_ref, wi_ref, ki_ref, ckv_ref, wuk_ref, wuv_ref, tab_ref, bfar_ref,
                o_ref, score_ref, qlat_ref, acc_ref, m_ref, l_ref, p_ref, alpha_ref, *,
                n_ih, n_h, k_sel, tk, hpg):
    i = pl.program_id(1)
    start = i * Q_BLOCK
    qb = Q_BLOCK
    bpc = tk // LANES
    scale = HEAD_DIM ** -0.5
    w_fold = (IDX_DIM ** -0.5) * (n_ih ** -0.5)

    wcol = wi_ref[...] * w_fold
    row = lax.broadcasted_iota(jnp.int32, (qb, tk), 0)
    col = lax.broadcasted_iota(jnp.int32, (qb, tk), 1)
    limit = start + jnp.where(row < CHUNK, CHUNK, 2 * CHUNK)
    n_chunks = (i + bpc) // bpc

    def score_chunk(c, carry):
        k0 = pl.multiple_of(c * tk, tk)
        kblk = ki_ref[pl.ds(k0, tk), :]
        sc = jnp.zeros((qb, tk), F32)
        for h in range(n_ih):
            s = lax.dot_general(qi_ref[:, h * IDX_DIM:(h + 1) * IDX_DIM], kblk,
                                (((1,), (1,)), ((), ())), preferred_element_type=F32)
            sc = sc + jnp.maximum(s, 0.0) * wcol[:, h:h + 1]
        sc = jnp.where(col + k0 < limit, sc, -jnp.inf)
        for j in range(bpc):
            score_ref[c * bpc + j] = sc[:, j * LANES:(j + 1) * LANES]
        return carry

    lax.fori_loop(0, n_chunks, score_chunk, 0)

    def code_to_float(code):
        bits = jnp.where(code >= 0, code, code ^ 0x7FFFFFFF)
        return lax.bitcast_convert_type(bits, F32)

    def bit_body(bi, code):
        cand = code + lax.shift_left(jnp.int32(1), 31 - bi)
        cf = code_to_float(cand)

        def count_block(blk, cnt):
            return cnt + jnp.where(score_ref[blk] >= cf, 1.0, 0.0)

        cnt = lax.fori_loop(0, i + 1, count_block, jnp.zeros((qb, LANES), F32))
        total = jnp.sum(cnt, axis=-1, keepdims=True)
        return jnp.where(total >= k_sel, cand, code)

    code = lax.fori_loop(0, 32, bit_body, jnp.full((qb, LANES), INT_MIN, jnp.int32))
    thr = jnp.where(code <= CODE_NEG_INF, F32_LOWEST, code_to_float(code))

    for h in range(n_h):
        ql = jnp.dot(qa_ref[:, h * HEAD_DIM:(h + 1) * HEAD_DIM], wuk_ref[h],
                     preferred_element_type=F32)
        qlat_ref[h * qb:(h + 1) * qb, :] = ql.astype(BF16)

    m_ref[...] = jnp.full(m_ref.shape, NEG, F32)
    l_ref[...] = jnp.zeros(l_ref.shape, F32)
    acc_ref[...] = jnp.zeros(acc_ref.shape, F32)

    def flash_step(k0, width, sel, bias_of_head):
        cblk = ckv_ref[pl.ds(k0, width), :]
        s = lax.dot_general(qlat_ref[...], cblk, (((1,), (1,)), ((), ())),
                            preferred_element_type=F32)
        for g in range(n_h // hpg):
            rg = slice(g * hpg * qb, (g + 1) * hpg * qb)
            heads = [g * hpg + hh for hh in range(hpg)]
            sh, m_new = {}, {}
            for h in heads:
                rs = slice(h * qb, (h + 1) * qb)
                sh[h] = jnp.where(sel, s[rs, :] * (scale * LOG2E) + bias_of_head(h), NEG)
                m_new[h] = jnp.maximum(m_ref[rs, :], jnp.max(sh[h], axis=-1, keepdims=True))
            for h in heads:
                rs = slice(h * qb, (h + 1) * qb)
                a = jnp.exp2(m_ref[rs, :] - m_new[h])
                p = jnp.exp2(sh[h] - _widen(m_new[h], width))
                l_ref[rs, :] = a * l_ref[rs, :] + jnp.sum(p, axis=-1, keepdims=True)
                m_ref[rs, :] = m_new[h]
                alpha_ref[rs, :] = a
                p_ref[rs, 0:width] = p.astype(BF16)
            pv = jnp.dot(p_ref[rg, 0:width], cblk, preferred_element_type=F32)
            acc_ref[rg, :] = acc_ref[rg, :] * _widen(alpha_ref[rg, :], pv.shape[1]) + pv

    far_end = (i - 1) * qb

    def far_chunk(c, carry):
        k0 = pl.multiple_of(c * tk, tk)
        sc = jnp.concatenate([score_ref[c * bpc + j] for j in range(bpc)], axis=-1)
        sel = (sc >= _widen(thr, tk)) & (col + k0 < far_end)
        flash_step(k0, tk, sel, lambda h: bfar_ref[h])
        return carry

    n_far = jnp.maximum((i - 1 + bpc - 1) // bpc, 0)
    lax.fori_loop(0, n_far, far_chunk, 0)

    wb = jnp.maximum(i - 1, 0)
    tsel = jnp.where(i == 0, 1, 0)
    scw = jnp.concatenate([score_ref[wb], score_ref[wb + 1]], axis=-1)
    selw = scw >= _widen(thr, 2 * LANES)
    flash_step(pl.multiple_of(wb * LANES, LANES), 2 * LANES, selw, lambda h: tab_ref[tsel, h])

    for h in range(n_h):
        rs = slice(h * qb, (h + 1) * qb)
        o_lat = (acc_ref[rs, :] / _widen(l_ref[rs, :], acc_ref.shape[1])).astype(BF16)
        o_ref[:, h * HEAD_DIM:(h + 1) * HEAD_DIM] = jnp.dot(
            o_lat, wuv_ref[h], preferred_element_type=F32).astype(o_ref.dtype)


def _dsa(qkv, wi, ki, ckv, wuk, wuv, tab, bfar, *, batch, seq, n_ih, n_h, qi_blk, qa_blk):
    m = qkv.shape[0]
    c_lat = ckv.shape[1]
    nq = seq // Q_BLOCK
    k_sel = min(TOPK_MAX, seq // 4)
    tk = _tile(seq, 512)
    rows = n_h * Q_BLOCK
    kern = functools.partial(_dsa_kernel, n_ih=n_ih, n_h=n_h, k_sel=k_sel, tk=tk,
                             hpg=math.gcd(n_h, DSA_HEADS_PER_GROUP))
    const = dict(pipeline_mode=pl.Buffered(1))
    return pl.pallas_call(
        kern,
        grid=(batch, nq),
        in_specs=[
            pl.BlockSpec((Q_BLOCK, n_ih * IDX_DIM), lambda b, i: (b * nq + i, qi_blk)),
            pl.BlockSpec((Q_BLOCK, n_h * HEAD_DIM), lambda b, i: (b * nq + i, qa_blk)),
            pl.BlockSpec((Q_BLOCK, n_ih), lambda b, i: (b * nq + i, 0)),
            pl.BlockSpec((seq, IDX_DIM), lambda b, i: (b, 0)),
            pl.BlockSpec((seq, c_lat), lambda b, i: (b, 0)),
            pl.BlockSpec((n_h, HEAD_DIM, c_lat), lambda b, i: (0, 0, 0), **const),
            pl.BlockSpec((n_h, c_lat, HEAD_DIM), lambda b, i: (0, 0, 0), **const),
            pl.BlockSpec((2, n_h, Q_BLOCK, 2 * LANES), lambda b, i: (0, 0, 0, 0), **const),
            pl.BlockSpec(memory_space=pltpu.SMEM),
        ],
        out_specs=pl.BlockSpec((Q_BLOCK, n_h * HEAD_DIM), lambda b, i: (b * nq + i, 0)),
        out_shape=jax.ShapeDtypeStruct((m, n_h * HEAD_DIM), BF16),
        scratch_shapes=[
            pltpu.VMEM((seq // LANES, Q_BLOCK, LANES), F32),
            pltpu.VMEM((rows, c_lat), BF16),
            pltpu.VMEM((rows, c_lat), F32),
            pltpu.VMEM((rows, LANES), F32),
            pltpu.VMEM((rows, LANES), F32),
            pltpu.VMEM((rows, tk), BF16),
            pltpu.VMEM((rows, LANES), F32),
        ],
        compiler_params=_params(("parallel", "arbitrary")),
        name="dsa_mixer",
    )(qkv, qkv, wi, ki, ckv, wuk, wuv, tab, bfar)


def _fox_kernel(q_ref, k_ref, v_ref, cq_ref, ck_ref, o_ref, acc_ref, m_ref, l_ref, *, tq, hpb):
    hp = pl.program_id(1)
    i = pl.program_id(2)
    c1 = (HEAD_DIM ** -0.5) * LOG2E
    lane = lax.broadcasted_iota(jnp.int32, cq_ref.shape, 1)
    cq_all = cq_ref[...] * LOG2E
    cq = [jnp.sum(jnp.where(lane == hp * hpb + hh, cq_all, 0.0), axis=-1, keepdims=True)
          for hh in range(hpb)]

    m_ref[...] = jnp.full(m_ref.shape, NEG, F32)
    l_ref[...] = jnp.zeros(l_ref.shape, F32)
    acc_ref[...] = jnp.zeros(acc_ref.shape, F32)

    def step(kb, causal):
        k0 = pl.multiple_of(kb * tq, tq)
        logits = []
        for hh in range(hpb):
            ls = slice(hh * HEAD_DIM, (hh + 1) * HEAD_DIM)
            logits.append(lax.dot_general(q_ref[:, ls], k_ref[pl.ds(k0, tq), ls],
                                          (((1,), (1,)), ((), ())), preferred_element_type=F32))
        sh, m_new = [], []
        for hh in range(hpb):
            ck = ck_ref[pl.ds(hp * hpb + hh, 1), pl.ds(kb, 1), :].reshape(1, tq) * LOG2E
            s = logits[hh] * c1 + (cq[hh] - ck)
            if causal:
                row = lax.broadcasted_iota(jnp.int32, (tq, tq), 0)
                col = lax.broadcasted_iota(jnp.int32, (tq, tq), 1)
                s = jnp.where(col <= row, s, NEG)
            sh.append(s)
            m_new.append(jnp.maximum(m_ref[hh], jnp.max(s, axis=-1, keepdims=True)))
        for hh in range(hpb):
            ls = slice(hh * HEAD_DIM, (hh + 1) * HEAD_DIM)
            a = jnp.exp2(m_ref[hh] - m_new[hh])
            p = jnp.exp2(sh[hh] - _widen(m_new[hh], tq))
            l_ref[hh] = a * l_ref[hh] + jnp.sum(p, axis=-1, keepdims=True)
            m_ref[hh] = m_new[hh]
            acc_ref[hh] = acc_ref[hh] * a + jnp.dot(p.astype(BF16), v_ref[pl.ds(k0, tq), ls],
                                                    preferred_element_type=F32)

    def full_block(kb, carry):
        step(kb, False)
        return carry

    lax.fori_loop(0, i, full_block, 0)
    step(i, True)
    for hh in range(hpb):
        o_ref[:, hh * HEAD_DIM:(hh + 1) * HEAD_DIM] = (acc_ref[hh] / l_ref[hh]).astype(o_ref.dtype)


def _fox(qkv, cum_tok, cum_blk, *, batch, seq, n_h, q_blk0, k_blk0, v_blk0):
    m = qkv.shape[0]
    tq = _tile(seq, 512)
    nq = seq // tq
    hpb = math.gcd(n_h, FOX_HEADS_PER_BLOCK)
    assert q_blk0 % hpb == 0 and k_blk0 % hpb == 0 and v_blk0 % hpb == 0
    wb = hpb * HEAD_DIM
    kern = functools.partial(_fox_kernel, tq=tq, hpb=hpb)
    return pl.pallas_call(
        kern,
        grid=(batch, n_h // hpb, nq),
        in_specs=[
            pl.BlockSpec((tq, wb), lambda b, h, i: (b * nq + i, q_blk0 // hpb + h)),
            pl.BlockSpec((seq, wb), lambda b, h, i: (b, k_blk0 // hpb + h)),
            pl.BlockSpec((seq, wb), lambda b, h, i: (b, v_blk0 // hpb + h)),
            pl.BlockSpec((tq, n_h), lambda b, h, i: (b * nq + i, 0)),
            pl.BlockSpec((n_h, nq, tq), lambda b, h, i: (0, b, 0)),
        ],
        out_specs=pl.BlockSpec((tq, wb), lambda b, h, i: (b * nq + i, h)),
        out_shape=jax.ShapeDtypeStruct((m, n_h * HEAD_DIM), BF16),
        scratch_shapes=[
            pltpu.VMEM((hpb, tq, HEAD_DIM), F32),
            pltpu.VMEM((hpb, tq, LANES), F32),
            pltpu.VMEM((hpb, tq, LANES), F32),
        ],
        compiler_params=_params(("parallel", "parallel", "arbitrary")),
        name="fox_mixer",
    )(qkv, qkv, qkv, cum_tok, cum_blk)


def _outproj_kernel(oa_ref, ob_ref, w_ref, res_ref, g_ref, b_ref, h_ref, hb_ref, *,
                    alpha, n_ka, n_k, rows, col_chunk):
    k = pl.program_id(1)
    tm, d = h_ref.shape

    @pl.when(k == 0)
    def _():
        def body(r, carry):
            r0 = pl.multiple_of(r * rows, rows)
            h_ref[pl.ds(r0, rows), :] = alpha * res_ref[pl.ds(r0, rows), :]
            return carry
        lax.fori_loop(0, tm // rows, body, 0)

    def accumulate(src_ref):
        src = src_ref[...]
        for c in range(d // col_chunk):
            sl = slice(c * col_chunk, (c + 1) * col_chunk)
            h_ref[:, sl] += jnp.dot(src, w_ref[:, sl], preferred_element_type=F32)

    @pl.when(k < n_ka)
    def _():
        accumulate(oa_ref)

    @pl.when(k >= n_ka)
    def _():
        accumulate(ob_ref)

    @pl.when(k == n_k - 1)
    def _():
        _ln_epilogue(h_ref, hb_ref, g_ref, b_ref, rows)


def _outproj_ln(oa, ob, w, res, g, b, alpha):
    m, wa = oa.shape
    wbw = ob.shape[1]
    d = w.shape[1]
    tm = _tile(m, 512)
    tk = _tile(math.gcd(wa, wbw), 512)
    n_ka, n_kb = wa // tk, wbw // tk
    n_k = n_ka + n_kb
    kern = functools.partial(_outproj_kernel, alpha=alpha, n_ka=n_ka, n_k=n_k,
                             rows=_tile(tm, 32), col_chunk=_tile(d, 512))
    return pl.pallas_call(
        kern,
        grid=(m // tm, n_k),
        in_specs=[
            pl.BlockSpec((tm, tk), lambda i, k: (i, jnp.minimum(k, n_ka - 1))),
            pl.BlockSpec((tm, tk), lambda i, k: (i, jnp.maximum(k - n_ka, 0))),
            pl.BlockSpec((tk, d), lambda i, k: (k, 0)),
            pl.BlockSpec((tm, d), lambda i, k: (i, 0), pipeline_mode=pl.Buffered(1)),
            pl.BlockSpec((1, d), lambda i, k: (0, 0)),
            pl.BlockSpec((1, d), lambda i, k: (0, 0)),
        ],
        out_specs=[
            pl.BlockSpec((tm, d), lambda i, k: (i, 0)),
            pl.BlockSpec((tm, d), lambda i, k: (i, 0)),
        ],
        out_shape=[jax.ShapeDtypeStruct((m, d), F32), jax.ShapeDtypeStruct((m, d), BF16)],
        compiler_params=_params(("parallel", "arbitrary")),
        name="outproj_ln",
    )(oa, ob, w, res, g, b)


def _t5_bucket(rel):
    nb = N_BUCKETS // 2
    max_exact = nb // 2
    ret = jnp.where(rel > 0, nb, 0)
    n = jnp.abs(rel)
    nf = jnp.maximum(n, 1).astype(jnp.float32)
    large = max_exact + (jnp.log(nf / max_exact) / math.log(MAX_DISTANCE / max_exact)
                         * (nb - max_exact)).astype(jnp.int32)
    large = jnp.minimum(large, nb - 1)
    return ret + jnp.where(n < max_exact, n, large)


def _bias_tables(rel_bias):
    r = jnp.arange(Q_BLOCK, dtype=jnp.int32)[:, None]
    j = jnp.arange(2 * LANES, dtype=jnp.int32)[None, :]
    rel = jnp.stack([j - LANES - r, j - r])
    tab = jnp.moveaxis(rel_bias[_t5_bucket(rel)], -1, 1).astype(F32)
    far = rel_bias[_t5_bucket(jnp.full((), -(MAX_DISTANCE + 1), jnp.int32))].astype(F32)
    return tab, far


def _hybrid_mixer_ln(h, hb, w_in, b_f, kv_norm_g, idx_k_g, idx_k_b, w_uk, w_uv, rel_bias, w_out,
                     ln_g, ln_b, alpha, batch, seq):
    m, d = h.shape
    n_h = w_uk.shape[0]
    c_lat = w_uk.shape[2]
    n_hf = b_f.shape[0]
    w_dsa = n_h * HEAD_DIM
    w_fox = n_hf * HEAD_DIM
    d_in = w_in.shape[1]
    n_ih = (d_in - w_dsa - c_lat - IDX_DIM - 3 * w_fox - n_hf) // (IDX_DIM + 1)
    w_qi = n_ih * IDX_DIM
    assert w_dsa + c_lat + w_qi + IDX_DIM + n_ih + 3 * w_fox + n_hf == d_in
    assert w_qi % w_dsa == 0 and seq % (2 * LANES) == 0

    o0 = 0
    w_qa = w_in[:, o0:o0 + w_dsa]; o0 += w_dsa
    w_ckv = w_in[:, o0:o0 + c_lat]; o0 += c_lat
    w_qidx = w_in[:, o0:o0 + w_qi]; o0 += w_qi
    w_kidx = w_in[:, o0:o0 + IDX_DIM]; o0 += IDX_DIM
    w_widx = w_in[:, o0:o0 + n_ih]; o0 += n_ih
    w_qkv_fox = w_in[:, o0:o0 + 3 * w_fox]; o0 += 3 * w_fox
    w_f = w_in[:, o0:o0 + n_hf]

    w_main = jnp.concatenate([w_qidx, w_qa, w_qkv_fox], axis=1).astype(BF16)
    qkv = _proj(hb, w_main)

    n_small = c_lat + IDX_DIM + n_ih
    pad = (-n_small) % LANES
    w_small = jnp.pad(jnp.concatenate([w_ckv, w_kidx, w_widx], axis=1), ((0, 0), (0, pad))).astype(BF16)
    ckv, ki, wi, cum_t = _proj_small(
        hb, w_small, w_f.T.astype(BF16), b_f.reshape(n_hf, 1).astype(F32),
        kv_norm_g.reshape(1, c_lat), idx_k_g.reshape(1, IDX_DIM), idx_k_b.reshape(1, IDX_DIM),
        c_lat=c_lat, n_ih=n_ih, seq=seq)

    tab, far = _bias_tables(rel_bias)
    tab, far = tab * LOG2E, far * LOG2E
    o_a = _dsa(qkv, wi, ki, ckv, w_uk.astype(BF16), w_uv.astype(BF16), tab, far,
               batch=batch, seq=seq, n_ih=n_ih, n_h=n_h, qi_blk=0, qa_blk=w_qi // w_dsa)

    tq = _tile(seq, 512)
    blk0 = (w_qi + w_dsa) // HEAD_DIM
    o_b = _fox(qkv, cum_t.T, cum_t.reshape(n_hf, m // tq, tq), batch=batch, seq=seq, n_h=n_hf,
               q_blk0=blk0, k_blk0=blk0 + n_hf, v_blk0=blk0 + 2 * n_hf)

    return _outproj_ln(o_a, o_b, w_out.astype(BF16), h, ln_g.reshape(1, d), ln_b.reshape(1, d), alpha)


def kernel(x, ffn1_w_gate, ffn1_w_up, ffn1_w_down, ln1_g, ln1_b, w_in, b_f, kv_norm_g, idx_k_g,
           idx_k_b, w_uk, w_uv, rel_bias, w_out, ln2_g, ln2_b, ffn2_w_gate, ffn2_w_up,
           ffn2_w_down, ln3_g, ln3_b):
    batch, seq, d = x.shape
    depth = ffn1_w_gate.shape[0]
    alpha = (2.0 * depth) ** 0.25
    h = x.reshape(batch * seq, d)
    for l in range(depth):
        h, hb = _ffn_ln(h, ffn1_w_gate[l].astype(BF16), ffn1_w_up[l].astype(BF16),
                        ffn1_w_down[l].astype(BF16), ln1_g[l].reshape(1, d), ln1_b[l].reshape(1, d),
                        alpha)
        h, hb = _hybrid_mixer_ln(h, hb, w_in[l], b_f[l], kv_norm_g[l], idx_k_g[l], idx_k_b[l],
                                 w_uk[l], w_uv[l], rel_bias, w_out[l], ln2_g[l], ln2_b[l], alpha,
                                 batch, seq)
        h, hb = _ffn_ln(h, ffn2_w_gate[l].astype(BF16), ffn2_w_up[l].astype(BF16),
                        ffn2_w_down[l].astype(BF16), ln3_g[l].reshape(1, d), ln3_b[l].reshape(1, d),
                        alpha)
    return h.reshape(batch, seq, d)
```

```python
import functools
import math

import jax
import jax.numpy as jnp
import numpy as np
from jax import lax
from jax.experimental import pallas as pl
from jax.experimental.pallas import tpu as pltpu

CHUNK = 64
Q_BLOCK = 128
HEAD_DIM = 128
IDX_DIM = 128
TOPK_MAX = 256
N_BUCKETS = 32
MAX_DISTANCE = 128
LN_EPS = 1e-5
RMS_EPS = 1e-6
NEG = -1e30
LOG2E = math.log2(math.e)
DSA_HEADS_PER_GROUP = 4
FOX_HEADS_PER_BLOCK = 4
LN_ROWS_PER_STEP = 64
LN_ROW_GROUPS = 4

LANES = 128
VMEM_LIMIT_BYTES = 56 * 1024 * 1024

BF16 = jnp.bfloat16
F32 = jnp.float32

INT_MIN = -(2 ** 31)
CODE_NEG_INF = -(2 ** 31) + 0x7FFFFF
F32_LOWEST = float(np.finfo(np.float32).min)


def _tile(dim, pref):
    t = min(dim, pref)
    while dim % t:
        t //= 2
    return t


def _params(sem):
    return pltpu.CompilerParams(dimension_semantics=sem, vmem_limit_bytes=VMEM_LIMIT_BYTES)


def _widen(x, width):
    reps = width // x.shape[1]
    return x if reps == 1 else jnp.concatenate([x] * reps, axis=-1)


def _layer_norm_rows(y, g, b):
    mu = jnp.mean(y, axis=-1, keepdims=True)
    d = y - mu
    var = jnp.mean(d * d, axis=-1, keepdims=True)
    return d * lax.rsqrt(var + LN_EPS) * g + b


def _ln_epilogue(h_ref, hb_ref, g_ref, b_ref, rows):
    g = g_ref[...]
    b = b_ref[...]
    sub = rows // LN_ROW_GROUPS

    def body(r, carry):
        r0 = pl.multiple_of(r * rows, rows)
        sls = [pl.ds(r0 + k * sub, sub) for k in range(LN_ROW_GROUPS)]
        mus = [jnp.mean(h_ref[sl, :], axis=-1, keepdims=True) for sl in sls]
        rstds = []
        for sl, mu in zip(sls, mus):
            d = h_ref[sl, :] - mu
            rstds.append(lax.rsqrt(jnp.mean(d * d, axis=-1, keepdims=True) + LN_EPS))
        for sl, mu, rstd in zip(sls, mus, rstds):
            out = (h_ref[sl, :] - mu) * rstd * g + b
            h_ref[sl, :] = out
            hb_ref[sl, :] = out.astype(BF16)
        return carry

    lax.fori_loop(0, h_ref.shape[0] // rows, body, 0)


def _ffn_kernel(x_hbm, wg_ref, wu_ref, wd_ref, g_ref, b_ref, h_ref, hb_ref, xb_ref, *,
                alpha, n_f, rows, col_chunk):
    i = pl.program_id(0)
    f = pl.program_id(1)
    tm, d = h_ref.shape

    @pl.when(f == 0)
    def _():
        pltpu.sync_copy(x_hbm.at[pl.ds(pl.multiple_of(i * tm, tm), tm), :], h_ref)

        def body(r, carry):
            r0 = pl.multiple_of(r * rows, rows)
            xr = h_ref[pl.ds(r0, rows), :]
            xb_ref[pl.ds(r0, rows), :] = xr.astype(BF16)
            h_ref[pl.ds(r0, rows), :] = alpha * xr
            return carry
        lax.fori_loop(0, tm // rows, body, 0)

    xb = xb_ref[...]
    gate = jnp.dot(xb, wg_ref[...], preferred_element_type=F32)
    up = jnp.dot(xb, wu_ref[...], preferred_element_type=F32)
    act = (0.5 * (gate * jax.nn.sigmoid(gate)) * up).astype(BF16)
    for c in range(d // col_chunk):
        sl = slice(c * col_chunk, (c + 1) * col_chunk)
        h_ref[:, sl] += jnp.dot(act, wd_ref[:, sl], preferred_element_type=F32)

    @pl.when(f == n_f - 1)
    def _():
        _ln_epilogue(h_ref, hb_ref, g_ref, b_ref, rows)


def _ffn_ln(x, wg, wu, wd, g, b, alpha):
    m, d = x.shape
    f = wg.shape[1]
    tm = _tile(m, 1024)
    tf = _tile(f, 256)
    n_f = f // tf
    kern = functools.partial(_ffn_kernel, alpha=alpha, n_f=n_f, rows=_tile(tm, LN_ROWS_PER_STEP),
                             col_chunk=_tile(d, 512))
    single = dict(pipeline_mode=pl.Buffered(1))
    return pl.pallas_call(
        kern,
        grid=(m // tm, n_f),
        in_specs=[
            pl.BlockSpec(memory_space=pl.ANY),
            pl.BlockSpec((d, tf), lambda i, j: (0, j)),
            pl.BlockSpec((d, tf), lambda i, j: (0, j)),
            pl.BlockSpec((tf, d), lambda i, j: (j, 0)),
            pl.BlockSpec((1, d), lambda i, j: (0, 0)),
            pl.BlockSpec((1, d), lambda i, j: (0, 0)),
        ],
        out_specs=[
            pl.BlockSpec((tm, d), lambda i, j: (i, 0), **single),
            pl.BlockSpec((tm, d), lambda i, j: (i, 0), **single),
        ],
        out_shape=[jax.ShapeDtypeStruct((m, d), F32), jax.ShapeDtypeStruct((m, d), BF16)],
        scratch_shapes=[pltpu.VMEM((tm, d), BF16)],
        compiler_params=_params(("parallel", "arbitrary")),
        name="ffn_ln",
    )(x, wg, wu, wd, g, b)


def _proj_kernel(x_ref, w_ref, o_ref):
    o_ref[...] = jnp.dot(x_ref[...], w_ref[...], preferred_element_type=F32).astype(o_ref.dtype)


def _proj(xb, w):
    m, d = xb.shape
    n = w.shape[1]
    tm = _tile(m, 1024)
    tn = _tile(n, 512)
    return pl.pallas_call(
        _proj_kernel,
        grid=(m // tm, n // tn),
        in_specs=[
            pl.BlockSpec((tm, d), lambda i, j: (i, 0)),
            pl.BlockSpec((d, tn), lambda i, j: (0, j)),
        ],
        out_specs=pl.BlockSpec((tm, tn), lambda i, j: (i, j)),
        out_shape=jax.ShapeDtypeStruct((m, n), BF16),
        compiler_params=_params(("parallel", "arbitrary")),
        name="proj_main",
    )(xb, w)


def _split3(v):
    hi = v.astype(BF16)
    r1 = v - hi.astype(F32)
    mid = r1.astype(BF16)
    lo = (r1 - mid.astype(F32)).astype(BF16)
    return hi, mid, lo


def _proj_small_kernel(x_ref, ws_ref, wft_ref, bf_ref, kvg_ref, kg_ref, kb_ref,
                       ckv_ref, ki_ref, wi_ref, cum_ref, carry_ref, *, c_lat, n_ih, tiles_per_seq):
    i = pl.program_id(0)
    x = x_ref[...]
    tm = x.shape[0]
    y = jnp.dot(x, ws_ref[...], preferred_element_type=F32)
    c = y[:, :c_lat]
    c = c * lax.rsqrt(jnp.mean(c * c, axis=-1, keepdims=True) + RMS_EPS) * kvg_ref[...]
    ckv_ref[...] = c.astype(BF16)
    k = y[:, c_lat:c_lat + IDX_DIM]
    ki_ref[...] = _layer_norm_rows(k, kg_ref[...], kb_ref[...]).astype(BF16)
    wi_ref[...] = y[:, c_lat + IDX_DIM:c_lat + IDX_DIM + n_ih]

    ft = lax.dot_general(wft_ref[...], x, (((1,), (1,)), ((), ())), preferred_element_type=F32)
    z = ft + bf_ref[...]
    log_f = jnp.minimum(z, 0.0) - jnp.log1p(jnp.exp(-jnp.abs(z)))

    @pl.when(i % tiles_per_seq == 0)
    def _():
        carry_ref[...] = jnp.zeros_like(carry_ref)

    row = lax.broadcasted_iota(jnp.int32, (tm, tm), 0)
    col = lax.broadcasted_iota(jnp.int32, (tm, tm), 1)
    tri = jnp.where(row <= col, 1.0, 0.0).astype(BF16)
    hi, mid, lo = _split3(log_f)
    cs = (jnp.dot(hi, tri, preferred_element_type=F32)
          + jnp.dot(mid, tri, preferred_element_type=F32)
          + jnp.dot(lo, tri, preferred_element_type=F32))
    cum = cs + carry_ref[:, 0:1]
    cum_ref[...] = cum
    carry_ref[...] = jnp.broadcast_to(cum[:, tm - 1:tm], carry_ref.shape)


def _proj_small(xb, ws, wft, bf, kvg, kg, kb, *, c_lat, n_ih, seq):
    m, d = xb.shape
    ns = ws.shape[1]
    hf = wft.shape[0]
    tm = _tile(seq, 512)
    kern = functools.partial(_proj_small_kernel, c_lat=c_lat, n_ih=n_ih, tiles_per_seq=seq // tm)
    return pl.pallas_call(
        kern,
        grid=(m // tm,),
        in_specs=[
            pl.BlockSpec((tm, d), lambda i: (i, 0)),
            pl.BlockSpec((d, ns), lambda i: (0, 0)),
            pl.BlockSpec((hf, d), lambda i: (0, 0)),
            pl.BlockSpec((hf, 1), lambda i: (0, 0)),
            pl.BlockSpec((1, c_lat), lambda i: (0, 0)),
            pl.BlockSpec((1, IDX_DIM), lambda i: (0, 0)),
            pl.BlockSpec((1, IDX_DIM), lambda i: (0, 0)),
        ],
        out_specs=[
            pl.BlockSpec((tm, c_lat), lambda i: (i, 0)),
            pl.BlockSpec((tm, IDX_DIM), lambda i: (i, 0)),
            pl.BlockSpec((tm, n_ih), lambda i: (i, 0)),
            pl.BlockSpec((hf, tm), lambda i: (0, i)),
        ],
        out_shape=[
            jax.ShapeDtypeStruct((m, c_lat), BF16),
            jax.ShapeDtypeStruct((m, IDX_DIM), BF16),
            jax.ShapeDtypeStruct((m, n_ih), F32),
            jax.ShapeDtypeStruct((hf, m), F32),
        ],
        scratch_shapes=[pltpu.VMEM((hf, LANES), F32)],
        compiler_params=_params(("arbitrary",)),
        name="proj_small",
    )(xb, ws, wft, bf, kvg, kg, kb)


def _dsa_kernel(qi_ref, qa_ref, wi_ref, ki_ref, ckv_ref, wuk_ref, wuv_ref, tab_ref, bfar_ref,
                o_ref, score_ref, score_t_ref, qlat_ref, acc_ref, m_ref, l_ref, p_ref, alpha_ref, *,
                n_ih, n_h, k_sel, tk, hpg):
    i = pl.program_id(1)
    start = i * Q_BLOCK
    qb = Q_BLOCK
    bpc = tk // LANES
    scale = HEAD_DIM ** -0.5
    w_fold = (IDX_DIM ** -0.5) * (n_ih ** -0.5)

    wcol = wi_ref[...] * w_fold
    row = lax.broadcasted_iota(jnp.int32, (qb, tk), 0)
    col = lax.broadcasted_iota(jnp.int32, (qb, tk), 1)
    limit = start + jnp.where(row < CHUNK, CHUNK, 2 * CHUNK)
    n_chunks = (i + bpc) // bpc

    def score_chunk(c, carry):
        k0 = pl.multiple_of(c * tk, tk)
        kblk = ki_ref[pl.ds(k0, tk), :]
        sc = jnp.zeros((qb, tk), F32)
        for h in range(n_ih):
            s = lax.dot_general(qi_ref[:, h * IDX_DIM:(h + 1) * IDX_DIM], kblk,
                                (((1,), (1,)), ((), ())), preferred_element_type=F32)
            sc = sc + jnp.maximum(s, 0.0) * wcol[:, h:h + 1]
        sc = jnp.where(col + k0 < limit, sc, -jnp.inf)
        for j in range(bpc):
            blk = sc[:, j * LANES:(j + 1) * LANES]
            score_ref[c * bpc + j] = blk
            score_t_ref[c * bpc + j] = blk.T
        return carry

    lax.fori_loop(0, n_chunks, score_chunk, 0)

    def code_to_float(code):
        bits = jnp.where(code >= 0, code, code ^ 0x7FFFFFFF)
        return lax.bitcast_convert_type(bits, F32)

    def bit_body(bi, code):
        cand = code + lax.shift_left(jnp.int32(1), 31 - bi)
        cf = jnp.concatenate([code_to_float(cand)] * (LANES // 8), axis=0)

        def count_chunk(c, cnt):
            for j in range(bpc):
                hit = jnp.where(score_t_ref[c * bpc + j] >= cf, 1.0, 0.0)
                cnt = cnt + jnp.sum(hit.reshape(LANES // 8, 8, qb), axis=0)
            return cnt

        cnt = lax.fori_loop(0, n_chunks, count_chunk, jnp.zeros((8, qb), F32))
        total = jnp.sum(cnt, axis=0, keepdims=True)
        return jnp.where(total >= k_sel, cand, code)

    code = lax.fori_loop(0, 32, bit_body, jnp.full((8, qb), INT_MIN, jnp.int32))
    thr_row = jnp.where(code <= CODE_NEG_INF, F32_LOWEST, code_to_float(code))
    thr = jnp.concatenate([thr_row] * (qb // 8), axis=0).T

    for h in range(n_h):
        ql = jnp.dot(qa_ref[:, h * HEAD_DIM:(h + 1) * HEAD_DIM], wuk_ref[h],
                     preferred_element_type=F32)
        qlat_ref[h * qb:(h + 1) * qb, :] = ql.astype(BF16)

    m_ref[...] = jnp.full(m_ref.shape, NEG, F32)
    l_ref[...] = jnp.zeros(l_ref.shape, F32)
    acc_ref[...] = jnp.zeros(acc_ref.shape, F32)

    def flash_step(k0, width, sel, bias_of_head):
        cblk = ckv_ref[pl.ds(k0, width), :]
        s = lax.dot_general(qlat_ref[...], cblk, (((1,), (1,)), ((), ())),
                            preferred_element_type=F32)
        for g in range(n_h // hpg):
            rg = slice(g * hpg * qb, (g + 1) * hpg * qb)
            heads = [g * hpg + hh for hh in range(hpg)]
            sh, m_new = {}, {}
            for h in heads:
                rs = slice(h * qb, (h + 1) * qb)
                sh[h] = jnp.where(sel, s[rs, :] * (scale * LOG2E) + bias_of_head(h), NEG)
                m_new[h] = jnp.maximum(m_ref[rs, :], jnp.max(sh[h], axis=-1, keepdims=True))
            for h in heads:
                rs = slice(h * qb, (h + 1) * qb)
                a = jnp.exp2(m_ref[rs, :] - m_new[h])
                p = jnp.exp2(sh[h] - _widen(m_new[h], width))
                l_ref[rs, :] = a * l_ref[rs, :] + jnp.sum(p, axis=-1, keepdims=True)
                m_ref[rs, :] = m_new[h]
                alpha_ref[rs, :] = a
                p_ref[rs, 0:width] = p.astype(BF16)
            pv = jnp.dot(p_ref[rg, 0:width], cblk, preferred_element_type=F32)
            acc_ref[rg, :] = acc_ref[rg, :] * _widen(alpha_ref[rg, :], pv.shape[1]) + pv

    far_end = (i - 1) * qb

    def far_chunk(c, carry):
        k0 = pl.multiple_of(c * tk, tk)
        sc = jnp.concatenate([score_ref[c * bpc + j] for j in range(bpc)], axis=-1)
        sel = (sc >= _widen(thr, tk)) & (col + k0 < far_end)
        flash_step(k0, tk, sel, lambda h: bfar_ref[h])
        return carry

    n_far = jnp.maximum((i - 1 + bpc - 1) // bpc, 0)
    lax.fori_loop(0, n_far, far_chunk, 0)

    wb = jnp.maximum(i - 1, 0)
    tsel = jnp.where(i == 0, 1, 0)
    scw = jnp.concatenate([score_ref[wb], score_ref[wb + 1]], axis=-1)
    selw = scw >= _widen(thr, 2 * LANES)
    flash_step(pl.multiple_of(wb * LANES, LANES), 2 * LANES, selw, lambda h: tab_ref[tsel, h])

    for h in range(n_h):
        rs = slice(h * qb, (h + 1) * qb)
        o_lat = (acc_ref[rs, :] / _widen(l_ref[rs, :], acc_ref.shape[1])).astype(BF16)
        o_ref[:, h * HEAD_DIM:(h + 1) * HEAD_DIM] = jnp.dot(
            o_lat, wuv_ref[h], preferred_element_type=F32).astype(o_ref.dtype)


def _dsa(qkv, wi, ki, ckv, wuk, wuv, tab, bfar, *, batch, seq, n_ih, n_h, qi_blk, qa_blk):
    m = qkv.shape[0]
    c_lat = ckv.shape[1]
    nq = seq // Q_BLOCK
    k_sel = min(TOPK_MAX, seq // 4)
    tk = _tile(seq, 512)
    rows = n_h * Q_BLOCK
    kern = functools.partial(_dsa_kernel, n_ih=n_ih, n_h=n_h, k_sel=k_sel, tk=tk,
                             hpg=math.gcd(n_h, DSA_HEADS_PER_GROUP))
    const = dict(pipeline_mode=pl.Buffered(1))
    return pl.pallas_call(
        kern,
        grid=(batch, nq),
        in_specs=[
            pl.BlockSpec((Q_BLOCK, n_ih * IDX_DIM), lambda b, i: (b * nq + i, qi_blk)),
            pl.BlockSpec((Q_BLOCK, n_h * HEAD_DIM), lambda b, i: (b * nq + i, qa_blk)),
            pl.BlockSpec((Q_BLOCK, n_ih), lambda b, i: (b * nq + i, 0)),
            pl.BlockSpec((seq, IDX_DIM), lambda b, i: (b, 0)),
            pl.BlockSpec((seq, c_lat), lambda b, i: (b, 0)),
            pl.BlockSpec((n_h, HEAD_DIM, c_lat), lambda b, i: (0, 0, 0), **const),
            pl.BlockSpec((n_h, c_lat, HEAD_DIM), lambda b, i: (0, 0, 0), **const),
            pl.BlockSpec((2, n_h, Q_BLOCK, 2 * LANES), lambda b, i: (0, 0, 0, 0), **const),
            pl.BlockSpec(memory_space=pltpu.SMEM),
        ],
        out_specs=pl.BlockSpec((Q_BLOCK, n_h * HEAD_DIM), lambda b, i: (b * nq + i, 0)),
        out_shape=jax.ShapeDtypeStruct((m, n_h * HEAD_DIM), BF16),
        scratch_shapes=[
            pltpu.VMEM((seq // LANES, Q_BLOCK, LANES), F32),
            pltpu.VMEM((seq // LANES, LANES, Q_BLOCK), F32),
            pltpu.VMEM((rows, c_lat), BF16),
            pltpu.VMEM((rows, c_lat), F32),
            pltpu.VMEM((rows, LANES), F32),
            pltpu.VMEM((rows, LANES), F32),
            pltpu.VMEM((rows, tk), BF16),
            pltpu.VMEM((rows, LANES), F32),
        ],
        compiler_params=_params(("parallel", "arbitrary")),
        name="dsa_mixer",
    )(qkv, qkv, wi, ki, ckv, wuk, wuv, tab, bfar)


def _fox_kernel(q_ref, k_ref, v_ref, cq_ref, ck_ref, o_ref, acc_ref, m_ref, l_ref, *, tq, hpb):
    hp = pl.program_id(1)
    i = pl.program_id(2)
    c1 = (HEAD_DIM ** -0.5) * LOG2E
    lane = lax.broadcasted_iota(jnp.int32, cq_ref.shape, 1)
    cq_all = cq_ref[...] * LOG2E
    cq = [jnp.sum(jnp.where(lane == hp * hpb + hh, cq_all, 0.0), axis=-1, keepdims=True)
          for hh in range(hpb)]

    m_ref[...] = jnp.full(m_ref.shape, NEG, F32)
    l_ref[...] = jnp.zeros(l_ref.shape, F32)
    acc_ref[...] = jnp.zeros(acc_ref.shape, F32)

    def step(kb, causal):
        k0 = pl.multiple_of(kb * tq, tq)
        logits = []
        for hh in range(hpb):
            ls = slice(hh * HEAD_DIM, (hh + 1) * HEAD_DIM)
            logits.append(lax.dot_general(q_ref[:, ls], k_ref[pl.ds(k0, tq), ls],
                                          (((1,), (1,)), ((), ())), preferred_element_type=F32))
        sh, m_new = [], []
        for hh in range(hpb):
            ck = ck_ref[pl.ds(hp * hpb + hh, 1), pl.ds(kb, 1), :].reshape(1, tq) * LOG2E
            s = logits[hh] * c1 + (cq[hh] - ck)
            if causal:
                row = lax.broadcasted_iota(jnp.int32, (tq, tq), 0)
                col = lax.broadcasted_iota(jnp.int32, (tq, tq), 1)
                s = jnp.where(col <= row, s, NEG)
            sh.append(s)
            m_new.append(jnp.maximum(m_ref[hh], jnp.max(s, axis=-1, keepdims=True)))
        for hh in range(hpb):
            ls = slice(hh * HEAD_DIM, (hh + 1) * HEAD_DIM)
            a = jnp.exp2(m_ref[hh] - m_new[hh])
            p = jnp.exp2(sh[hh] - _widen(m_new[hh], tq))
            l_ref[hh] = a * l_ref[hh] + jnp.sum(p, axis=-1, keepdims=True)
            m_ref[hh] = m_new[hh]
            acc_ref[hh] = acc_ref[hh] * a + jnp.dot(p.astype(BF16), v_ref[pl.ds(k0, tq), ls],
                                                    preferred_element_type=F32)

    def full_block(kb, carry):
        step(kb, False)
        return carry

    lax.fori_loop(0, i, full_block, 0)
    step(i, True)
    for hh in range(hpb):
        o_ref[:, hh * HEAD_DIM:(hh + 1) * HEAD_DIM] = (acc_ref[hh] / l_ref[hh]).astype(o_ref.dtype)


def _fox(qkv, cum_tok, cum_blk, *, batch, seq, n_h, q_blk0, k_blk0, v_blk0):
    m = qkv.shape[0]
    tq = _tile(seq, 512)
    nq = seq // tq
    hpb = math.gcd(n_h, FOX_HEADS_PER_BLOCK)
    assert q_blk0 % hpb == 0 and k_blk0 % hpb == 0 and v_blk0 % hpb == 0
    wb = hpb * HEAD_DIM
    kern = functools.partial(_fox_kernel, tq=tq, hpb=hpb)
    return pl.pallas_call(
        kern,
        grid=(batch, n_h // hpb, nq),
        in_specs=[
            pl.BlockSpec((tq, wb), lambda b, h, i: (b * nq + i, q_blk0 // hpb + h)),
            pl.BlockSpec((seq, wb), lambda b, h, i: (b, k_blk0 // hpb + h)),
            pl.BlockSpec((seq, wb), lambda b, h, i: (b, v_blk0 // hpb + h)),
            pl.BlockSpec((tq, n_h), lambda b, h, i: (b * nq + i, 0)),
            pl.BlockSpec((n_h, nq, tq), lambda b, h, i: (0, b, 0)),
        ],
        out_specs=pl.BlockSpec((tq, wb), lambda b, h, i: (b * nq + i, h)),
        out_shape=jax.ShapeDtypeStruct((m, n_h * HEAD_DIM), BF16),
        scratch_shapes=[
            pltpu.VMEM((hpb, tq, HEAD_DIM), F32),
            pltpu.VMEM((hpb, tq, LANES), F32),
            pltpu.VMEM((hpb, tq, LANES), F32),
        ],
        compiler_params=_params(("parallel", "parallel", "arbitrary")),
        name="fox_mixer",
    )(qkv, qkv, qkv, cum_tok, cum_blk)


def _outproj_kernel(oa_ref, ob_ref, w_ref, res_hbm, g_ref, b_ref, h_ref, hb_ref, *,
                    alpha, n_ka, n_k, rows, col_chunk):
    i = pl.program_id(0)
    k = pl.program_id(1)
    tm, d = h_ref.shape

    @pl.when(k == 0)
    def _():
        pltpu.sync_copy(res_hbm.at[pl.ds(pl.multiple_of(i * tm, tm), tm), :], h_ref)

        def body(r, carry):
            r0 = pl.multiple_of(r * rows, rows)
            h_ref[pl.ds(r0, rows), :] = alpha * h_ref[pl.ds(r0, rows), :]
            return carry
        lax.fori_loop(0, tm // rows, body, 0)

    def accumulate(src_ref):
        src = src_ref[...]
        for c in range(d // col_chunk):
            sl = slice(c * col_chunk, (c + 1) * col_chunk)
            h_ref[:, sl] += jnp.dot(src, w_ref[:, sl], preferred_element_type=F32)

    @pl.when(k < n_ka)
    def _():
        accumulate(oa_ref)

    @pl.when(k >= n_ka)
    def _():
        accumulate(ob_ref)

    @pl.when(k == n_k - 1)
    def _():
        _ln_epilogue(h_ref, hb_ref, g_ref, b_ref, rows)


def _outproj_ln(oa, ob, w, res, g, b, alpha):
    m, wa = oa.shape
    wbw = ob.shape[1]
    d = w.shape[1]
    tm = _tile(m, 1024)
    tk = _tile(math.gcd(wa, wbw), 512)
    n_ka, n_kb = wa // tk, wbw // tk
    n_k = n_ka + n_kb
    kern = functools.partial(_outproj_kernel, alpha=alpha, n_ka=n_ka, n_k=n_k,
                             rows=_tile(tm, LN_ROWS_PER_STEP), col_chunk=_tile(d, 512))
    single = dict(pipeline_mode=pl.Buffered(1))
    return pl.pallas_call(
        kern,
        grid=(m // tm, n_k),
        in_specs=[
            pl.BlockSpec((tm, tk), lambda i, k: (i, jnp.minimum(k, n_ka - 1))),
            pl.BlockSpec((tm, tk), lambda i, k: (i, jnp.maximum(k - n_ka, 0))),
            pl.BlockSpec((tk, d), lambda i, k: (k, 0)),
            pl.BlockSpec(memory_space=pl.ANY),
            pl.BlockSpec((1, d), lambda i, k: (0, 0)),
            pl.BlockSpec((1, d), lambda i, k: (0, 0)),
        ],
        out_specs=[
            pl.BlockSpec((tm, d), lambda i, k: (i, 0), **single),
            pl.BlockSpec((tm, d), lambda i, k: (i, 0), **single),
        ],
        out_shape=[jax.ShapeDtypeStruct((m, d), F32), jax.ShapeDtypeStruct((m, d), BF16)],
        compiler_params=_params(("parallel", "arbitrary")),
        name="outproj_ln",
    )(oa, ob, w, res, g, b)


def _t5_bucket(rel):
    nb = N_BUCKETS // 2
    max_exact = nb // 2
    ret = jnp.where(rel > 0, nb, 0)
    n = jnp.abs(rel)
    nf = jnp.maximum(n, 1).astype(jnp.float32)
    large = max_exact + (jnp.log(nf / max_exact) / math.log(MAX_DISTANCE / max_exact)
                         * (nb - max_exact)).astype(jnp.int32)
    large = jnp.minimum(large, nb - 1)
    return ret + jnp.where(n < max_exact, n, large)


def _bias_tables(rel_bias):
    n_h = rel_bias.shape[1]
    width = 2 * LANES
    period = 2 * width
    m = jnp.arange(period, dtype=jnp.int32)
    m = jnp.where(m >= width, m - period, m)
    rel = jnp.stack([m - LANES, m])
    u = jnp.moveaxis(rel_bias[_t5_bucket(rel)], -1, 1).astype(F32)
    flat = jnp.tile(u, (1, 1, Q_BLOCK))[..., :Q_BLOCK * (period - 1)]
    tab = flat.reshape(2, n_h, Q_BLOCK, period - 1)[..., :width]
    far = rel_bias[_t5_bucket(jnp.full((), -(MAX_DISTANCE + 1), jnp.int32))].astype(F32)
    return tab, far


def _hybrid_mixer_ln(h, hb, w_in, b_f, kv_norm_g, idx_k_g, idx_k_b, w_uk, w_uv, rel_bias, w_out,
                     ln_g, ln_b, alpha, batch, seq):
    m, d = h.shape
    n_h = w_uk.shape[0]
    c_lat = w_uk.shape[2]
    n_hf = b_f.shape[0]
    w_dsa = n_h * HEAD_DIM
    w_fox = n_hf * HEAD_DIM
    d_in = w_in.shape[1]
    n_ih = (d_in - w_dsa - c_lat - IDX_DIM - 3 * w_fox - n_hf) // (IDX_DIM + 1)
    w_qi = n_ih * IDX_DIM
    assert w_dsa + c_lat + w_qi + IDX_DIM + n_ih + 3 * w_fox + n_hf == d_in
    assert w_qi % w_dsa == 0 and seq % (2 * LANES) == 0

    o0 = 0
    w_qa = w_in[:, o0:o0 + w_dsa]; o0 += w_dsa
    w_ckv = w_in[:, o0:o0 + c_lat]; o0 += c_lat
    w_qidx = w_in[:, o0:o0 + w_qi]; o0 += w_qi
    w_kidx = w_in[:, o0:o0 + IDX_DIM]; o0 += IDX_DIM
    w_widx = w_in[:, o0:o0 + n_ih]; o0 += n_ih
    w_qkv_fox = w_in[:, o0:o0 + 3 * w_fox]; o0 += 3 * w_fox
    w_f = w_in[:, o0:o0 + n_hf]

    w_main = jnp.concatenate([w_qidx, w_qa, w_qkv_fox], axis=1).astype(BF16)
    qkv = _proj(hb, w_main)

    n_small = c_lat + IDX_DIM + n_ih
    pad = (-n_small) % LANES
    w_small = jnp.pad(jnp.concatenate([w_ckv, w_kidx, w_widx], axis=1), ((0, 0), (0, pad))).astype(BF16)
    ckv, ki, wi, cum_t = _proj_small(
        hb, w_small, w_f.T.astype(BF16), b_f.reshape(n_hf, 1).astype(F32),
        kv_norm_g.reshape(1, c_lat), idx_k_g.reshape(1, IDX_DIM), idx_k_b.reshape(1, IDX_DIM),
        c_lat=c_lat, n_ih=n_ih, seq=seq)

    tab, far = _bias_tables(rel_bias)
    tab, far = tab * LOG2E, far * LOG2E
    o_a = _dsa(qkv, wi, ki, ckv, w_uk.astype(BF16), w_uv.astype(BF16), tab, far,
               batch=batch, seq=seq, n_ih=n_ih, n_h=n_h, qi_blk=0, qa_blk=w_qi // w_dsa)

    tq = _tile(seq, 512)
    blk0 = (w_qi + w_dsa) // HEAD_DIM
    o_b = _fox(qkv, cum_t.T, cum_t.reshape(n_hf, m // tq, tq), batch=batch, seq=seq, n_h=n_hf,
               q_blk0=blk0, k_blk0=blk0 + n_hf, v_blk0=blk0 + 2 * n_hf)

    return _outproj_ln(o_a, o_b, w_out.astype(BF16), h, ln_g.reshape(1, d), ln_b.reshape(1, d), alpha)


def kernel(x, ffn1_w_gate, ffn1_w_up, ffn1_w_down, ln1_g, ln1_b, w_in, b_f, kv_norm_g, idx_k_g,
           idx_k_b, w_uk, w_uv, rel_bias, w_out, ln2_g, ln2_b, ffn2_w_gate, ffn2_w_up,
           ffn2_w_down, ln3_g, ln3_b):
    batch, seq, d = x.shape
    depth = ffn1_w_gate.shape[0]
    alpha = (2.0 * depth) ** 0.25
    h = x.reshape(batch * seq, d)
    for l in range(depth):
        h, hb = _ffn_ln(h, ffn1_w_gate[l].astype(BF16), ffn1_w_up[l].astype(BF16),
                        ffn1_w_down[l].astype(BF16), ln1_g[l].reshape(1, d), ln1_b[l].reshape(1, d),
                        alpha)
        h, hb = _hybrid_mixer_ln(h, hb, w_in[l], b_f[l], kv_norm_g[l], idx_k_g[l], idx_k_b[l],
                                 w_uk[l], w_uv[l], rel_bias, w_out[l], ln2_g[l], ln2_b[l], alpha,
                                 batch, seq)
        h, hb = _ffn_ln(h, ffn2_w_gate[l].astype(BF16), ffn2_w_up[l].astype(BF16),
                        ffn2_w_down[l].astype(BF16), ln3_g[l].reshape(1, d), ln3_b[l].reshape(1, d),
                        alpha)
    return h.reshape(batch, seq, d)
```

```python
import functools
import math

import jax
import jax.numpy as jnp
import numpy as np
from jax import lax
from jax.experimental import pallas as pl
from jax.experimental.pallas import tpu as pltpu

CHUNK = 64
Q_BLOCK = 128
HEAD_DIM = 128
IDX_DIM = 128
TOPK_MAX = 256
N_BUCKETS = 32
MAX_DISTANCE = 128
LN_EPS = 1e-5
RMS_EPS = 1e-6
NEG = -1e30
LOG2E = math.log2(math.e)
DSA_HEADS_PER_GROUP = 4
FOX_HEADS_PER_BLOCK = 4
LN_ROWS_PER_STEP = 64
LN_ROW_GROUPS = 4

LANES = 128
VMEM_LIMIT_BYTES = 56 * 1024 * 1024

BF16 = jnp.bfloat16
F32 = jnp.float32

INT_MIN = -(2 ** 31)
CODE_NEG_INF = -(2 ** 31) + 0x7FFFFF
F32_LOWEST = float(np.finfo(np.float32).min)


def _tile(dim, pref):
    t = min(dim, pref)
    while dim % t:
        t //= 2
    return t


def _params(sem):
    return pltpu.CompilerParams(dimension_semantics=sem, vmem_limit_bytes=VMEM_LIMIT_BYTES)


def _widen(x, width):
    reps = width // x.shape[1]
    return x if reps == 1 else jnp.concatenate([x] * reps, axis=-1)


def _layer_norm_rows(y, g, b):
    mu = jnp.mean(y, axis=-1, keepdims=True)
    d = y - mu
    var = jnp.mean(d * d, axis=-1, keepdims=True)
    return d * lax.rsqrt(var + LN_EPS) * g + b


def _ln_epilogue(h_ref, hb_ref, g_ref, b_ref, rows):
    g = g_ref[...]
    b = b_ref[...]
    sub = rows // LN_ROW_GROUPS

    def body(r, carry):
        r0 = pl.multiple_of(r * rows, rows)
        sls = [pl.ds(r0 + k * sub, sub) for k in range(LN_ROW_GROUPS)]
        mus = [jnp.mean(h_ref[sl, :], axis=-1, keepdims=True) for sl in sls]
        rstds = []
        for sl, mu in zip(sls, mus):
            d = h_ref[sl, :] - mu
            rstds.append(lax.rsqrt(jnp.mean(d * d, axis=-1, keepdims=True) + LN_EPS))
        for sl, mu, rstd in zip(sls, mus, rstds):
            out = (h_ref[sl, :] - mu) * rstd * g + b
            h_ref[sl, :] = out
            if hb_ref is not None:
                hb_ref[sl, :] = out.astype(BF16)
        return carry

    lax.fori_loop(0, h_ref.shape[0] // rows, body, 0)


def _ffn_kernel(x_hbm, wg_ref, wu_ref, wd_ref, g_ref, b_ref, h_ref, *rest,
                alpha, n_f, rows, col_chunk):
    hb_ref = rest[0] if len(rest) == 2 else None
    xb_ref = rest[-1]
    i = pl.program_id(0)
    f = pl.program_id(1)
    tm, d = h_ref.shape

    @pl.when(f == 0)
    def _():
        pltpu.sync_copy(x_hbm.at[pl.ds(pl.multiple_of(i * tm, tm), tm), :], h_ref)

        def body(r, carry):
            r0 = pl.multiple_of(r * rows, rows)
            xr = h_ref[pl.ds(r0, rows), :]
            xb_ref[pl.ds(r0, rows), :] = xr.astype(BF16)
            h_ref[pl.ds(r0, rows), :] = alpha * xr
            return carry
        lax.fori_loop(0, tm // rows, body, 0)

    xb = xb_ref[...]
    gate = jnp.dot(xb, wg_ref[...], preferred_element_type=F32)
    up = jnp.dot(xb, wu_ref[...], preferred_element_type=F32)
    act = (0.5 * (gate * jax.nn.sigmoid(gate)) * up).astype(BF16)
    for c in range(d // col_chunk):
        sl = slice(c * col_chunk, (c + 1) * col_chunk)
        h_ref[:, sl] += jnp.dot(act, wd_ref[:, sl], preferred_element_type=F32)

    @pl.when(f == n_f - 1)
    def _():
        _ln_epilogue(h_ref, hb_ref, g_ref, b_ref, rows)


def _ffn_ln(x, wg, wu, wd, g, b, alpha, emit_bf16):
    m, d = x.shape
    f = wg.shape[1]
    tm = _tile(m, 1024)
    tf = _tile(f, 256)
    n_f = f // tf
    kern = functools.partial(_ffn_kernel, alpha=alpha, n_f=n_f, rows=_tile(tm, LN_ROWS_PER_STEP),
                             col_chunk=_tile(d, 512))
    single = dict(pipeline_mode=pl.Buffered(1))
    out_specs = [pl.BlockSpec((tm, d), lambda i, j: (i, 0), **(single if emit_bf16 else {}))]
    out_shape = [jax.ShapeDtypeStruct((m, d), F32)]
    if emit_bf16:
        out_specs.append(pl.BlockSpec((tm, d), lambda i, j: (i, 0), **single))
        out_shape.append(jax.ShapeDtypeStruct((m, d), BF16))
    outs = pl.pallas_call(
        kern,
        grid=(m // tm, n_f),
        in_specs=[
            pl.BlockSpec(memory_space=pl.ANY),
            pl.BlockSpec((d, tf), lambda i, j: (0, j)),
            pl.BlockSpec((d, tf), lambda i, j: (0, j)),
            pl.BlockSpec((tf, d), lambda i, j: (j, 0)),
            pl.BlockSpec((1, d), lambda i, j: (0, 0)),
            pl.BlockSpec((1, d), lambda i, j: (0, 0)),
        ],
        out_specs=out_specs,
        out_shape=out_shape,
        scratch_shapes=[pltpu.VMEM((tm, d), BF16)],
        compiler_params=_params(("parallel", "arbitrary")),
        name="ffn_ln",
    )(x, wg, wu, wd, g, b)
    return (outs[0], outs[1]) if emit_bf16 else (outs[0], None)


def _proj_kernel(x_ref, w_ref, o_ref):
    o_ref[...] = jnp.dot(x_ref[...], w_ref[...], preferred_element_type=F32).astype(o_ref.dtype)


def _proj(xb, w):
    m, d = xb.shape
    n = w.shape[1]
    tm = _tile(m, 1024)
    tn = _tile(n, 512)
    return pl.pallas_call(
        _proj_kernel,
        grid=(m // tm, n // tn),
        in_specs=[
            pl.BlockSpec((tm, d), lambda i, j: (i, 0)),
            pl.BlockSpec((d, tn), lambda i, j: (0, j)),
        ],
        out_specs=pl.BlockSpec((tm, tn), lambda i, j: (i, j)),
        out_shape=jax.ShapeDtypeStruct((m, n), BF16),
        compiler_params=_params(("parallel", "arbitrary")),
        name="proj_main",
    )(xb, w)


def _split3(v):
    hi = v.astype(BF16)
    r1 = v - hi.astype(F32)
    mid = r1.astype(BF16)
    lo = (r1 - mid.astype(F32)).astype(BF16)
    return hi, mid, lo


def _proj_small_kernel(x_ref, ws_ref, wft_ref, bf_ref, kvg_ref, kg_ref, kb_ref,
                       ckv_ref, ki_ref, wi_ref, cum_ref, carry_ref, *, c_lat, n_ih, tiles_per_seq):
    i = pl.program_id(0)
    x = x_ref[...]
    tm = x.shape[0]
    y = jnp.dot(x, ws_ref[...], preferred_element_type=F32)
    c = y[:, :c_lat]
    c = c * lax.rsqrt(jnp.mean(c * c, axis=-1, keepdims=True) + RMS_EPS) * kvg_ref[...]
    ckv_ref[...] = c.astype(BF16)
    k = y[:, c_lat:c_lat + IDX_DIM]
    ki_ref[...] = _layer_norm_rows(k, kg_ref[...], kb_ref[...]).astype(BF16)
    wi_ref[...] = y[:, c_lat + IDX_DIM:c_lat + IDX_DIM + n_ih]

    ft = lax.dot_general(wft_ref[...], x, (((1,), (1,)), ((), ())), preferred_element_type=F32)
    z = ft + bf_ref[...]
    log_f = jnp.minimum(z, 0.0) - jnp.log1p(jnp.exp(-jnp.abs(z)))

    @pl.when(i % tiles_per_seq == 0)
    def _():
        carry_ref[...] = jnp.zeros_like(carry_ref)

    row = lax.broadcasted_iota(jnp.int32, (tm, tm), 0)
    col = lax.broadcasted_iota(jnp.int32, (tm, tm), 1)
    tri = jnp.where(row <= col, 1.0, 0.0).astype(BF16)
    hi, mid, lo = _split3(log_f)
    cs = (jnp.dot(hi, tri, preferred_element_type=F32)
          + jnp.dot(mid, tri, preferred_element_type=F32)
          + jnp.dot(lo, tri, preferred_element_type=F32))
    cum = cs + carry_ref[:, 0:1]
    cum_ref[...] = cum
    carry_ref[...] = jnp.broadcast_to(cum[:, tm - 1:tm], carry_ref.shape)


def _proj_small(xb, ws, wft, bf, kvg, kg, kb, *, c_lat, n_ih, seq):
    m, d = xb.shape
    ns = ws.shape[1]
    hf = wft.shape[0]
    tm = _tile(seq, 512)
    kern = functools.partial(_proj_small_kernel, c_lat=c_lat, n_ih=n_ih, tiles_per_seq=seq // tm)
    return pl.pallas_call(
        kern,
        grid=(m // tm,),
        in_specs=[
            pl.BlockSpec((tm, d), lambda i: (i, 0)),
            pl.BlockSpec((d, ns), lambda i: (0, 0)),
            pl.BlockSpec((hf, d), lambda i: (0, 0)),
            pl.BlockSpec((hf, 1), lambda i: (0, 0)),
            pl.BlockSpec((1, c_lat), lambda i: (0, 0)),
            pl.BlockSpec((1, IDX_DIM), lambda i: (0, 0)),
            pl.BlockSpec((1, IDX_DIM), lambda i: (0, 0)),
        ],
        out_specs=[
            pl.BlockSpec((tm, c_lat), lambda i: (i, 0)),
            pl.BlockSpec((tm, IDX_DIM), lambda i: (i, 0)),
            pl.BlockSpec((tm, n_ih), lambda i: (i, 0)),
            pl.BlockSpec((hf, tm), lambda i: (0, i)),
        ],
        out_shape=[
            jax.ShapeDtypeStruct((m, c_lat), BF16),
            jax.ShapeDtypeStruct((m, IDX_DIM), BF16),
            jax.ShapeDtypeStruct((m, n_ih), F32),
            jax.ShapeDtypeStruct((hf, m), F32),
        ],
        scratch_shapes=[pltpu.VMEM((hf, LANES), F32)],
        compiler_params=_params(("arbitrary",)),
        name="proj_small",
    )(xb, ws, wft, bf, kvg, kg, kb)


def _dsa_kernel(qi_ref, qa_ref, wi_ref, ki_ref, ckv_ref, wuk_ref, wuv_ref, tab_ref, bfar_ref,
                o_ref, score_ref, score_t_ref, qlat_ref, acc_ref, m_ref, l_ref, p_ref, alpha_ref, *,
                n_ih, n_h, k_sel, tk, hpg):
    i = pl.program_id(1)
    start = i * Q_BLOCK
    qb = Q_BLOCK
    bpc = tk // LANES
    scale = HEAD_DIM ** -0.5
    w_fold = (IDX_DIM ** -0.5) * (n_ih ** -0.5)

    wcol = wi_ref[...] * w_fold
    row = lax.broadcasted_iota(jnp.int32, (qb, tk), 0)
    col = lax.broadcasted_iota(jnp.int32, (qb, tk), 1)
    limit = start + jnp.where(row < CHUNK, CHUNK, 2 * CHUNK)
    n_chunks = (i + bpc) // bpc

    def score_chunk(c, carry):
        k0 = pl.multiple_of(c * tk, tk)
        kblk = ki_ref[pl.ds(k0, tk), :]
        sc = jnp.zeros((qb, tk), F32)
        for h in range(n_ih):
            s = lax.dot_general(qi_ref[:, h * IDX_DIM:(h + 1) * IDX_DIM], kblk,
                                (((1,), (1,)), ((), ())), preferred_element_type=F32)
            sc = sc + jnp.maximum(s, 0.0) * wcol[:, h:h + 1]
        sc = jnp.where(col + k0 < limit, sc, -jnp.inf)
        for j in range(bpc):
            blk = sc[:, j * LANES:(j + 1) * LANES]
            score_ref[c * bpc + j] = blk
            score_t_ref[c * bpc + j] = blk.T
        return carry

    lax.fori_loop(0, n_chunks, score_chunk, 0)

    def code_to_float(code):
        bits = jnp.where(code >= 0, code, code ^ 0x7FFFFFFF)
        return lax.bitcast_convert_type(bits, F32)

    def bit_body(bi, code):
        cand = code + lax.shift_left(jnp.int32(1), 31 - bi)
        cf = jnp.concatenate([code_to_float(cand)] * (LANES // 8), axis=0)

        def count_chunk(c, cnt):
            for j in range(bpc):
                hit = jnp.where(score_t_ref[c * bpc + j] >= cf, 1.0, 0.0)
                cnt = cnt + jnp.sum(hit.reshape(LANES // 8, 8, qb), axis=0)
            return cnt

        cnt = lax.fori_loop(0, n_chunks, count_chunk, jnp.zeros((8, qb), F32))
        total = jnp.sum(cnt, axis=0, keepdims=True)
        return jnp.where(total >= k_sel, cand, code)

    code = lax.fori_loop(0, 32, bit_body, jnp.full((8, qb), INT_MIN, jnp.int32))
    thr_row = jnp.where(code <= CODE_NEG_INF, F32_LOWEST, code_to_float(code))
    thr = jnp.concatenate([thr_row] * (qb // 8), axis=0).T

    for h in range(n_h):
        ql = jnp.dot(qa_ref[:, h * HEAD_DIM:(h + 1) * HEAD_DIM], wuk_ref[h],
                     preferred_element_type=F32)
        qlat_ref[h * qb:(h + 1) * qb, :] = ql.astype(BF16)

    def flash_step(k0, width, sel, bias_of_head, first):
        cblk = ckv_ref[pl.ds(k0, width), :]
        s_of = {}
        for g in range(n_h // hpg):
            sg = lax.dot_general(qlat_ref[g * hpg * qb:(g + 1) * hpg * qb, :], cblk,
                                 (((1,), (1,)), ((), ())), preferred_element_type=F32)
            for hh in range(hpg):
                s_of[g * hpg + hh] = sg[hh * qb:(hh + 1) * qb, :]
        for g in range(n_h // hpg):
            rg = slice(g * hpg * qb, (g + 1) * hpg * qb)
            heads = [g * hpg + hh for hh in range(hpg)]
            sh, m_new = {}, {}
            for h in heads:
                rs = slice(h * qb, (h + 1) * qb)
                sh[h] = jnp.where(sel, s_of[h] * (scale * LOG2E) + bias_of_head(h), NEG)
                row_max = jnp.broadcast_to(jnp.max(sh[h], axis=-1, keepdims=True), (qb, LANES))
                m_new[h] = row_max if first else jnp.maximum(m_ref[rs, :], row_max)
            for h in heads:
                rs = slice(h * qb, (h + 1) * qb)
                p = jnp.exp2(sh[h] - _widen(m_new[h], width))
                row_sum = jnp.broadcast_to(jnp.sum(p, axis=-1, keepdims=True), (qb, LANES))
                if first:
                    l_ref[rs, :] = row_sum
                else:
                    a = jnp.exp2(m_ref[rs, :] - m_new[h])
                    l_ref[rs, :] = a * l_ref[rs, :] + row_sum
                    alpha_ref[rs, :] = a
                m_ref[rs, :] = m_new[h]
                p_ref[rs, 0:width] = p.astype(BF16)
            pv = jnp.dot(p_ref[rg, 0:width], cblk, preferred_element_type=F32)
            if first:
                acc_ref[rg, :] = pv
            else:
                acc_ref[rg, :] = acc_ref[rg, :] * _widen(alpha_ref[rg, :], pv.shape[1]) + pv

    wb = jnp.maximum(i - 1, 0)
    tsel = jnp.where(i == 0, 1, 0)
    scw = jnp.concatenate([score_ref[wb], score_ref[wb + 1]], axis=-1)
    selw = scw >= _widen(thr, 2 * LANES)
    flash_step(pl.multiple_of(wb * LANES, LANES), 2 * LANES, selw, lambda h: tab_ref[tsel, h],
               first=True)

    far_end = (i - 1) * qb

    def far_chunk(c, carry):
        k0 = pl.multiple_of(c * tk, tk)
        sc = jnp.concatenate([score_ref[c * bpc + j] for j in range(bpc)], axis=-1)
        sel = (sc >= _widen(thr, tk)) & (col + k0 < far_end)
        flash_step(k0, tk, sel, lambda h: bfar_ref[h], first=False)
        return carry

    n_far = jnp.maximum((i - 1 + bpc - 1) // bpc, 0)
    lax.fori_loop(0, n_far, far_chunk, 0)

    for h in range(n_h):
        rs = slice(h * qb, (h + 1) * qb)
        o_lat = (acc_ref[rs, :] / _widen(l_ref[rs, :], acc_ref.shape[1])).astype(BF16)
        o_ref[:, h * HEAD_DIM:(h + 1) * HEAD_DIM] = jnp.dot(
            o_lat, wuv_ref[h], preferred_element_type=F32).astype(o_ref.dtype)


def _dsa(qkv, wi, ki, ckv, wuk, wuv, tab, bfar, *, batch, seq, n_ih, n_h, qi_blk, qa_blk):
    m = qkv.shape[0]
    c_lat = ckv.shape[1]
    nq = seq // Q_BLOCK
    k_sel = min(TOPK_MAX, seq // 4)
    tk = _tile(seq, 512)
    rows = n_h * Q_BLOCK
    kern = functools.partial(_dsa_kernel, n_ih=n_ih, n_h=n_h, k_sel=k_sel, tk=tk,
                             hpg=math.gcd(n_h, DSA_HEADS_PER_GROUP))
    const = dict(pipeline_mode=pl.Buffered(1))
    return pl.pallas_call(
        kern,
        grid=(batch, nq),
        in_specs=[
            pl.BlockSpec((Q_BLOCK, n_ih * IDX_DIM), lambda b, i: (b * nq + i, qi_blk)),
            pl.BlockSpec((Q_BLOCK, n_h * HEAD_DIM), lambda b, i: (b * nq + i, qa_blk)),
            pl.BlockSpec((Q_BLOCK, n_ih), lambda b, i: (b * nq + i, 0)),
            pl.BlockSpec((seq, IDX_DIM), lambda b, i: (b, 0)),
            pl.BlockSpec((seq, c_lat), lambda b, i: (b, 0)),
            pl.BlockSpec((n_h, HEAD_DIM, c_lat), lambda b, i: (0, 0, 0), **const),
            pl.BlockSpec((n_h, c_lat, HEAD_DIM), lambda b, i: (0, 0, 0), **const),
            pl.BlockSpec((2, n_h, Q_BLOCK, 2 * LANES), lambda b, i: (0, 0, 0, 0), **const),
            pl.BlockSpec(memory_space=pltpu.SMEM),
        ],
        out_specs=pl.BlockSpec((Q_BLOCK, n_h * HEAD_DIM), lambda b, i: (b * nq + i, 0)),
        out_shape=jax.ShapeDtypeStruct((m, n_h * HEAD_DIM), BF16),
        scratch_shapes=[
            pltpu.VMEM((seq // LANES, Q_BLOCK, LANES), F32),
            pltpu.VMEM((seq // LANES, LANES, Q_BLOCK), F32),
            pltpu.VMEM((rows, c_lat), BF16),
            pltpu.VMEM((rows, c_lat), F32),
            pltpu.VMEM((rows, LANES), F32),
            pltpu.VMEM((rows, LANES), F32),
            pltpu.VMEM((rows, tk), BF16),
            pltpu.VMEM((rows, LANES), F32),
        ],
        compiler_params=_params(("parallel", "arbitrary")),
        name="dsa_mixer",
    )(qkv, qkv, wi, ki, ckv, wuk, wuv, tab, bfar)


def _fox_kernel(q_ref, k_ref, v_ref, cq_ref, ck_ref, o_ref, acc_ref, m_ref, l_ref, *, tq, hpb):
    hp = pl.program_id(1)
    i = pl.program_id(2)
    c1 = (HEAD_DIM ** -0.5) * LOG2E
    lane = lax.broadcasted_iota(jnp.int32, cq_ref.shape, 1)
    cq_all = cq_ref[...] * LOG2E
    cq = [jnp.sum(jnp.where(lane == hp * hpb + hh, cq_all, 0.0), axis=-1, keepdims=True)
          for hh in range(hpb)]

    def step(kb, first):
        causal = first
        k0 = pl.multiple_of(kb * tq, tq)
        logits = []
        for hh in range(hpb):
            ls = slice(hh * HEAD_DIM, (hh + 1) * HEAD_DIM)
            logits.append(lax.dot_general(q_ref[:, ls], k_ref[pl.ds(k0, tq), ls],
                                          (((1,), (1,)), ((), ())), preferred_element_type=F32))
        sh, m_new = [], []
        for hh in range(hpb):
            ck = ck_ref[pl.ds(hp * hpb + hh, 1), pl.ds(kb, 1), :].reshape(1, tq) * LOG2E
            s = logits[hh] * c1 + (cq[hh] - ck)
            if causal:
                row = lax.broadcasted_iota(jnp.int32, (tq, tq), 0)
                col = lax.broadcasted_iota(jnp.int32, (tq, tq), 1)
                s = jnp.where(col <= row, s, NEG)
            sh.append(s)
            row_max = jnp.broadcast_to(jnp.max(s, axis=-1, keepdims=True), (tq, LANES))
            m_new.append(row_max if first else jnp.maximum(m_ref[hh], row_max))
        for hh in range(hpb):
            ls = slice(hh * HEAD_DIM, (hh + 1) * HEAD_DIM)
            p = jnp.exp2(sh[hh] - _widen(m_new[hh], tq))
            row_sum = jnp.broadcast_to(jnp.sum(p, axis=-1, keepdims=True), (tq, LANES))
            pv = jnp.dot(p.astype(BF16), v_ref[pl.ds(k0, tq), ls], preferred_element_type=F32)
            if first:
                l_ref[hh] = row_sum
                acc_ref[hh] = pv
            else:
                a = jnp.exp2(m_ref[hh] - m_new[hh])
                l_ref[hh] = a * l_ref[hh] + row_sum
                acc_ref[hh] = acc_ref[hh] * a + pv
            m_ref[hh] = m_new[hh]

    def full_block(kb, carry):
        step(kb, False)
        return carry

    step(i, True)
    lax.fori_loop(0, i, full_block, 0)
    for hh in range(hpb):
        o_ref[:, hh * HEAD_DIM:(hh + 1) * HEAD_DIM] = (acc_ref[hh] / l_ref[hh]).astype(o_ref.dtype)


def _fox(qkv, cum_tok, cum_blk, *, batch, seq, n_h, q_blk0, k_blk0, v_blk0):
    m = qkv.shape[0]
    tq = _tile(seq, 512)
    nq = seq // tq
    hpb = math.gcd(n_h, FOX_HEADS_PER_BLOCK)
    assert q_blk0 % hpb == 0 and k_blk0 % hpb == 0 and v_blk0 % hpb == 0
    wb = hpb * HEAD_DIM
    kern = functools.partial(_fox_kernel, tq=tq, hpb=hpb)
    return pl.pallas_call(
        kern,
        grid=(batch, n_h // hpb, nq),
        in_specs=[
            pl.BlockSpec((tq, wb), lambda b, h, i: (b * nq + i, q_blk0 // hpb + h)),
            pl.BlockSpec((seq, wb), lambda b, h, i: (b, k_blk0 // hpb + h)),
            pl.BlockSpec((seq, wb), lambda b, h, i: (b, v_blk0 // hpb + h)),
            pl.BlockSpec((tq, n_h), lambda b, h, i: (b * nq + i, 0)),
            pl.BlockSpec((n_h, nq, tq), lambda b, h, i: (0, b, 0)),
        ],
        out_specs=pl.BlockSpec((tq, wb), lambda b, h, i: (b * nq + i, h)),
        out_shape=jax.ShapeDtypeStruct((m, n_h * HEAD_DIM), BF16),
        scratch_shapes=[
            pltpu.VMEM((hpb, tq, HEAD_DIM), F32),
            pltpu.VMEM((hpb, tq, LANES), F32),
            pltpu.VMEM((hpb, tq, LANES), F32),
        ],
        compiler_params=_params(("parallel", "parallel", "arbitrary")),
        name="fox_mixer",
    )(qkv, qkv, qkv, cum_tok, cum_blk)


def _outproj_kernel(oa_ref, ob_ref, w_ref, res_hbm, g_ref, b_ref, h_ref, hb_ref, *,
                    alpha, n_ka, n_k, rows, col_chunk):
    i = pl.program_id(0)
    k = pl.program_id(1)
    tm, d = h_ref.shape

    @pl.when(k == 0)
    def _():
        pltpu.sync_copy(res_hbm.at[pl.ds(pl.multiple_of(i * tm, tm), tm), :], h_ref)

        def body(r, carry):
            r0 = pl.multiple_of(r * rows, rows)
            h_ref[pl.ds(r0, rows), :] = alpha * h_ref[pl.ds(r0, rows), :]
            return carry
        lax.fori_loop(0, tm // rows, body, 0)

    def accumulate(src_ref):
        src = src_ref[...]
        for c in range(d // col_chunk):
            sl = slice(c * col_chunk, (c + 1) * col_chunk)
            h_ref[:, sl] += jnp.dot(src, w_ref[:, sl], preferred_element_type=F32)

    @pl.when(k < n_ka)
    def _():
        accumulate(oa_ref)

    @pl.when(k >= n_ka)
    def _():
        accumulate(ob_ref)

    @pl.when(k == n_k - 1)
    def _():
        _ln_epilogue(h_ref, hb_ref, g_ref, b_ref, rows)


def _outproj_ln(oa, ob, w, res, g, b, alpha):
    m, wa = oa.shape
    wbw = ob.shape[1]
    d = w.shape[1]
    tm = _tile(m, 1024)
    tk = _tile(math.gcd(wa, wbw), 512)
    n_ka, n_kb = wa // tk, wbw // tk
    n_k = n_ka + n_kb
    kern = functools.partial(_outproj_kernel, alpha=alpha, n_ka=n_ka, n_k=n_k,
                             rows=_tile(tm, LN_ROWS_PER_STEP), col_chunk=_tile(d, 512))
    single = dict(pipeline_mode=pl.Buffered(1))
    return pl.pallas_call(
        kern,
        grid=(m // tm, n_k),
        in_specs=[
            pl.BlockSpec((tm, tk), lambda i, k: (i, jnp.minimum(k, n_ka - 1))),
            pl.BlockSpec((tm, tk), lambda i, k: (i, jnp.maximum(k - n_ka, 0))),
            pl.BlockSpec((tk, d), lambda i, k: (k, 0)),
            pl.BlockSpec(memory_space=pl.ANY),
            pl.BlockSpec((1, d), lambda i, k: (0, 0)),
            pl.BlockSpec((1, d), lambda i, k: (0, 0)),
        ],
        out_specs=[
            pl.BlockSpec((tm, d), lambda i, k: (i, 0), **single),
            pl.BlockSpec((tm, d), lambda i, k: (i, 0), **single),
        ],
        out_shape=[jax.ShapeDtypeStruct((m, d), F32), jax.ShapeDtypeStruct((m, d), BF16)],
        compiler_params=_params(("parallel", "arbitrary")),
        name="outproj_ln",
    )(oa, ob, w, res, g, b)


def _t5_bucket(rel):
    nb = N_BUCKETS // 2
    max_exact = nb // 2
    ret = jnp.where(rel > 0, nb, 0)
    n = jnp.abs(rel)
    nf = jnp.maximum(n, 1).astype(jnp.float32)
    large = max_exact + (jnp.log(nf / max_exact) / math.log(MAX_DISTANCE / max_exact)
                         * (nb - max_exact)).astype(jnp.int32)
    large = jnp.minimum(large, nb - 1)
    return ret + jnp.where(n < max_exact, n, large)


def _bias_tables(rel_bias):
    n_h = rel_bias.shape[1]
    width = 2 * LANES
    period = 2 * width
    m = jnp.arange(period, dtype=jnp.int32)
    m = jnp.where(m >= width, m - period, m)
    rel = jnp.stack([m - LANES, m])
    u = jnp.moveaxis(rel_bias[_t5_bucket(rel)], -1, 1).astype(F32)
    flat = jnp.tile(u, (1, 1, Q_BLOCK))[..., :Q_BLOCK * (period - 1)]
    tab = flat.reshape(2, n_h, Q_BLOCK, period - 1)[..., :width]
    far = rel_bias[_t5_bucket(jnp.full((), -(MAX_DISTANCE + 1), jnp.int32))].astype(F32)
    return tab, far


def _hybrid_mixer_ln(h, hb, w_in, b_f, kv_norm_g, idx_k_g, idx_k_b, w_uk, w_uv, rel_bias, w_out,
                     ln_g, ln_b, alpha, batch, seq):
    m, d = h.shape
    n_h = w_uk.shape[0]
    c_lat = w_uk.shape[2]
    n_hf = b_f.shape[0]
    w_dsa = n_h * HEAD_DIM
    w_fox = n_hf * HEAD_DIM
    d_in = w_in.shape[1]
    n_ih = (d_in - w_dsa - c_lat - IDX_DIM - 3 * w_fox - n_hf) // (IDX_DIM + 1)
    w_qi = n_ih * IDX_DIM
    assert w_dsa + c_lat + w_qi + IDX_DIM + n_ih + 3 * w_fox + n_hf == d_in
    assert w_qi % w_dsa == 0 and seq % (2 * LANES) == 0

    o0 = 0
    w_qa = w_in[:, o0:o0 + w_dsa]; o0 += w_dsa
    w_ckv = w_in[:, o0:o0 + c_lat]; o0 += c_lat
    w_qidx = w_in[:, o0:o0 + w_qi]; o0 += w_qi
    w_kidx = w_in[:, o0:o0 + IDX_DIM]; o0 += IDX_DIM
    w_widx = w_in[:, o0:o0 + n_ih]; o0 += n_ih
    w_qkv_fox = w_in[:, o0:o0 + 3 * w_fox]; o0 += 3 * w_fox
    w_f = w_in[:, o0:o0 + n_hf]

    w_main = jnp.concatenate([w_qidx, w_qa, w_qkv_fox], axis=1).astype(BF16)
    qkv = _proj(hb, w_main)

    n_small = c_lat + IDX_DIM + n_ih
    pad = (-n_small) % LANES
    w_small = jnp.pad(jnp.concatenate([w_ckv, w_kidx, w_widx], axis=1), ((0, 0), (0, pad))).astype(BF16)
    ckv, ki, wi, cum_t = _proj_small(
        hb, w_small, w_f.T.astype(BF16), b_f.reshape(n_hf, 1).astype(F32),
        kv_norm_g.reshape(1, c_lat), idx_k_g.reshape(1, IDX_DIM), idx_k_b.reshape(1, IDX_DIM),
        c_lat=c_lat, n_ih=n_ih, seq=seq)

    tab, far = _bias_tables(rel_bias)
    tab, far = tab * LOG2E, far * LOG2E
    o_a = _dsa(qkv, wi, ki, ckv, w_uk.astype(BF16), w_uv.astype(BF16), tab, far,
               batch=batch, seq=seq, n_ih=n_ih, n_h=n_h, qi_blk=0, qa_blk=w_qi // w_dsa)

    tq = _tile(seq, 512)
    blk0 = (w_qi + w_dsa) // HEAD_DIM
    o_b = _fox(qkv, cum_t.T, cum_t.reshape(n_hf, m // tq, tq), batch=batch, seq=seq, n_h=n_hf,
               q_blk0=blk0, k_blk0=blk0 + n_hf, v_blk0=blk0 + 2 * n_hf)

    return _outproj_ln(o_a, o_b, w_out.astype(BF16), h, ln_g.reshape(1, d), ln_b.reshape(1, d), alpha)


def kernel(x, ffn1_w_gate, ffn1_w_up, ffn1_w_down, ln1_g, ln1_b, w_in, b_f, kv_norm_g, idx_k_g,
           idx_k_b, w_uk, w_uv, rel_bias, w_out, ln2_g, ln2_b, ffn2_w_gate, ffn2_w_up,
           ffn2_w_down, ln3_g, ln3_b):
    batch, seq, d = x.shape
    depth = ffn1_w_gate.shape[0]
    alpha = (2.0 * depth) ** 0.25
    h = x.reshape(batch * seq, d)
    for l in range(depth):
        h, hb = _ffn_ln(h, ffn1_w_gate[l].astype(BF16), ffn1_w_up[l].astype(BF16),
                        ffn1_w_down[l].astype(BF16), ln1_g[l].reshape(1, d), ln1_b[l].reshape(1, d),
                        alpha, emit_bf16=True)
        h, hb = _hybrid_mixer_ln(h, hb, w_in[l], b_f[l], kv_norm_g[l], idx_k_g[l], idx_k_b[l],
                                 w_uk[l], w_uv[l], rel_bias, w_out[l], ln2_g[l], ln2_b[l], alpha,
                                 batch, seq)
        h, _ = _ffn_ln(h, ffn2_w_gate[l].astype(BF16), ffn2_w_up[l].astype(BF16),
                       ffn2_w_down[l].astype(BF16), ln3_g[l].reshape(1, d), ln3_b[l].reshape(1, d),
                       alpha, emit_bf16=False)
    return h.reshape(batch, seq, d)
```

```python
import functools
import math

import jax
import jax.numpy as jnp
import numpy as np
from jax import lax
from jax.experimental import pallas as pl
from jax.experimental.pallas import tpu as pltpu

CHUNK = 64
Q_BLOCK = 128
HEAD_DIM = 128
IDX_DIM = 128
TOPK_MAX = 256
N_BUCKETS = 32
MAX_DISTANCE = 128
LN_EPS = 1e-5
RMS_EPS = 1e-6
NEG = -1e30
LOG2E = math.log2(math.e)
DSA_HEADS_PER_GROUP = 4
FOX_HEADS_PER_BLOCK = 4
LN_ROWS_PER_STEP = 64
LN_ROW_GROUPS = 4

LANES = 128
VMEM_LIMIT_BYTES = 56 * 1024 * 1024

BF16 = jnp.bfloat16
F32 = jnp.float32

INT_MIN = -(2 ** 31)
CODE_NEG_INF = -(2 ** 31) + 0x7FFFFF
F32_LOWEST = float(np.finfo(np.float32).min)


def _tile(dim, pref):
    t = min(dim, pref)
    while dim % t:
        t //= 2
    return t


def _params(sem):
    return pltpu.CompilerParams(dimension_semantics=sem, vmem_limit_bytes=VMEM_LIMIT_BYTES)


def _widen(x, width):
    reps = width // x.shape[1]
    return x if reps == 1 else jnp.concatenate([x] * reps, axis=-1)


def _layer_norm_rows(y, g, b):
    mu = jnp.mean(y, axis=-1, keepdims=True)
    d = y - mu
    var = jnp.mean(d * d, axis=-1, keepdims=True)
    return d * lax.rsqrt(var + LN_EPS) * g + b


def _ln_epilogue(h_ref, hb_ref, g_ref, b_ref, rows):
    g = g_ref[...]
    b = b_ref[...]
    sub = rows // LN_ROW_GROUPS

    def body(r, carry):
        r0 = pl.multiple_of(r * rows, rows)
        sls = [pl.ds(r0 + k * sub, sub) for k in range(LN_ROW_GROUPS)]
        mus = [jnp.mean(h_ref[sl, :], axis=-1, keepdims=True) for sl in sls]
        rstds = []
        for sl, mu in zip(sls, mus):
            d = h_ref[sl, :] - mu
            rstds.append(lax.rsqrt(jnp.mean(d * d, axis=-1, keepdims=True) + LN_EPS))
        for sl, mu, rstd in zip(sls, mus, rstds):
            out = (h_ref[sl, :] - mu) * rstd * g + b
            h_ref[sl, :] = out
            if hb_ref is not None:
                hb_ref[sl, :] = out.astype(BF16)
        return carry

    lax.fori_loop(0, h_ref.shape[0] // rows, body, 0)


def _ffn_kernel(x_hbm, wg_ref, wu_ref, wd_ref, g_ref, b_ref, h_ref, *rest,
                alpha, n_f, rows, col_chunk):
    hb_ref = rest[0] if len(rest) == 2 else None
    xb_ref = rest[-1]
    i = pl.program_id(0)
    f = pl.program_id(1)
    tm, d = h_ref.shape

    @pl.when(f == 0)
    def _():
        pltpu.sync_copy(x_hbm.at[pl.ds(pl.multiple_of(i * tm, tm), tm), :], h_ref)

        def body(r, carry):
            r0 = pl.multiple_of(r * rows, rows)
            xr = h_ref[pl.ds(r0, rows), :]
            xb_ref[pl.ds(r0, rows), :] = xr.astype(BF16)
            h_ref[pl.ds(r0, rows), :] = alpha * xr
            return carry
        lax.fori_loop(0, tm // rows, body, 0)

    xb = xb_ref[...]
    gate = jnp.dot(xb, wg_ref[...], preferred_element_type=F32)
    up = jnp.dot(xb, wu_ref[...], preferred_element_type=F32)
    act = (0.5 * (gate * jax.nn.sigmoid(gate)) * up).astype(BF16)
    wd = wd_ref[...].astype(BF16)
    for c in range(d // col_chunk):
        sl = slice(c * col_chunk, (c + 1) * col_chunk)
        h_ref[:, sl] += jnp.dot(act, wd[:, sl], preferred_element_type=F32)

    @pl.when(f == n_f - 1)
    def _():
        _ln_epilogue(h_ref, hb_ref, g_ref, b_ref, rows)


def _ffn_ln(x, wg, wu, wd, g, b, alpha, emit_bf16):
    m, d = x.shape
    f = wg.shape[1]
    tm = _tile(m, 1024)
    tf = _tile(f, 256)
    n_f = f // tf
    kern = functools.partial(_ffn_kernel, alpha=alpha, n_f=n_f, rows=_tile(tm, LN_ROWS_PER_STEP),
                             col_chunk=_tile(d, 512))
    single = dict(pipeline_mode=pl.Buffered(1))
    out_specs = [pl.BlockSpec((tm, d), lambda i, j: (i, 0), **single)]
    out_shape = [jax.ShapeDtypeStruct((m, d), F32)]
    if emit_bf16:
        out_specs.append(pl.BlockSpec((tm, d), lambda i, j: (i, 0), **single))
        out_shape.append(jax.ShapeDtypeStruct((m, d), BF16))
    outs = pl.pallas_call(
        kern,
        grid=(m // tm, n_f),
        in_specs=[
            pl.BlockSpec(memory_space=pl.ANY),
            pl.BlockSpec((d, tf), lambda i, j: (0, j)),
            pl.BlockSpec((d, tf), lambda i, j: (0, j)),
            pl.BlockSpec((tf, d), lambda i, j: (j, 0)),
            pl.BlockSpec((1, d), lambda i, j: (0, 0)),
            pl.BlockSpec((1, d), lambda i, j: (0, 0)),
        ],
        out_specs=out_specs,
        out_shape=out_shape,
        scratch_shapes=[pltpu.VMEM((tm, d), BF16)],
        compiler_params=_params(("parallel", "arbitrary")),
        name="ffn_ln",
    )(x, wg, wu, wd, g, b)
    return (outs[0], outs[1]) if emit_bf16 else (outs[0], None)


def _proj_kernel(x_ref, *refs, starts):
    w_refs, o_ref = refs[:-1], refs[-1]
    j = pl.program_id(1)
    for k, w_ref in enumerate(w_refs):
        @pl.when((j >= starts[k]) & (j < starts[k + 1]))
        def _(w_ref=w_ref):
            o_ref[...] = jnp.dot(x_ref[...], w_ref[...],
                                 preferred_element_type=F32).astype(o_ref.dtype)


def _proj(xb, ws):
    m, d = xb.shape
    tm = _tile(m, 1024)
    tn = _tile(math.gcd(*[w.shape[1] for w in ws]), 512)
    starts = [0]
    for w in ws:
        starts.append(starts[-1] + w.shape[1] // tn)

    def w_spec(k):
        lo, count = starts[k], starts[k + 1] - starts[k]
        return pl.BlockSpec((d, tn), lambda i, j: (0, jnp.clip(j - lo, 0, count - 1)))

    return pl.pallas_call(
        functools.partial(_proj_kernel, starts=tuple(starts)),
        grid=(m // tm, starts[-1]),
        in_specs=[pl.BlockSpec((tm, d), lambda i, j: (i, 0))] + [w_spec(k) for k in range(len(ws))],
        out_specs=pl.BlockSpec((tm, tn), lambda i, j: (i, j)),
        out_shape=jax.ShapeDtypeStruct((m, starts[-1] * tn), BF16),
        compiler_params=_params(("parallel", "arbitrary")),
        name="proj_main",
    )(xb, *ws)


def _split3(v):
    hi = v.astype(BF16)
    r1 = v - hi.astype(F32)
    mid = r1.astype(BF16)
    lo = (r1 - mid.astype(F32)).astype(BF16)
    return hi, mid, lo


def _proj_small_kernel(x_ref, ws_ref, wft_ref, bf_ref, kvg_ref, kg_ref, kb_ref,
                       ckv_ref, ki_ref, wi_ref, cum_ref, carry_ref, *, c_lat, n_ih, tiles_per_seq):
    i = pl.program_id(0)
    x = x_ref[...]
    tm = x.shape[0]
    y = jnp.dot(x, ws_ref[...], preferred_element_type=F32)
    c = y[:, :c_lat]
    c = c * lax.rsqrt(jnp.mean(c * c, axis=-1, keepdims=True) + RMS_EPS) * kvg_ref[...]
    ckv_ref[...] = c.astype(BF16)
    k = y[:, c_lat:c_lat + IDX_DIM]
    ki_ref[...] = _layer_norm_rows(k, kg_ref[...], kb_ref[...]).astype(BF16)
    wi_ref[...] = y[:, c_lat + IDX_DIM:c_lat + IDX_DIM + n_ih]

    ft = lax.dot_general(wft_ref[...], x, (((1,), (1,)), ((), ())), preferred_element_type=F32)
    z = ft + bf_ref[...]
    log_f = jnp.minimum(z, 0.0) - jnp.log1p(jnp.exp(-jnp.abs(z)))

    @pl.when(i % tiles_per_seq == 0)
    def _():
        carry_ref[...] = jnp.zeros_like(carry_ref)

    row = lax.broadcasted_iota(jnp.int32, (tm, tm), 0)
    col = lax.broadcasted_iota(jnp.int32, (tm, tm), 1)
    tri = jnp.where(row <= col, 1.0, 0.0).astype(BF16)
    hi, mid, lo = _split3(log_f)
    cs = (jnp.dot(hi, tri, preferred_element_type=F32)
          + jnp.dot(mid, tri, preferred_element_type=F32)
          + jnp.dot(lo, tri, preferred_element_type=F32))
    cum = cs + carry_ref[:, 0:1]
    cum_ref[...] = cum
    carry_ref[...] = jnp.broadcast_to(cum[:, tm - 1:tm], carry_ref.shape)


def _proj_small(xb, ws, wft, bf, kvg, kg, kb, *, c_lat, n_ih, seq):
    m, d = xb.shape
    ns = ws.shape[1]
    hf = wft.shape[0]
    tm = _tile(seq, 512)
    kern = functools.partial(_proj_small_kernel, c_lat=c_lat, n_ih=n_ih, tiles_per_seq=seq // tm)
    return pl.pallas_call(
        kern,
        grid=(m // tm,),
        in_specs=[
            pl.BlockSpec((tm, d), lambda i: (i, 0)),
            pl.BlockSpec((d, ns), lambda i: (0, 0)),
            pl.BlockSpec((hf, d), lambda i: (0, 0)),
            pl.BlockSpec((hf, 1), lambda i: (0, 0)),
            pl.BlockSpec((1, c_lat), lambda i: (0, 0)),
            pl.BlockSpec((1, IDX_DIM), lambda i: (0, 0)),
            pl.BlockSpec((1, IDX_DIM), lambda i: (0, 0)),
        ],
        out_specs=[
            pl.BlockSpec((tm, c_lat), lambda i: (i, 0)),
            pl.BlockSpec((tm, IDX_DIM), lambda i: (i, 0)),
            pl.BlockSpec((tm, n_ih), lambda i: (i, 0)),
            pl.BlockSpec((hf, tm), lambda i: (0, i)),
        ],
        out_shape=[
            jax.ShapeDtypeStruct((m, c_lat), BF16),
            jax.ShapeDtypeStruct((m, IDX_DIM), BF16),
            jax.ShapeDtypeStruct((m, n_ih), F32),
            jax.ShapeDtypeStruct((hf, m), F32),
        ],
        scratch_shapes=[pltpu.VMEM((hf, LANES), F32)],
        compiler_params=_params(("arbitrary",)),
        name="proj_small",
    )(xb, ws, wft, bf, kvg, kg, kb)


def _dsa_kernel(qi_ref, qa_ref, wi_ref, ki_ref, ckv_ref, wuk_ref, wuv_ref, tab_ref, bfar_ref,
                o_ref, score_ref, score_t_ref, qlat_ref, acc_ref, m_ref, l_ref, p_ref, alpha_ref, *,
                n_ih, n_h, k_sel, tk, hpg):
    i = pl.program_id(1)
    start = i * Q_BLOCK
    qb = Q_BLOCK
    bpc = tk // LANES
    scale = HEAD_DIM ** -0.5
    w_fold = (IDX_DIM ** -0.5) * (n_ih ** -0.5)

    wcol = wi_ref[...] * w_fold
    row = lax.broadcasted_iota(jnp.int32, (qb, tk), 0)
    col = lax.broadcasted_iota(jnp.int32, (qb, tk), 1)
    limit = start + jnp.where(row < CHUNK, CHUNK, 2 * CHUNK)
    n_chunks = (i + bpc) // bpc

    def score_chunk(c, carry):
        k0 = pl.multiple_of(c * tk, tk)
        kblk = ki_ref[pl.ds(k0, tk), :]
        sc = jnp.zeros((qb, tk), F32)
        for h in range(n_ih):
            s = lax.dot_general(qi_ref[:, h * IDX_DIM:(h + 1) * IDX_DIM], kblk,
                                (((1,), (1,)), ((), ())), preferred_element_type=F32)
            sc = sc + jnp.maximum(s, 0.0) * wcol[:, h:h + 1]
        sc = jnp.where(col + k0 < limit, sc, -jnp.inf)
        for j in range(bpc):
            blk = sc[:, j * LANES:(j + 1) * LANES]
            score_ref[c * bpc + j] = blk
            score_t_ref[c * bpc + j] = blk.T
        return carry

    lax.fori_loop(0, n_chunks, score_chunk, 0)

    def code_to_float(code):
        bits = jnp.where(code >= 0, code, code ^ 0x7FFFFFFF)
        return lax.bitcast_convert_type(bits, F32)

    def bit_body(bi, code):
        cand = code + lax.shift_left(jnp.int32(1), 31 - bi)
        cf = jnp.concatenate([code_to_float(cand)] * (LANES // 8), axis=0)

        def count_chunk(c, cnt):
            for j in range(bpc):
                hit = jnp.where(score_t_ref[c * bpc + j] >= cf, 1.0, 0.0)
                cnt = cnt + jnp.sum(hit.reshape(LANES // 8, 8, qb), axis=0)
            return cnt

        cnt = lax.fori_loop(0, n_chunks, count_chunk, jnp.zeros((8, qb), F32))
        total = jnp.sum(cnt, axis=0, keepdims=True)
        return jnp.where(total >= k_sel, cand, code)

    code = lax.fori_loop(0, 32, bit_body, jnp.full((8, qb), INT_MIN, jnp.int32))
    thr_row = jnp.where(code <= CODE_NEG_INF, F32_LOWEST, code_to_float(code))
    thr = jnp.concatenate([thr_row] * (qb // 8), axis=0).T

    for h in range(n_h):
        ql = jnp.dot(qa_ref[:, h * HEAD_DIM:(h + 1) * HEAD_DIM], wuk_ref[h],
                     preferred_element_type=F32)
        qlat_ref[h * qb:(h + 1) * qb, :] = ql.astype(BF16)

    def flash_step(k0, width, sel, bias_of_head, first):
        cblk = ckv_ref[pl.ds(k0, width), :]
        s_of = {}
        for g in range(n_h // hpg):
            sg = lax.dot_general(qlat_ref[g * hpg * qb:(g + 1) * hpg * qb, :], cblk,
                                 (((1,), (1,)), ((), ())), preferred_element_type=F32)
            for hh in range(hpg):
                s_of[g * hpg + hh] = sg[hh * qb:(hh + 1) * qb, :]
        for g in range(n_h // hpg):
            rg = slice(g * hpg * qb, (g + 1) * hpg * qb)
            heads = [g * hpg + hh for hh in range(hpg)]
            sh, m_new = {}, {}
            for h in heads:
                rs = slice(h * qb, (h + 1) * qb)
                sh[h] = jnp.where(sel, s_of[h] * (scale * LOG2E) + bias_of_head(h), NEG)
                row_max = jnp.broadcast_to(jnp.max(sh[h], axis=-1, keepdims=True), (qb, LANES))
                m_new[h] = row_max if first else jnp.maximum(m_ref[rs, :], row_max)
            for h in heads:
                rs = slice(h * qb, (h + 1) * qb)
                p = jnp.exp2(sh[h] - _widen(m_new[h], width))
                row_sum = jnp.broadcast_to(jnp.sum(p, axis=-1, keepdims=True), (qb, LANES))
                if first:
                    l_ref[rs, :] = row_sum
                else:
                    a = jnp.exp2(m_ref[rs, :] - m_new[h])
                    l_ref[rs, :] = a * l_ref[rs, :] + row_sum
                    alpha_ref[rs, :] = a
                m_ref[rs, :] = m_new[h]
                p_ref[rs, 0:width] = p.astype(BF16)
            pv = jnp.dot(p_ref[rg, 0:width], cblk, preferred_element_type=F32)
            if first:
                acc_ref[rg, :] = pv
            else:
                acc_ref[rg, :] = acc_ref[rg, :] * _widen(alpha_ref[rg, :], pv.shape[1]) + pv

    wb = jnp.maximum(i - 1, 0)
    tsel = jnp.where(i == 0, 1, 0)
    scw = jnp.concatenate([score_ref[wb], score_ref[wb + 1]], axis=-1)
    selw = scw >= _widen(thr, 2 * LANES)
    flash_step(pl.multiple_of(wb * LANES, LANES), 2 * LANES, selw, lambda h: tab_ref[tsel, h],
               first=True)

    far_end = (i - 1) * qb

    def far_chunk(c, carry):
        k0 = pl.multiple_of(c * tk, tk)
        sc = jnp.concatenate([score_ref[c * bpc + j] for j in range(bpc)], axis=-1)
        sel = (sc >= _widen(thr, tk)) & (col + k0 < far_end)
        flash_step(k0, tk, sel, lambda h: bfar_ref[h], first=False)
        return carry

    n_far = jnp.maximum((i - 1 + bpc - 1) // bpc, 0)
    lax.fori_loop(0, n_far, far_chunk, 0)

    for h in range(n_h):
        rs = slice(h * qb, (h + 1) * qb)
        o_lat = (acc_ref[rs, :] / _widen(l_ref[rs, :], acc_ref.shape[1])).astype(BF16)
        o_ref[:, h * HEAD_DIM:(h + 1) * HEAD_DIM] = jnp.dot(
            o_lat, wuv_ref[h], preferred_element_type=F32).astype(o_ref.dtype)


def _dsa(qkv, wi, ki, ckv, wuk, wuv, tab, bfar, *, batch, seq, n_ih, n_h, qi_blk, qa_blk):
    m = qkv.shape[0]
    c_lat = ckv.shape[1]
    nq = seq // Q_BLOCK
    k_sel = min(TOPK_MAX, seq // 4)
    tk = _tile(seq, 512)
    rows = n_h * Q_BLOCK
    kern = functools.partial(_dsa_kernel, n_ih=n_ih, n_h=n_h, k_sel=k_sel, tk=tk,
                             hpg=math.gcd(n_h, DSA_HEADS_PER_GROUP))
    const = dict(pipeline_mode=pl.Buffered(1))
    return pl.pallas_call(
        kern,
        grid=(batch, nq),
        in_specs=[
            pl.BlockSpec((Q_BLOCK, n_ih * IDX_DIM), lambda b, i: (b * nq + i, qi_blk)),
            pl.BlockSpec((Q_BLOCK, n_h * HEAD_DIM), lambda b, i: (b * nq + i, qa_blk)),
            pl.BlockSpec((Q_BLOCK, n_ih), lambda b, i: (b * nq + i, 0)),
            pl.BlockSpec((seq, IDX_DIM), lambda b, i: (b, 0)),
            pl.BlockSpec((seq, c_lat), lambda b, i: (b, 0)),
            pl.BlockSpec((n_h, HEAD_DIM, c_lat), lambda b, i: (0, 0, 0), **const),
            pl.BlockSpec((n_h, c_lat, HEAD_DIM), lambda b, i: (0, 0, 0), **const),
            pl.BlockSpec((2, n_h, Q_BLOCK, 2 * LANES), lambda b, i: (0, 0, 0, 0), **const),
            pl.BlockSpec(memory_space=pltpu.SMEM),
        ],
        out_specs=pl.BlockSpec((Q_BLOCK, n_h * HEAD_DIM), lambda b, i: (b * nq + i, 0)),
        out_shape=jax.ShapeDtypeStruct((m, n_h * HEAD_DIM), BF16),
        scratch_shapes=[
            pltpu.VMEM((seq // LANES, Q_BLOCK, LANES), F32),
            pltpu.VMEM((seq // LANES, LANES, Q_BLOCK), F32),
            pltpu.VMEM((rows, c_lat), BF16),
            pltpu.VMEM((rows, c_lat), F32),
            pltpu.VMEM((rows, LANES), F32),
            pltpu.VMEM((rows, LANES), F32),
            pltpu.VMEM((rows, tk), BF16),
            pltpu.VMEM((rows, LANES), F32),
        ],
        compiler_params=_params(("parallel", "arbitrary")),
        name="dsa_mixer",
    )(qkv, qkv, wi, ki, ckv, wuk, wuv, tab, bfar)


def _fox_kernel(q_ref, k_ref, v_ref, cq_ref, ck_ref, o_ref, acc_ref, m_ref, l_ref, *, tq, hpb):
    hp = pl.program_id(1)
    i = pl.program_id(2)
    c1 = (HEAD_DIM ** -0.5) * LOG2E
    lane = lax.broadcasted_iota(jnp.int32, cq_ref.shape, 1)
    cq_all = cq_ref[...] * LOG2E
    cq = [jnp.sum(jnp.where(lane == hp * hpb + hh, cq_all, 0.0), axis=-1, keepdims=True)
          for hh in range(hpb)]

    def step(kb, first):
        causal = first
        k0 = pl.multiple_of(kb * tq, tq)
        logits = []
        for hh in range(hpb):
            ls = slice(hh * HEAD_DIM, (hh + 1) * HEAD_DIM)
            logits.append(lax.dot_general(q_ref[:, ls], k_ref[pl.ds(k0, tq), ls],
                                          (((1,), (1,)), ((), ())), preferred_element_type=F32))
        sh, m_new = [], []
        for hh in range(hpb):
            ck = ck_ref[pl.ds(hp * hpb + hh, 1), pl.ds(kb, 1), :].reshape(1, tq) * LOG2E
            s = logits[hh] * c1 + (cq[hh] - ck)
            if causal:
                row = lax.broadcasted_iota(jnp.int32, (tq, tq), 0)
                col = lax.broadcasted_iota(jnp.int32, (tq, tq), 1)
                s = jnp.where(col <= row, s, NEG)
            sh.append(s)
            row_max = jnp.broadcast_to(jnp.max(s, axis=-1, keepdims=True), (tq, LANES))
            m_new.append(row_max if first else jnp.maximum(m_ref[hh], row_max))
        for hh in range(hpb):
            ls = slice(hh * HEAD_DIM, (hh + 1) * HEAD_DIM)
            p = jnp.exp2(sh[hh] - _widen(m_new[hh], tq))
            row_sum = jnp.broadcast_to(jnp.sum(p, axis=-1, keepdims=True), (tq, LANES))
            pv = jnp.dot(p.astype(BF16), v_ref[pl.ds(k0, tq), ls], preferred_element_type=F32)
            if first:
                l_ref[hh] = row_sum
                acc_ref[hh] = pv
            else:
                a = jnp.exp2(m_ref[hh] - m_new[hh])
                l_ref[hh] = a * l_ref[hh] + row_sum
                acc_ref[hh] = acc_ref[hh] * a + pv
            m_ref[hh] = m_new[hh]

    def full_block(kb, carry):
        step(kb, False)
        return carry

    step(i, True)
    lax.fori_loop(0, i, full_block, 0)
    for hh in range(hpb):
        o_ref[:, hh * HEAD_DIM:(hh + 1) * HEAD_DIM] = (acc_ref[hh] / l_ref[hh]).astype(o_ref.dtype)


def _fox(qkv, cum_tok, cum_blk, *, batch, seq, n_h, q_blk0, k_blk0, v_blk0):
    m = qkv.shape[0]
    tq = _tile(seq, 512)
    nq = seq // tq
    hpb = math.gcd(n_h, FOX_HEADS_PER_BLOCK)
    assert q_blk0 % hpb == 0 and k_blk0 % hpb == 0 and v_blk0 % hpb == 0
    wb = hpb * HEAD_DIM
    kern = functools.partial(_fox_kernel, tq=tq, hpb=hpb)
    return pl.pallas_call(
        kern,
        grid=(batch, n_h // hpb, nq),
        in_specs=[
            pl.BlockSpec((tq, wb), lambda b, h, i: (b * nq + i, q_blk0 // hpb + h)),
            pl.BlockSpec((seq, wb), lambda b, h, i: (b, k_blk0 // hpb + h)),
            pl.BlockSpec((seq, wb), lambda b, h, i: (b, v_blk0 // hpb + h)),
            pl.BlockSpec((tq, n_h), lambda b, h, i: (b * nq + i, 0)),
            pl.BlockSpec((n_h, nq, tq), lambda b, h, i: (0, b, 0)),
        ],
        out_specs=pl.BlockSpec((tq, wb), lambda b, h, i: (b * nq + i, h)),
        out_shape=jax.ShapeDtypeStruct((m, n_h * HEAD_DIM), BF16),
        scratch_shapes=[
            pltpu.VMEM((hpb, tq, HEAD_DIM), F32),
            pltpu.VMEM((hpb, tq, LANES), F32),
            pltpu.VMEM((hpb, tq, LANES), F32),
        ],
        compiler_params=_params(("parallel", "parallel", "arbitrary")),
        name="fox_mixer",
    )(qkv, qkv, qkv, cum_tok, cum_blk)


def _outproj_kernel(oa_ref, ob_ref, w_ref, res_hbm, g_ref, b_ref, h_ref, hb_ref, *,
                    alpha, n_ka, n_k, rows, col_chunk):
    i = pl.program_id(0)
    k = pl.program_id(1)
    tm, d = h_ref.shape

    @pl.when(k == 0)
    def _():
        pltpu.sync_copy(res_hbm.at[pl.ds(pl.multiple_of(i * tm, tm), tm), :], h_ref)

        def body(r, carry):
            r0 = pl.multiple_of(r * rows, rows)
            h_ref[pl.ds(r0, rows), :] = alpha * h_ref[pl.ds(r0, rows), :]
            return carry
        lax.fori_loop(0, tm // rows, body, 0)

    def accumulate(src_ref):
        src = src_ref[...]
        for c in range(d // col_chunk):
            sl = slice(c * col_chunk, (c + 1) * col_chunk)
            h_ref[:, sl] += jnp.dot(src, w_ref[:, sl], preferred_element_type=F32)

    @pl.when(k < n_ka)
    def _():
        accumulate(oa_ref)

    @pl.when(k >= n_ka)
    def _():
        accumulate(ob_ref)

    @pl.when(k == n_k - 1)
    def _():
        _ln_epilogue(h_ref, hb_ref, g_ref, b_ref, rows)


def _outproj_ln(oa, ob, w, res, g, b, alpha):
    m, wa = oa.shape
    wbw = ob.shape[1]
    d = w.shape[1]
    tm = _tile(m, 1024)
    tk = _tile(math.gcd(wa, wbw), 512)
    n_ka, n_kb = wa // tk, wbw // tk
    n_k = n_ka + n_kb
    kern = functools.partial(_outproj_kernel, alpha=alpha, n_ka=n_ka, n_k=n_k,
                             rows=_tile(tm, LN_ROWS_PER_STEP), col_chunk=_tile(d, 512))
    single = dict(pipeline_mode=pl.Buffered(1))
    return pl.pallas_call(
        kern,
        grid=(m // tm, n_k),
        in_specs=[
            pl.BlockSpec((tm, tk), lambda i, k: (i, jnp.minimum(k, n_ka - 1))),
            pl.BlockSpec((tm, tk), lambda i, k: (i, jnp.maximum(k - n_ka, 0))),
            pl.BlockSpec((tk, d), lambda i, k: (k, 0)),
            pl.BlockSpec(memory_space=pl.ANY),
            pl.BlockSpec((1, d), lambda i, k: (0, 0)),
            pl.BlockSpec((1, d), lambda i, k: (0, 0)),
        ],
        out_specs=[
            pl.BlockSpec((tm, d), lambda i, k: (i, 0), **single),
            pl.BlockSpec((tm, d), lambda i, k: (i, 0), **single),
        ],
        out_shape=[jax.ShapeDtypeStruct((m, d), F32), jax.ShapeDtypeStruct((m, d), BF16)],
        compiler_params=_params(("parallel", "arbitrary")),
        name="outproj_ln",
    )(oa, ob, w, res, g, b)


def _t5_bucket(rel):
    nb = N_BUCKETS // 2
    max_exact = nb // 2
    ret = jnp.where(rel > 0, nb, 0)
    n = jnp.abs(rel)
    nf = jnp.maximum(n, 1).astype(jnp.float32)
    large = max_exact + (jnp.log(nf / max_exact) / math.log(MAX_DISTANCE / max_exact)
                         * (nb - max_exact)).astype(jnp.int32)
    large = jnp.minimum(large, nb - 1)
    return ret + jnp.where(n < max_exact, n, large)


def _bias_tables(rel_bias):
    n_h = rel_bias.shape[1]
    width = 2 * LANES
    period = 2 * width
    m = jnp.arange(period, dtype=jnp.int32)
    m = jnp.where(m >= width, m - period, m)
    rel = jnp.stack([m - LANES, m])
    u = jnp.moveaxis(rel_bias[_t5_bucket(rel)], -1, 1).astype(F32)
    flat = jnp.tile(u, (1, 1, Q_BLOCK))[..., :Q_BLOCK * (period - 1)]
    tab = flat.reshape(2, n_h, Q_BLOCK, period - 1)[..., :width]
    far = rel_bias[_t5_bucket(jnp.full((), -(MAX_DISTANCE + 1), jnp.int32))].astype(F32)
    return tab, far


def _hybrid_mixer_ln(h, hb, w_in, b_f, kv_norm_g, idx_k_g, idx_k_b, w_uk, w_uv, rel_bias, w_out,
                     ln_g, ln_b, alpha, batch, seq):
    m, d = h.shape
    n_h = w_uk.shape[0]
    c_lat = w_uk.shape[2]
    n_hf = b_f.shape[0]
    w_dsa = n_h * HEAD_DIM
    w_fox = n_hf * HEAD_DIM
    d_in = w_in.shape[1]
    n_ih = (d_in - w_dsa - c_lat - IDX_DIM - 3 * w_fox - n_hf) // (IDX_DIM + 1)
    w_qi = n_ih * IDX_DIM
    assert w_dsa + c_lat + w_qi + IDX_DIM + n_ih + 3 * w_fox + n_hf == d_in
    assert w_qi % w_dsa == 0 and seq % (2 * LANES) == 0

    o0 = 0
    w_qa = w_in[:, o0:o0 + w_dsa]; o0 += w_dsa
    w_ckv = w_in[:, o0:o0 + c_lat]; o0 += c_lat
    w_qidx = w_in[:, o0:o0 + w_qi]; o0 += w_qi
    w_kidx = w_in[:, o0:o0 + IDX_DIM]; o0 += IDX_DIM
    w_widx = w_in[:, o0:o0 + n_ih]; o0 += n_ih
    w_qkv_fox = w_in[:, o0:o0 + 3 * w_fox]; o0 += 3 * w_fox
    w_f = w_in[:, o0:o0 + n_hf]

    qkv = _proj(hb, [w_qidx.astype(BF16), w_qa.astype(BF16), w_qkv_fox.astype(BF16)])

    n_small = c_lat + IDX_DIM + n_ih
    pad = (-n_small) % LANES
    w_small = jnp.pad(jnp.concatenate([w_ckv, w_kidx, w_widx], axis=1), ((0, 0), (0, pad))).astype(BF16)
    ckv, ki, wi, cum_t = _proj_small(
        hb, w_small, w_f.T.astype(BF16), b_f.reshape(n_hf, 1).astype(F32),
        kv_norm_g.reshape(1, c_lat), idx_k_g.reshape(1, IDX_DIM), idx_k_b.reshape(1, IDX_DIM),
        c_lat=c_lat, n_ih=n_ih, seq=seq)

    tab, far = _bias_tables(rel_bias)
    tab, far = tab * LOG2E, far * LOG2E
    o_a = _dsa(qkv, wi, ki, ckv, w_uk.astype(BF16), w_uv.astype(BF16), tab, far,
               batch=batch, seq=seq, n_ih=n_ih, n_h=n_h, qi_blk=0, qa_blk=w_qi // w_dsa)

    tq = _tile(seq, 512)
    blk0 = (w_qi + w_dsa) // HEAD_DIM
    o_b = _fox(qkv, cum_t.T, cum_t.reshape(n_hf, m // tq, tq), batch=batch, seq=seq, n_h=n_hf,
               q_blk0=blk0, k_blk0=blk0 + n_hf, v_blk0=blk0 + 2 * n_hf)

    return _outproj_ln(o_a, o_b, w_out.astype(BF16), h, ln_g.reshape(1, d), ln_b.reshape(1, d), alpha)


def kernel(x, ffn1_w_gate, ffn1_w_up, ffn1_w_down, ln1_g, ln1_b, w_in, b_f, kv_norm_g, idx_k_g,
           idx_k_b, w_uk, w_uv, rel_bias, w_out, ln2_g, ln2_b, ffn2_w_gate, ffn2_w_up,
           ffn2_w_down, ln3_g, ln3_b):
    batch, seq, d = x.shape
    depth = ffn1_w_gate.shape[0]
    alpha = (2.0 * depth) ** 0.25
    h = x.reshape(batch * seq, d)
    for l in range(depth):
        h, hb = _ffn_ln(h, ffn1_w_gate[l].astype(BF16), ffn1_w_up[l].astype(BF16),
                        ffn1_w_down[l], ln1_g[l].reshape(1, d), ln1_b[l].reshape(1, d),
                        alpha, emit_bf16=True)
        h, hb = _hybrid_mixer_ln(h, hb, w_in[l], b_f[l], kv_norm_g[l], idx_k_g[l], idx_k_b[l],
                                 w_uk[l], w_uv[l], rel_bias, w_out[l], ln2_g[l], ln2_b[l], alpha,
                                 batch, seq)
        h, _ = _ffn_ln(h, ffn2_w_gate[l].astype(BF16), ffn2_w_up[l].astype(BF16),
                       ffn2_w_down[l], ln3_g[l].reshape(1, d), ln3_b[l].reshape(1, d),
                       alpha, emit_bf16=False)
    return h.reshape(batch, seq, d)
```

```python
import functools
import math

import jax
import jax.numpy as jnp
import numpy as np
from jax import lax
from jax.experimental import pallas as pl
from jax.experimental.pallas import tpu as pltpu

CHUNK = 64
Q_BLOCK = 128
HEAD_DIM = 128
IDX_DIM = 128
TOPK_MAX = 256
N_BUCKETS = 32
MAX_DISTANCE = 128
LN_EPS = 1e-5
RMS_EPS = 1e-6
NEG = -1e30
LOG2E = math.log2(math.e)
DSA_HEADS_PER_GROUP = 4
FOX_HEADS_PER_BLOCK = 4
LN_ROWS_PER_STEP = 64
LN_ROW_GROUPS = 4
TILE_LOAD_CHUNKS = 8

LANES = 128
VMEM_LIMIT_BYTES = 56 * 1024 * 1024

BF16 = jnp.bfloat16
F32 = jnp.float32

INT_MIN = -(2 ** 31)
CODE_NEG_INF = -(2 ** 31) + 0x7FFFFF
F32_LOWEST = float(np.finfo(np.float32).min)


def _tile(dim, pref):
    t = min(dim, pref)
    while dim % t:
        t //= 2
    return t


def _params(sem):
    return pltpu.CompilerParams(dimension_semantics=sem, vmem_limit_bytes=VMEM_LIMIT_BYTES)


def _widen(x, width):
    reps = width // x.shape[1]
    return x if reps == 1 else jnp.concatenate([x] * reps, axis=-1)


def _layer_norm_rows(y, g, b):
    mu = jnp.mean(y, axis=-1, keepdims=True)
    d = y - mu
    var = jnp.mean(d * d, axis=-1, keepdims=True)
    return d * lax.rsqrt(var + LN_EPS) * g + b


def _load_rows_scaled(src_hbm, row0, acc_ref, xb_ref, sems, alpha, rows):
    chunk = acc_ref.shape[0] // TILE_LOAD_CHUNKS

    def copy(c):
        return pltpu.make_async_copy(src_hbm.at[pl.ds(row0 + c * chunk, chunk), :],
                                     acc_ref.at[pl.ds(c * chunk, chunk), :], sems.at[c])

    for c in range(TILE_LOAD_CHUNKS):
        copy(c).start()
    for c in range(TILE_LOAD_CHUNKS):
        copy(c).wait()

        def body(r, carry, c=c):
            r0 = pl.multiple_of(c * chunk + r * rows, rows)
            xr = acc_ref[pl.ds(r0, rows), :]
            if xb_ref is not None:
                xb_ref[pl.ds(r0, rows), :] = xr.astype(BF16)
            acc_ref[pl.ds(r0, rows), :] = alpha * xr
            return carry

        lax.fori_loop(0, chunk // rows, body, 0)


def _ln_epilogue(h_ref, hb_ref, g_ref, b_ref, rows, h_out, hb_out, row0, sem_h, sem_hb):
    g = g_ref[...]
    b = b_ref[...]
    sub = rows // LN_ROW_GROUPS
    n_steps = h_ref.shape[0] // rows

    def out_copies(r0):
        return (pltpu.make_async_copy(h_ref.at[pl.ds(r0, rows), :],
                                      h_out.at[pl.ds(row0 + r0, rows), :], sem_h),
                pltpu.make_async_copy(hb_ref.at[pl.ds(r0, rows), :],
                                      hb_out.at[pl.ds(row0 + r0, rows), :], sem_hb))

    def body(r, carry):
        r0 = pl.multiple_of(r * rows, rows)
        sls = [pl.ds(r0 + k * sub, sub) for k in range(LN_ROW_GROUPS)]
        mus = [jnp.mean(h_ref[sl, :], axis=-1, keepdims=True) for sl in sls]
        rstds = []
        for sl, mu in zip(sls, mus):
            d = h_ref[sl, :] - mu
            rstds.append(lax.rsqrt(jnp.mean(d * d, axis=-1, keepdims=True) + LN_EPS))
        for sl, mu, rstd in zip(sls, mus, rstds):
            out = (h_ref[sl, :] - mu) * rstd * g + b
            h_ref[sl, :] = out
            hb_ref[sl, :] = out.astype(BF16)
        for cp in out_copies(r0):
            cp.start()
        return carry

    lax.fori_loop(0, n_steps, body, 0)

    def drain(r, carry):
        for cp in out_copies(pl.multiple_of(r * rows, rows)):
            cp.wait()
        return carry

    lax.fori_loop(0, n_steps, drain, 0)


def _ln_tile_scratch(tm, d):
    return [pltpu.VMEM((tm, d), F32), pltpu.VMEM((tm, d), BF16),
            pltpu.SemaphoreType.DMA((TILE_LOAD_CHUNKS,)),
            pltpu.SemaphoreType.DMA(()), pltpu.SemaphoreType.DMA(())]


def _ffn_kernel(x_hbm, wg_ref, wu_ref, wd_ref, g_ref, b_ref, h_out, hb_out,
                h_ref, hb_ref, sems_in, sem_h, sem_hb, xb_ref, *, alpha, n_f, rows, col_chunk):
    i = pl.program_id(0)
    f = pl.program_id(1)
    tm, d = h_ref.shape
    row0 = pl.multiple_of(i * tm, tm)

    @pl.when(f == 0)
    def _():
        _load_rows_scaled(x_hbm, row0, h_ref, xb_ref, sems_in, alpha, rows)

    xb = xb_ref[...]
    gate = jnp.dot(xb, wg_ref[...], preferred_element_type=F32)
    up = jnp.dot(xb, wu_ref[...], preferred_element_type=F32)
    act = (0.5 * (gate * jax.nn.sigmoid(gate)) * up).astype(BF16)
    wd = wd_ref[...].astype(BF16)
    for c in range(d // col_chunk):
        sl = slice(c * col_chunk, (c + 1) * col_chunk)
        h_ref[:, sl] += jnp.dot(act, wd[:, sl], preferred_element_type=F32)

    @pl.when(f == n_f - 1)
    def _():
        _ln_epilogue(h_ref, hb_ref, g_ref, b_ref, rows, h_out, hb_out, row0, sem_h, sem_hb)


def _ffn_ln(x, wg, wu, wd, g, b, alpha):
    m, d = x.shape
    f = wg.shape[1]
    tm = _tile(m, 1024)
    tf = _tile(f, 256)
    n_f = f // tf
    kern = functools.partial(_ffn_kernel, alpha=alpha, n_f=n_f, rows=_tile(tm, LN_ROWS_PER_STEP),
                             col_chunk=_tile(d, 512))
    hbm = pl.BlockSpec(memory_space=pl.ANY)
    return pl.pallas_call(
        kern,
        grid=(m // tm, n_f),
        in_specs=[
            hbm,
            pl.BlockSpec((d, tf), lambda i, j: (0, j)),
            pl.BlockSpec((d, tf), lambda i, j: (0, j)),
            pl.BlockSpec((tf, d), lambda i, j: (j, 0)),
            pl.BlockSpec((1, d), lambda i, j: (0, 0)),
            pl.BlockSpec((1, d), lambda i, j: (0, 0)),
        ],
        out_specs=[hbm, hbm],
        out_shape=[jax.ShapeDtypeStruct((m, d), F32), jax.ShapeDtypeStruct((m, d), BF16)],
        scratch_shapes=_ln_tile_scratch(tm, d) + [pltpu.VMEM((tm, d), BF16)],
        compiler_params=_params(("parallel", "arbitrary")),
        name="ffn_ln",
    )(x, wg, wu, wd, g, b)


def _proj_kernel(x_ref, wf_ref, wb_ref, o_ref, *, n_direct):
    j = pl.program_id(1)

    @pl.when(j < n_direct)
    def _():
        o_ref[...] = jnp.dot(x_ref[...], wf_ref[...].astype(BF16),
                             preferred_element_type=F32).astype(o_ref.dtype)

    @pl.when(j >= n_direct)
    def _():
        o_ref[...] = jnp.dot(x_ref[...], wb_ref[...],
                             preferred_element_type=F32).astype(o_ref.dtype)


def _proj(xb, w_f32, col_ranges, w_extra):
    m, d = xb.shape
    tm = _tile(m, 1024)
    (a0, b0), (a1, b1) = col_ranges
    tn = _tile(math.gcd(a0, b0, a1, b1, w_extra.shape[1]), 512)
    n0, n1, nx = (b0 - a0) // tn, (b1 - a1) // tn, w_extra.shape[1] // tn
    n_direct = n0 + n1

    def direct_block(i, j):
        return 0, jnp.where(j < n0, a0 // tn + j, a1 // tn + jnp.clip(j - n0, 0, n1 - 1))

    return pl.pallas_call(
        functools.partial(_proj_kernel, n_direct=n_direct),
        grid=(m // tm, n_direct + nx),
        in_specs=[
            pl.BlockSpec((tm, d), lambda i, j: (i, 0)),
            pl.BlockSpec((d, tn), direct_block),
            pl.BlockSpec((d, tn), lambda i, j: (0, jnp.clip(j - n_direct, 0, nx - 1))),
        ],
        out_specs=pl.BlockSpec((tm, tn), lambda i, j: (i, j)),
        out_shape=jax.ShapeDtypeStruct((m, (n_direct + nx) * tn), BF16),
        compiler_params=_params(("parallel", "arbitrary")),
        name="proj_main",
    )(xb, w_f32, w_extra)


def _split3(v):
    hi = v.astype(BF16)
    r1 = v - hi.astype(F32)
    mid = r1.astype(BF16)
    lo = (r1 - mid.astype(F32)).astype(BF16)
    return hi, mid, lo


def _proj_small_kernel(x_ref, ws_ref, wft_ref, bf_ref, kvg_ref, kg_ref, kb_ref,
                       ckv_ref, ki_ref, wi_ref, cum_ref, carry_ref, *, c_lat, n_ih, tiles_per_seq):
    i = pl.program_id(0)
    x = x_ref[...]
    tm = x.shape[0]
    y = jnp.dot(x, ws_ref[...], preferred_element_type=F32)
    c = y[:, :c_lat]
    c = c * lax.rsqrt(jnp.mean(c * c, axis=-1, keepdims=True) + RMS_EPS) * kvg_ref[...]
    ckv_ref[...] = c.astype(BF16)
    k = y[:, c_lat:c_lat + IDX_DIM]
    ki_ref[...] = _layer_norm_rows(k, kg_ref[...], kb_ref[...]).astype(BF16)
    wi_ref[...] = y[:, c_lat + IDX_DIM:c_lat + IDX_DIM + n_ih]

    ft = lax.dot_general(wft_ref[...], x, (((1,), (1,)), ((), ())), preferred_element_type=F32)
    z = ft + bf_ref[...]
    log_f = jnp.minimum(z, 0.0) - jnp.log1p(jnp.exp(-jnp.abs(z)))

    @pl.when(i % tiles_per_seq == 0)
    def _():
        carry_ref[...] = jnp.zeros_like(carry_ref)

    row = lax.broadcasted_iota(jnp.int32, (tm, tm), 0)
    col = lax.broadcasted_iota(jnp.int32, (tm, tm), 1)
    tri = jnp.where(row <= col, 1.0, 0.0).astype(BF16)
    hi, mid, lo = _split3(log_f)
    cs = (jnp.dot(hi, tri, preferred_element_type=F32)
          + jnp.dot(mid, tri, preferred_element_type=F32)
          + jnp.dot(lo, tri, preferred_element_type=F32))
    cum = cs + carry_ref[:, 0:1]
    cum_ref[...] = cum
    carry_ref[...] = jnp.broadcast_to(cum[:, tm - 1:tm], carry_ref.shape)


def _proj_small(xb, ws, wft, bf, kvg, kg, kb, *, c_lat, n_ih, seq):
    m, d = xb.shape
    ns = ws.shape[1]
    hf = wft.shape[0]
    tm = _tile(seq, 512)
    kern = functools.partial(_proj_small_kernel, c_lat=c_lat, n_ih=n_ih, tiles_per_seq=seq // tm)
    return pl.pallas_call(
        kern,
        grid=(m // tm,),
        in_specs=[
            pl.BlockSpec((tm, d), lambda i: (i, 0)),
            pl.BlockSpec((d, ns), lambda i: (0, 0)),
            pl.BlockSpec((hf, d), lambda i: (0, 0)),
            pl.BlockSpec((hf, 1), lambda i: (0, 0)),
            pl.BlockSpec((1, c_lat), lambda i: (0, 0)),
            pl.BlockSpec((1, IDX_DIM), lambda i: (0, 0)),
            pl.BlockSpec((1, IDX_DIM), lambda i: (0, 0)),
        ],
        out_specs=[
            pl.BlockSpec((tm, c_lat), lambda i: (i, 0)),
            pl.BlockSpec((tm, IDX_DIM), lambda i: (i, 0)),
            pl.BlockSpec((tm, n_ih), lambda i: (i, 0)),
            pl.BlockSpec((hf, tm), lambda i: (0, i)),
        ],
        out_shape=[
            jax.ShapeDtypeStruct((m, c_lat), BF16),
            jax.ShapeDtypeStruct((m, IDX_DIM), BF16),
            jax.ShapeDtypeStruct((m, n_ih), F32),
            jax.ShapeDtypeStruct((hf, m), F32),
        ],
        scratch_shapes=[pltpu.VMEM((hf, LANES), F32)],
        compiler_params=_params(("arbitrary",)),
        name="proj_small",
    )(xb, ws, wft, bf, kvg, kg, kb)


def _dsa_kernel(qi_ref, qa_ref, wi_ref, ki_ref, ckv_ref, wuk_ref, wuv_ref, tab_ref, bfar_ref,
                o_ref, score_ref, score_t_ref, qlat_ref, acc_ref, m_ref, l_ref, p_ref, alpha_ref, *,
                n_ih, n_h, k_sel, tk, hpg):
    i = pl.program_id(1)
    start = i * Q_BLOCK
    qb = Q_BLOCK
    bpc = tk // LANES
    scale = HEAD_DIM ** -0.5
    w_fold = (IDX_DIM ** -0.5) * (n_ih ** -0.5)

    wcol = wi_ref[...] * w_fold
    row = lax.broadcasted_iota(jnp.int32, (qb, tk), 0)
    col = lax.broadcasted_iota(jnp.int32, (qb, tk), 1)
    limit = start + jnp.where(row < CHUNK, CHUNK, 2 * CHUNK)
    n_chunks = (i + bpc) // bpc

    def score_chunk(c, carry):
        k0 = pl.multiple_of(c * tk, tk)
        kblk = ki_ref[pl.ds(k0, tk), :]
        sc = jnp.zeros((qb, tk), F32)
        for h in range(n_ih):
            s = lax.dot_general(qi_ref[:, h * IDX_DIM:(h + 1) * IDX_DIM], kblk,
                                (((1,), (1,)), ((), ())), preferred_element_type=F32)
            sc = sc + jnp.maximum(s, 0.0) * wcol[:, h:h + 1]
        sc = jnp.where(col + k0 < limit, sc, -jnp.inf)
        for j in range(bpc):
            blk = sc[:, j * LANES:(j + 1) * LANES]
            score_ref[c * bpc + j] = blk
            score_t_ref[c * bpc + j] = blk.T
        return carry

    lax.fori_loop(0, n_chunks, score_chunk, 0)

    def code_to_float(code):
        bits = jnp.where(code >= 0, code, code ^ 0x7FFFFFFF)
        return lax.bitcast_convert_type(bits, F32)

    def bit_body(bi, code):
        cand = code + lax.shift_left(jnp.int32(1), 31 - bi)
        cf = jnp.concatenate([code_to_float(cand)] * (LANES // 8), axis=0)

        def count_chunk(c, cnt):
            for j in range(bpc):
                hit = jnp.where(score_t_ref[c * bpc + j] >= cf, 1.0, 0.0)
                cnt = cnt + jnp.sum(hit.reshape(LANES // 8, 8, qb), axis=0)
            return cnt

        cnt = lax.fori_loop(0, n_chunks, count_chunk, jnp.zeros((8, qb), F32))
        total = jnp.sum(cnt, axis=0, keepdims=True)
        return jnp.where(total >= k_sel, cand, code)

    code = lax.fori_loop(0, 32, bit_body, jnp.full((8, qb), INT_MIN, jnp.int32))
    thr_row = jnp.where(code <= CODE_NEG_INF, F32_LOWEST, code_to_float(code))
    thr = jnp.concatenate([thr_row] * (qb // 8), axis=0).T

    for h in range(n_h):
        ql = jnp.dot(qa_ref[:, h * HEAD_DIM:(h + 1) * HEAD_DIM], wuk_ref[h],
                     preferred_element_type=F32)
        qlat_ref[h * qb:(h + 1) * qb, :] = ql.astype(BF16)

    def flash_step(k0, width, sel, bias_of_head, first):
        cblk = ckv_ref[pl.ds(k0, width), :]
        s_of = {}
        for g in range(n_h // hpg):
            sg = lax.dot_general(qlat_ref[g * hpg * qb:(g + 1) * hpg * qb, :], cblk,
                                 (((1,), (1,)), ((), ())), preferred_element_type=F32)
            for hh in range(hpg):
                s_of[g * hpg + hh] = sg[hh * qb:(hh + 1) * qb, :]
        for g in range(n_h // hpg):
            rg = slice(g * hpg * qb, (g + 1) * hpg * qb)
            heads = [g * hpg + hh for hh in range(hpg)]
            sh, m_new = {}, {}
            for h in heads:
                rs = slice(h * qb, (h + 1) * qb)
                sh[h] = jnp.where(sel, s_of[h] * (scale * LOG2E) + bias_of_head(h), NEG)
                row_max = jnp.broadcast_to(jnp.max(sh[h], axis=-1, keepdims=True), (qb, LANES))
                m_new[h] = row_max if first else jnp.maximum(m_ref[rs, :], row_max)
            for h in heads:
                rs = slice(h * qb, (h + 1) * qb)
                p = jnp.exp2(sh[h] - _widen(m_new[h], width))
                row_sum = jnp.broadcast_to(jnp.sum(p, axis=-1, keepdims=True), (qb, LANES))
                if first:
                    l_ref[rs, :] = row_sum
                else:
                    a = jnp.exp2(m_ref[rs, :] - m_new[h])
                    l_ref[rs, :] = a * l_ref[rs, :] + row_sum
                    alpha_ref[rs, :] = a
                m_ref[rs, :] = m_new[h]
                p_ref[rs, 0:width] = p.astype(BF16)
            pv = jnp.dot(p_ref[rg, 0:width], cblk, preferred_element_type=F32)
            if first:
                acc_ref[rg, :] = pv
            else:
                acc_ref[rg, :] = acc_ref[rg, :] * _widen(alpha_ref[rg, :], pv.shape[1]) + pv

    wb = jnp.maximum(i - 1, 0)
    tsel = jnp.where(i == 0, 1, 0)
    scw = jnp.concatenate([score_ref[wb], score_ref[wb + 1]], axis=-1)
    selw = scw >= _widen(thr, 2 * LANES)
    flash_step(pl.multiple_of(wb * LANES, LANES), 2 * LANES, selw, lambda h: tab_ref[tsel, h],
               first=True)

    far_end = (i - 1) * qb

    def far_chunk(c, carry):
        k0 = pl.multiple_of(c * tk, tk)
        sc = jnp.concatenate([score_ref[c * bpc + j] for j in range(bpc)], axis=-1)
        sel = (sc >= _widen(thr, tk)) & (col + k0 < far_end)
        flash_step(k0, tk, sel, lambda h: bfar_ref[h], first=False)
        return carry

    n_far = jnp.maximum((i - 1 + bpc - 1) // bpc, 0)
    lax.fori_loop(0, n_far, far_chunk, 0)

    for h in range(n_h):
        rs = slice(h * qb, (h + 1) * qb)
        o_lat = (acc_ref[rs, :] / _widen(l_ref[rs, :], acc_ref.shape[1])).astype(BF16)
        o_ref[:, h * HEAD_DIM:(h + 1) * HEAD_DIM] = jnp.dot(
            o_lat, wuv_ref[h], preferred_element_type=F32).astype(o_ref.dtype)


def _dsa(qkv, wi, ki, ckv, wuk, wuv, tab, bfar, *, batch, seq, n_ih, n_h, qi_blk, qa_blk):
    m = qkv.shape[0]
    c_lat = ckv.shape[1]
    nq = seq // Q_BLOCK
    k_sel = min(TOPK_MAX, seq // 4)
    tk = _tile(seq, 512)
    rows = n_h * Q_BLOCK
    kern = functools.partial(_dsa_kernel, n_ih=n_ih, n_h=n_h, k_sel=k_sel, tk=tk,
                             hpg=math.gcd(n_h, DSA_HEADS_PER_GROUP))
    const = dict(pipeline_mode=pl.Buffered(1))
    return pl.pallas_call(
        kern,
        grid=(batch, nq),
        in_specs=[
            pl.BlockSpec((Q_BLOCK, n_ih * IDX_DIM), lambda b, i: (b * nq + i, qi_blk)),
            pl.BlockSpec((Q_BLOCK, n_h * HEAD_DIM), lambda b, i: (b * nq + i, qa_blk)),
            pl.BlockSpec((Q_BLOCK, n_ih), lambda b, i: (b * nq + i, 0)),
            pl.BlockSpec((seq, IDX_DIM), lambda b, i: (b, 0)),
            pl.BlockSpec((seq, c_lat), lambda b, i: (b, 0)),
            pl.BlockSpec((n_h, HEAD_DIM, c_lat), lambda b, i: (0, 0, 0), **const),
            pl.BlockSpec((n_h, c_lat, HEAD_DIM), lambda b, i: (0, 0, 0), **const),
            pl.BlockSpec((2, n_h, Q_BLOCK, 2 * LANES), lambda b, i: (0, 0, 0, 0), **const),
            pl.BlockSpec(memory_space=pltpu.SMEM),
        ],
        out_specs=pl.BlockSpec((Q_BLOCK, n_h * HEAD_DIM), lambda b, i: (b * nq + i, 0)),
        out_shape=jax.ShapeDtypeStruct((m, n_h * HEAD_DIM), BF16),
        scratch_shapes=[
            pltpu.VMEM((seq // LANES, Q_BLOCK, LANES), F32),
            pltpu.VMEM((seq // LANES, LANES, Q_BLOCK), F32),
            pltpu.VMEM((rows, c_lat), BF16),
            pltpu.VMEM((rows, c_lat), F32),
            pltpu.VMEM((rows, LANES), F32),
            pltpu.VMEM((rows, LANES), F32),
            pltpu.VMEM((rows, tk), BF16),
            pltpu.VMEM((rows, LANES), F32),
        ],
        compiler_params=_params(("parallel", "arbitrary")),
        name="dsa_mixer",
    )(qkv, qkv, wi, ki, ckv, wuk, wuv, tab, bfar)


def _fox_kernel(q_ref, k_ref, v_ref, cq_ref, ck_ref, o_ref, acc_ref, m_ref, l_ref, *, tq, hpb):
    hp = pl.program_id(1)
    i = pl.program_id(2)
    c1 = (HEAD_DIM ** -0.5) * LOG2E
    lane = lax.broadcasted_iota(jnp.int32, cq_ref.shape, 1)
    cq_all = cq_ref[...] * LOG2E
    cq = [jnp.sum(jnp.where(lane == hp * hpb + hh, cq_all, 0.0), axis=-1, keepdims=True)
          for hh in range(hpb)]

    def step(kb, first):
        causal = first
        k0 = pl.multiple_of(kb * tq, tq)
        logits = []
        for hh in range(hpb):
            ls = slice(hh * HEAD_DIM, (hh + 1) * HEAD_DIM)
            logits.append(lax.dot_general(q_ref[:, ls], k_ref[pl.ds(k0, tq), ls],
                                          (((1,), (1,)), ((), ())), preferred_element_type=F32))
        sh, m_new = [], []
        for hh in range(hpb):
            ck = ck_ref[pl.ds(hp * hpb + hh, 1), pl.ds(kb, 1), :].reshape(1, tq) * LOG2E
            s = logits[hh] * c1 + (cq[hh] - ck)
            if causal:
                row = lax.broadcasted_iota(jnp.int32, (tq, tq), 0)
                col = lax.broadcasted_iota(jnp.int32, (tq, tq), 1)
                s = jnp.where(col <= row, s, NEG)
            sh.append(s)
            row_max = jnp.broadcast_to(jnp.max(s, axis=-1, keepdims=True), (tq, LANES))
            m_new.append(row_max if first else jnp.maximum(m_ref[hh], row_max))
        for hh in range(hpb):
            ls = slice(hh * HEAD_DIM, (hh + 1) * HEAD_DIM)
            p = jnp.exp2(sh[hh] - _widen(m_new[hh], tq))
            row_sum = jnp.broadcast_to(jnp.sum(p, axis=-1, keepdims=True), (tq, LANES))
            pv = jnp.dot(p.astype(BF16), v_ref[pl.ds(k0, tq), ls], preferred_element_type=F32)
            if first:
                l_ref[hh] = row_sum
                acc_ref[hh] = pv
            else:
                a = jnp.exp2(m_ref[hh] - m_new[hh])
                l_ref[hh] = a * l_ref[hh] + row_sum
                acc_ref[hh] = acc_ref[hh] * a + pv
            m_ref[hh] = m_new[hh]

    def full_block(kb, carry):
        step(kb, False)
        return carry

    step(i, True)
    lax.fori_loop(0, i, full_block, 0)
    for hh in range(hpb):
        o_ref[:, hh * HEAD_DIM:(hh + 1) * HEAD_DIM] = (acc_ref[hh] / l_ref[hh]).astype(o_ref.dtype)


def _fox(qkv, cum_tok, cum_blk, *, batch, seq, n_h, q_blk0, k_blk0, v_blk0):
    m = qkv.shape[0]
    tq = _tile(seq, 512)
    nq = seq // tq
    hpb = math.gcd(n_h, FOX_HEADS_PER_BLOCK)
    assert q_blk0 % hpb == 0 and k_blk0 % hpb == 0 and v_blk0 % hpb == 0
    wb = hpb * HEAD_DIM
    kern = functools.partial(_fox_kernel, tq=tq, hpb=hpb)
    return pl.pallas_call(
        kern,
        grid=(batch, n_h // hpb, nq),
        in_specs=[
            pl.BlockSpec((tq, wb), lambda b, h, i: (b * nq + i, q_blk0 // hpb + h)),
            pl.BlockSpec((seq, wb), lambda b, h, i: (b, k_blk0 // hpb + h)),
            pl.BlockSpec((seq, wb), lambda b, h, i: (b, v_blk0 // hpb + h)),
            pl.BlockSpec((tq, n_h), lambda b, h, i: (b * nq + i, 0)),
            pl.BlockSpec((n_h, nq, tq), lambda b, h, i: (0, b, 0)),
        ],
        out_specs=pl.BlockSpec((tq, wb), lambda b, h, i: (b * nq + i, h)),
        out_shape=jax.ShapeDtypeStruct((m, n_h * HEAD_DIM), BF16),
        scratch_shapes=[
            pltpu.VMEM((hpb, tq, HEAD_DIM), F32),
            pltpu.VMEM((hpb, tq, LANES), F32),
            pltpu.VMEM((hpb, tq, LANES), F32),
        ],
        compiler_params=_params(("parallel", "parallel", "arbitrary")),
        name="fox_mixer",
    )(qkv, qkv, qkv, cum_tok, cum_blk)


def _outproj_kernel(oa_ref, ob_ref, w_ref, res_hbm, g_ref, b_ref, h_out, hb_out,
                    h_ref, hb_ref, sems_in, sem_h, sem_hb, *, alpha, n_ka, n_k, rows, col_chunk):
    i = pl.program_id(0)
    k = pl.program_id(1)
    tm, d = h_ref.shape
    row0 = pl.multiple_of(i * tm, tm)

    @pl.when(k == 0)
    def _():
        _load_rows_scaled(res_hbm, row0, h_ref, None, sems_in, alpha, rows)

    def accumulate(src_ref):
        src = src_ref[...]
        for c in range(d // col_chunk):
            sl = slice(c * col_chunk, (c + 1) * col_chunk)
            h_ref[:, sl] += jnp.dot(src, w_ref[:, sl], preferred_element_type=F32)

    @pl.when(k < n_ka)
    def _():
        accumulate(oa_ref)

    @pl.when(k >= n_ka)
    def _():
        accumulate(ob_ref)

    @pl.when(k == n_k - 1)
    def _():
        _ln_epilogue(h_ref, hb_ref, g_ref, b_ref, rows, h_out, hb_out, row0, sem_h, sem_hb)


def _outproj_ln(oa, ob, w, res, g, b, alpha):
    m, wa = oa.shape
    wbw = ob.shape[1]
    d = w.shape[1]
    tm = _tile(m, 1024)
    tk = _tile(math.gcd(wa, wbw), 512)
    n_ka, n_kb = wa // tk, wbw // tk
    n_k = n_ka + n_kb
    kern = functools.partial(_outproj_kernel, alpha=alpha, n_ka=n_ka, n_k=n_k,
                             rows=_tile(tm, LN_ROWS_PER_STEP), col_chunk=_tile(d, 512))
    hbm = pl.BlockSpec(memory_space=pl.ANY)
    return pl.pallas_call(
        kern,
        grid=(m // tm, n_k),
        in_specs=[
            pl.BlockSpec((tm, tk), lambda i, k: (i, jnp.minimum(k, n_ka - 1))),
            pl.BlockSpec((tm, tk), lambda i, k: (i, jnp.maximum(k - n_ka, 0))),
            pl.BlockSpec((tk, d), lambda i, k: (k, 0)),
            hbm,
            pl.BlockSpec((1, d), lambda i, k: (0, 0)),
            pl.BlockSpec((1, d), lambda i, k: (0, 0)),
        ],
        out_specs=[hbm, hbm],
        out_shape=[jax.ShapeDtypeStruct((m, d), F32), jax.ShapeDtypeStruct((m, d), BF16)],
        scratch_shapes=_ln_tile_scratch(tm, d),
        compiler_params=_params(("parallel", "arbitrary")),
        name="outproj_ln",
    )(oa, ob, w, res, g, b)


def _t5_bucket(rel):
    nb = N_BUCKETS // 2
    max_exact = nb // 2
    ret = jnp.where(rel > 0, nb, 0)
    n = jnp.abs(rel)
    nf = jnp.maximum(n, 1).astype(jnp.float32)
    large = max_exact + (jnp.log(nf / max_exact) / math.log(MAX_DISTANCE / max_exact)
                         * (nb - max_exact)).astype(jnp.int32)
    large = jnp.minimum(large, nb - 1)
    return ret + jnp.where(n < max_exact, n, large)


def _bias_tables(rel_bias):
    n_h = rel_bias.shape[1]
    width = 2 * LANES
    period = 2 * width
    m = jnp.arange(period, dtype=jnp.int32)
    m = jnp.where(m >= width, m - period, m)
    rel = jnp.stack([m - LANES, m])
    u = jnp.moveaxis(rel_bias[_t5_bucket(rel)], -1, 1).astype(F32)
    flat = jnp.tile(u, (1, 1, Q_BLOCK))[..., :Q_BLOCK * (period - 1)]
    tab = flat.reshape(2, n_h, Q_BLOCK, period - 1)[..., :width]
    far = rel_bias[_t5_bucket(jnp.full((), -(MAX_DISTANCE + 1), jnp.int32))].astype(F32)
    return tab, far


def _hybrid_mixer_ln(h, hb, w_in, b_f, kv_norm_g, idx_k_g, idx_k_b, w_uk, w_uv, rel_bias, w_out,
                     ln_g, ln_b, alpha, batch, seq):
    m, d = h.shape
    n_h = w_uk.shape[0]
    c_lat = w_uk.shape[2]
    n_hf = b_f.shape[0]
    w_dsa = n_h * HEAD_DIM
    w_fox = n_hf * HEAD_DIM
    d_in = w_in.shape[1]
    n_ih = (d_in - w_dsa - c_lat - IDX_DIM - 3 * w_fox - n_hf) // (IDX_DIM + 1)
    w_qi = n_ih * IDX_DIM
    assert w_dsa + c_lat + w_qi + IDX_DIM + n_ih + 3 * w_fox + n_hf == d_in
    assert w_qi % w_dsa == 0 and seq % (2 * LANES) == 0

    o_ckv = w_dsa
    o_qi = o_ckv + c_lat
    o_ki = o_qi + w_qi
    o_wi = o_ki + IDX_DIM
    o_fox = o_wi + n_ih
    o_f = o_fox + 3 * w_fox
    w_ckv = w_in[:, o_ckv:o_qi]
    w_kidx = w_in[:, o_ki:o_wi]
    w_widx = w_in[:, o_wi:o_fox]
    w_f = w_in[:, o_f:o_f + n_hf]

    qkv = _proj(hb, w_in, ((o_qi, o_ki), (0, w_dsa)), w_in[:, o_fox:o_f].astype(BF16))

    n_small = c_lat + IDX_DIM + n_ih
    pad = (-n_small) % LANES
    w_small = jnp.pad(jnp.concatenate([w_ckv, w_kidx, w_widx], axis=1), ((0, 0), (0, pad))).astype(BF16)
    ckv, ki, wi, cum_t = _proj_small(
        hb, w_small, w_f.T.astype(BF16), b_f.reshape(n_hf, 1).astype(F32),
        kv_norm_g.reshape(1, c_lat), idx_k_g.reshape(1, IDX_DIM), idx_k_b.reshape(1, IDX_DIM),
        c_lat=c_lat, n_ih=n_ih, seq=seq)

    tab, far = _bias_tables(rel_bias)
    tab, far = tab * LOG2E, far * LOG2E
    o_a = _dsa(qkv, wi, ki, ckv, w_uk.astype(BF16), w_uv.astype(BF16), tab, far,
               batch=batch, seq=seq, n_ih=n_ih, n_h=n_h, qi_blk=0, qa_blk=w_qi // w_dsa)

    tq = _tile(seq, 512)
    blk0 = (w_qi + w_dsa) // HEAD_DIM
    o_b = _fox(qkv, cum_t.T, cum_t.reshape(n_hf, m // tq, tq), batch=batch, seq=seq, n_h=n_hf,
               q_blk0=blk0, k_blk0=blk0 + n_hf, v_blk0=blk0 + 2 * n_hf)

    return _outproj_ln(o_a, o_b, w_out.astype(BF16), h, ln_g.reshape(1, d), ln_b.reshape(1, d), alpha)


def kernel(x, ffn1_w_gate, ffn1_w_up, ffn1_w_down, ln1_g, ln1_b, w_in, b_f, kv_norm_g, idx_k_g,
           idx_k_b, w_uk, w_uv, rel_bias, w_out, ln2_g, ln2_b, ffn2_w_gate, ffn2_w_up,
           ffn2_w_down, ln3_g, ln3_b):
    batch, seq, d = x.shape
    depth = ffn1_w_gate.shape[0]
    alpha = (2.0 * depth) ** 0.25
    h = x.reshape(batch * seq, d)
    for l in range(depth):
        h, hb = _ffn_ln(h, ffn1_w_gate[l].astype(BF16), ffn1_w_up[l].astype(BF16),
                        ffn1_w_down[l], ln1_g[l].reshape(1, d), ln1_b[l].reshape(1, d),
                        alpha)
        h, hb = _hybrid_mixer_ln(h, hb, w_in[l], b_f[l], kv_norm_g[l], idx_k_g[l], idx_k_b[l],
                                 w_uk[l], w_uv[l], rel_bias, w_out[l], ln2_g[l], ln2_b[l], alpha,
                                 batch, seq)
        h, _ = _ffn_ln(h, ffn2_w_gate[l].astype(BF16), ffn2_w_up[l].astype(BF16),
                       ffn2_w_down[l], ln3_g[l].reshape(1, d), ln3_b[l].reshape(1, d),
                       alpha)
    return h.reshape(batch, seq, d)
```

```python
import functools
import math

import jax
import jax.numpy as jnp
import numpy as np
from jax import lax
from jax.experimental import pallas as pl
from jax.experimental.pallas import tpu as pltpu

CHUNK = 64
Q_BLOCK = 128
HEAD_DIM = 128
IDX_DIM = 128
TOPK_MAX = 256
N_BUCKETS = 32
MAX_DISTANCE = 128
LN_EPS = 1e-5
RMS_EPS = 1e-6
NEG = -1e30
LOG2E = math.log2(math.e)
DSA_HEADS_PER_GROUP = 4
FOX_HEADS_PER_BLOCK = 4
LN_ROWS_PER_STEP = 64
LN_ROW_GROUPS = 4
TILE_LOAD_CHUNKS = 8

LANES = 128
VMEM_LIMIT_BYTES = 56 * 1024 * 1024

BF16 = jnp.bfloat16
F32 = jnp.float32

INT_MIN = -(2 ** 31)
CODE_NEG_INF = -(2 ** 31) + 0x7FFFFF
F32_LOWEST = float(np.finfo(np.float32).min)


def _tile(dim, pref):
    t = min(dim, pref)
    while dim % t:
        t //= 2
    return t


def _params(sem):
    return pltpu.CompilerParams(dimension_semantics=sem, vmem_limit_bytes=VMEM_LIMIT_BYTES)


def _widen(x, width):
    reps = width // x.shape[1]
    return x if reps == 1 else jnp.concatenate([x] * reps, axis=-1)


def _layer_norm_rows(y, g, b):
    mu = jnp.mean(y, axis=-1, keepdims=True)
    d = y - mu
    var = jnp.mean(d * d, axis=-1, keepdims=True)
    return d * lax.rsqrt(var + LN_EPS) * g + b


def _load_rows_scaled(src_hbm, row0, acc_ref, xb_ref, sems, alpha, rows):
    chunk = acc_ref.shape[0] // TILE_LOAD_CHUNKS

    def copy(c):
        return pltpu.make_async_copy(src_hbm.at[pl.ds(row0 + c * chunk, chunk), :],
                                     acc_ref.at[pl.ds(c * chunk, chunk), :], sems.at[c])

    for c in range(TILE_LOAD_CHUNKS):
        copy(c).start()
    for c in range(TILE_LOAD_CHUNKS):
        copy(c).wait()

        def body(r, carry, c=c):
            r0 = pl.multiple_of(c * chunk + r * rows, rows)
            xr = acc_ref[pl.ds(r0, rows), :]
            if xb_ref is not None:
                xb_ref[pl.ds(r0, rows), :] = xr.astype(BF16)
            acc_ref[pl.ds(r0, rows), :] = alpha * xr
            return carry

        lax.fori_loop(0, chunk // rows, body, 0)


def _ln_epilogue(h_ref, hb_ref, g_ref, b_ref, rows, h_out, hb_out, row0, sem_h, sem_hb):
    g = g_ref[...]
    b = b_ref[...]
    sub = rows // LN_ROW_GROUPS
    n_steps = h_ref.shape[0] // rows

    def out_copies(r0):
        return (pltpu.make_async_copy(h_ref.at[pl.ds(r0, rows), :],
                                      h_out.at[pl.ds(row0 + r0, rows), :], sem_h),
                pltpu.make_async_copy(hb_ref.at[pl.ds(r0, rows), :],
                                      hb_out.at[pl.ds(row0 + r0, rows), :], sem_hb))

    def body(r, carry):
        r0 = pl.multiple_of(r * rows, rows)
        sls = [pl.ds(r0 + k * sub, sub) for k in range(LN_ROW_GROUPS)]
        mus = [jnp.mean(h_ref[sl, :], axis=-1, keepdims=True) for sl in sls]
        rstds = []
        for sl, mu in zip(sls, mus):
            d = h_ref[sl, :] - mu
            rstds.append(lax.rsqrt(jnp.mean(d * d, axis=-1, keepdims=True) + LN_EPS))
        for sl, mu, rstd in zip(sls, mus, rstds):
            out = (h_ref[sl, :] - mu) * rstd * g + b
            h_ref[sl, :] = out
            hb_ref[sl, :] = out.astype(BF16)
        for cp in out_copies(r0):
            cp.start()
        return carry

    lax.fori_loop(0, n_steps, body, 0)

    def drain(r, carry):
        for cp in out_copies(pl.multiple_of(r * rows, rows)):
            cp.wait()
        return carry

    lax.fori_loop(0, n_steps, drain, 0)


def _ln_tile_scratch(tm, d):
    return [pltpu.VMEM((tm, d), F32), pltpu.VMEM((tm, d), BF16),
            pltpu.SemaphoreType.DMA((TILE_LOAD_CHUNKS,)),
            pltpu.SemaphoreType.DMA(()), pltpu.SemaphoreType.DMA(())]


def _ffn_kernel(x_hbm, wg_ref, wu_ref, wd_ref, g_ref, b_ref, h_out, hb_out,
                h_ref, hb_ref, sems_in, sem_h, sem_hb, xb_ref, *, alpha, n_f, rows, col_chunk):
    i = pl.program_id(0)
    f = pl.program_id(1)
    tm, d = h_ref.shape
    row0 = pl.multiple_of(i * tm, tm)

    @pl.when(f == 0)
    def _():
        _load_rows_scaled(x_hbm, row0, h_ref, xb_ref, sems_in, alpha, rows)

    xb = xb_ref[...]
    gate = jnp.dot(xb, wg_ref[...], preferred_element_type=F32)
    up = jnp.dot(xb, wu_ref[...], preferred_element_type=F32)
    act = (0.5 * (gate * jax.nn.sigmoid(gate)) * up).astype(BF16)
    wd = wd_ref[...].astype(BF16)
    for c in range(d // col_chunk):
        sl = slice(c * col_chunk, (c + 1) * col_chunk)
        h_ref[:, sl] += jnp.dot(act, wd[:, sl], preferred_element_type=F32)

    @pl.when(f == n_f - 1)
    def _():
        _ln_epilogue(h_ref, hb_ref, g_ref, b_ref, rows, h_out, hb_out, row0, sem_h, sem_hb)


def _ffn_ln(x, wg, wu, wd, g, b, alpha):
    m, d = x.shape
    f = wg.shape[1]
    tm = _tile(m, 1024)
    tf = _tile(f, 256)
    n_f = f // tf
    kern = functools.partial(_ffn_kernel, alpha=alpha, n_f=n_f, rows=_tile(tm, LN_ROWS_PER_STEP),
                             col_chunk=_tile(d, 512))
    hbm = pl.BlockSpec(memory_space=pl.ANY)
    return pl.pallas_call(
        kern,
        grid=(m // tm, n_f),
        in_specs=[
            hbm,
            pl.BlockSpec((d, tf), lambda i, j: (0, j)),
            pl.BlockSpec((d, tf), lambda i, j: (0, j)),
            pl.BlockSpec((tf, d), lambda i, j: (j, 0)),
            pl.BlockSpec((1, d), lambda i, j: (0, 0)),
            pl.BlockSpec((1, d), lambda i, j: (0, 0)),
        ],
        out_specs=[hbm, hbm],
        out_shape=[jax.ShapeDtypeStruct((m, d), F32), jax.ShapeDtypeStruct((m, d), BF16)],
        scratch_shapes=_ln_tile_scratch(tm, d) + [pltpu.VMEM((tm, d), BF16)],
        compiler_params=_params(("parallel", "arbitrary")),
        name="ffn_ln",
    )(x, wg, wu, wd, g, b)


def _proj_kernel(offs_ref, x_ref, wt_ref, o_ref):
    del offs_ref
    o_ref[...] = lax.dot_general(x_ref[...], wt_ref[...].astype(BF16), (((1,), (1,)), ((), ())),
                                 preferred_element_type=F32).astype(o_ref.dtype)


def _proj(xb, w_t, col_ranges):
    m, d = xb.shape
    tm = _tile(m, 1024)
    tn = _tile(math.gcd(*[b - a for a, b in col_ranges]), 512)
    offsets = [a + k * tn for a, b in col_ranges for k in range((b - a) // tn)]
    assert all(o % 8 == 0 for o in offsets)
    grid_spec = pltpu.PrefetchScalarGridSpec(
        num_scalar_prefetch=1,
        grid=(m // tm, len(offsets)),
        in_specs=[
            pl.BlockSpec((tm, d), lambda i, j, offs: (i, 0)),
            pl.BlockSpec((pl.Element(tn), pl.Element(d)),
                         lambda i, j, offs: (pl.multiple_of(offs[j], 8), 0)),
        ],
        out_specs=pl.BlockSpec((tm, tn), lambda i, j, offs: (i, j)),
    )
    return pl.pallas_call(
        _proj_kernel,
        grid_spec=grid_spec,
        out_shape=jax.ShapeDtypeStruct((m, len(offsets) * tn), BF16),
        compiler_params=_params(("parallel", "arbitrary")),
        name="proj_main",
    )(jnp.asarray(offsets, jnp.int32), xb, w_t)


def _split3(v):
    hi = v.astype(BF16)
    r1 = v - hi.astype(F32)
    mid = r1.astype(BF16)
    lo = (r1 - mid.astype(F32)).astype(BF16)
    return hi, mid, lo


def _proj_small_kernel(x_ref, wc_ref, wk_ref, wft_ref, bf_ref, kvg_ref, kg_ref, kb_ref,
                       ckv_ref, ki_ref, wi_ref, cum_ref, carry_ref, *, n_ih, tiles_per_seq):
    i = pl.program_id(0)
    x = x_ref[...]
    tm = x.shape[0]
    nt = (((1,), (1,)), ((), ()))
    c = lax.dot_general(x, wc_ref[...].astype(BF16), nt, preferred_element_type=F32)
    c = c * lax.rsqrt(jnp.mean(c * c, axis=-1, keepdims=True) + RMS_EPS) * kvg_ref[...]
    ckv_ref[...] = c.astype(BF16)
    kw = lax.dot_general(x, wk_ref[...].astype(BF16), nt, preferred_element_type=F32)
    ki_ref[...] = _layer_norm_rows(kw[:, :IDX_DIM], kg_ref[...], kb_ref[...]).astype(BF16)
    wi_ref[...] = kw[:, IDX_DIM:IDX_DIM + n_ih]

    ft = lax.dot_general(wft_ref[...].astype(BF16), x, nt, preferred_element_type=F32)
    z = ft + bf_ref[...]
    log_f = jnp.minimum(z, 0.0) - jnp.log1p(jnp.exp(-jnp.abs(z)))

    @pl.when(i % tiles_per_seq == 0)
    def _():
        carry_ref[...] = jnp.zeros_like(carry_ref)

    row = lax.broadcasted_iota(jnp.int32, (tm, tm), 0)
    col = lax.broadcasted_iota(jnp.int32, (tm, tm), 1)
    tri = jnp.where(row <= col, 1.0, 0.0).astype(BF16)
    hi, mid, lo = _split3(log_f)
    cs = (jnp.dot(hi, tri, preferred_element_type=F32)
          + jnp.dot(mid, tri, preferred_element_type=F32)
          + jnp.dot(lo, tri, preferred_element_type=F32))
    cum = cs + carry_ref[:, 0:1]
    cum_ref[...] = cum
    carry_ref[...] = jnp.broadcast_to(cum[:, tm - 1:tm], carry_ref.shape)


def _proj_small(xb, w_t, bf, kvg, kg, kb, *, o_ckv, o_ki, o_f, c_lat, n_ih, seq):
    m, d = xb.shape
    hf = bf.shape[0]
    tm = _tile(seq, 512)
    kern = functools.partial(_proj_small_kernel, n_ih=n_ih, tiles_per_seq=seq // tm)

    def rows(start, count):
        return pl.BlockSpec((pl.Element(count), pl.Element(d)), lambda i: (start, 0),
                            pipeline_mode=pl.Buffered(1))

    return pl.pallas_call(
        kern,
        grid=(m // tm,),
        in_specs=[
            pl.BlockSpec((tm, d), lambda i: (i, 0)),
            rows(o_ckv, c_lat),
            rows(o_ki, IDX_DIM + n_ih),
            rows(o_f, hf),
            pl.BlockSpec((hf, 1), lambda i: (0, 0)),
            pl.BlockSpec((1, c_lat), lambda i: (0, 0)),
            pl.BlockSpec((1, IDX_DIM), lambda i: (0, 0)),
            pl.BlockSpec((1, IDX_DIM), lambda i: (0, 0)),
        ],
        out_specs=[
            pl.BlockSpec((tm, c_lat), lambda i: (i, 0)),
            pl.BlockSpec((tm, IDX_DIM), lambda i: (i, 0)),
            pl.BlockSpec((tm, n_ih), lambda i: (i, 0)),
            pl.BlockSpec((hf, tm), lambda i: (0, i)),
        ],
        out_shape=[
            jax.ShapeDtypeStruct((m, c_lat), BF16),
            jax.ShapeDtypeStruct((m, IDX_DIM), BF16),
            jax.ShapeDtypeStruct((m, n_ih), F32),
            jax.ShapeDtypeStruct((hf, m), F32),
        ],
        scratch_shapes=[pltpu.VMEM((hf, LANES), F32)],
        compiler_params=_params(("arbitrary",)),
        name="proj_small",
    )(xb, w_t, w_t, w_t, bf, kvg, kg, kb)


def _dsa_kernel(qi_ref, qa_ref, wi_ref, ki_ref, ckv_ref, wuk_ref, wuv_ref, tab_ref, bfar_ref,
                o_ref, score_ref, score_t_ref, qlat_ref, acc_ref, m_ref, l_ref, p_ref, alpha_ref, *,
                n_ih, n_h, k_sel, tk, hpg):
    i = pl.program_id(1)
    start = i * Q_BLOCK
    qb = Q_BLOCK
    bpc = tk // LANES
    scale = HEAD_DIM ** -0.5
    w_fold = (IDX_DIM ** -0.5) * (n_ih ** -0.5)

    wcol = wi_ref[...] * w_fold
    row = lax.broadcasted_iota(jnp.int32, (qb, tk), 0)
    col = lax.broadcasted_iota(jnp.int32, (qb, tk), 1)
    limit = start + jnp.where(row < CHUNK, CHUNK, 2 * CHUNK)
    n_chunks = (i + bpc) // bpc

    def score_chunk(c, carry):
        k0 = pl.multiple_of(c * tk, tk)
        kblk = ki_ref[pl.ds(k0, tk), :]
        sc = jnp.zeros((qb, tk), F32)
        for h in range(n_ih):
            s = lax.dot_general(qi_ref[:, h * IDX_DIM:(h + 1) * IDX_DIM], kblk,
                                (((1,), (1,)), ((), ())), preferred_element_type=F32)
            sc = sc + jnp.maximum(s, 0.0) * wcol[:, h:h + 1]
        sc = jnp.where(col + k0 < limit, sc, -jnp.inf)
        for j in range(bpc):
            blk = sc[:, j * LANES:(j + 1) * LANES]
            score_ref[c * bpc + j] = blk
            score_t_ref[c * bpc + j] = blk.T
        return carry

    lax.fori_loop(0, n_chunks, score_chunk, 0)

    def code_to_float(code):
        bits = jnp.where(code >= 0, code, code ^ 0x7FFFFFFF)
        return lax.bitcast_convert_type(bits, F32)

    def bit_body(bi, code):
        cand = code + lax.shift_left(jnp.int32(1), 31 - bi)
        cf = jnp.concatenate([code_to_float(cand)] * (LANES // 8), axis=0)

        def count_chunk(c, cnt):
            for j in range(bpc):
                hit = jnp.where(score_t_ref[c * bpc + j] >= cf, 1.0, 0.0)
                cnt = cnt + jnp.sum(hit.reshape(LANES // 8, 8, qb), axis=0)
            return cnt

        cnt = lax.fori_loop(0, n_chunks, count_chunk, jnp.zeros((8, qb), F32))
        total = jnp.sum(cnt, axis=0, keepdims=True)
        return jnp.where(total >= k_sel, cand, code)

    code = lax.fori_loop(0, 32, bit_body, jnp.full((8, qb), INT_MIN, jnp.int32))
    thr_row = jnp.where(code <= CODE_NEG_INF, F32_LOWEST, code_to_float(code))
    thr = jnp.concatenate([thr_row] * (qb // 8), axis=0).T

    for h in range(n_h):
        ql = jnp.dot(qa_ref[:, h * HEAD_DIM:(h + 1) * HEAD_DIM], wuk_ref[h],
                     preferred_element_type=F32)
        qlat_ref[h * qb:(h + 1) * qb, :] = ql.astype(BF16)

    def flash_step(k0, width, sel, bias_of_head, first):
        cblk = ckv_ref[pl.ds(k0, width), :]
        s_of = {}
        for g in range(n_h // hpg):
            sg = lax.dot_general(qlat_ref[g * hpg * qb:(g + 1) * hpg * qb, :], cblk,
                                 (((1,), (1,)), ((), ())), preferred_element_type=F32)
            for hh in range(hpg):
                s_of[g * hpg + hh] = sg[hh * qb:(hh + 1) * qb, :]
        for g in range(n_h // hpg):
            rg = slice(g * hpg * qb, (g + 1) * hpg * qb)
            heads = [g * hpg + hh for hh in range(hpg)]
            sh, m_new = {}, {}
            for h in heads:
                rs = slice(h * qb, (h + 1) * qb)
                sh[h] = jnp.where(sel, s_of[h] * (scale * LOG2E) + bias_of_head(h), NEG)
                row_max = jnp.broadcast_to(jnp.max(sh[h], axis=-1, keepdims=True), (qb, LANES))
                m_new[h] = row_max if first else jnp.maximum(m_ref[rs, :], row_max)
            for h in heads:
                rs = slice(h * qb, (h + 1) * qb)
                p = jnp.exp2(sh[h] - _widen(m_new[h], width))
                row_sum = jnp.broadcast_to(jnp.sum(p, axis=-1, keepdims=True), (qb, LANES))
                if first:
                    l_ref[rs, :] = row_sum
                else:
                    a = jnp.exp2(m_ref[rs, :] - m_new[h])
                    l_ref[rs, :] = a * l_ref[rs, :] + row_sum
                    alpha_ref[rs, :] = a
                m_ref[rs, :] = m_new[h]
                p_ref[rs, 0:width] = p.astype(BF16)
            pv = jnp.dot(p_ref[rg, 0:width], cblk, preferred_element_type=F32)
            if first:
                acc_ref[rg, :] = pv
            else:
                acc_ref[rg, :] = acc_ref[rg, :] * _widen(alpha_ref[rg, :], pv.shape[1]) + pv

    wb = jnp.maximum(i - 1, 0)
    tsel = jnp.where(i == 0, 1, 0)
    scw = jnp.concatenate([score_ref[wb], score_ref[wb + 1]], axis=-1)
    selw = scw >= _widen(thr, 2 * LANES)
    flash_step(pl.multiple_of(wb * LANES, LANES), 2 * LANES, selw, lambda h: tab_ref[tsel, h],
               first=True)

    far_end = (i - 1) * qb

    def far_chunk(c, carry):
        k0 = pl.multiple_of(c * tk, tk)
        sc = jnp.concatenate([score_ref[c * bpc + j] for j in range(bpc)], axis=-1)
        sel = (sc >= _widen(thr, tk)) & (col + k0 < far_end)
        flash_step(k0, tk, sel, lambda h: bfar_ref[h], first=False)
        return carry

    n_far = jnp.maximum((i - 1 + bpc - 1) // bpc, 0)
    lax.fori_loop(0, n_far, far_chunk, 0)

    for h in range(n_h):
        rs = slice(h * qb, (h + 1) * qb)
        o_lat = (acc_ref[rs, :] / _widen(l_ref[rs, :], acc_ref.shape[1])).astype(BF16)
        o_ref[:, h * HEAD_DIM:(h + 1) * HEAD_DIM] = jnp.dot(
            o_lat, wuv_ref[h], preferred_element_type=F32).astype(o_ref.dtype)


def _dsa(qkv, wi, ki, ckv, wuk, wuv, tab, bfar, *, batch, seq, n_ih, n_h, qi_blk, qa_blk):
    m = qkv.shape[0]
    c_lat = ckv.shape[1]
    nq = seq // Q_BLOCK
    k_sel = min(TOPK_MAX, seq // 4)
    tk = _tile(seq, 512)
    rows = n_h * Q_BLOCK
    kern = functools.partial(_dsa_kernel, n_ih=n_ih, n_h=n_h, k_sel=k_sel, tk=tk,
                             hpg=math.gcd(n_h, DSA_HEADS_PER_GROUP))
    const = dict(pipeline_mode=pl.Buffered(1))
    return pl.pallas_call(
        kern,
        grid=(batch, nq),
        in_specs=[
            pl.BlockSpec((Q_BLOCK, n_ih * IDX_DIM), lambda b, i: (b * nq + i, qi_blk)),
            pl.BlockSpec((Q_BLOCK, n_h * HEAD_DIM), lambda b, i: (b * nq + i, qa_blk)),
            pl.BlockSpec((Q_BLOCK, n_ih), lambda b, i: (b * nq + i, 0)),
            pl.BlockSpec((seq, IDX_DIM), lambda b, i: (b, 0)),
            pl.BlockSpec((seq, c_lat), lambda b, i: (b, 0)),
            pl.BlockSpec((n_h, HEAD_DIM, c_lat), lambda b, i: (0, 0, 0), **const),
            pl.BlockSpec((n_h, c_lat, HEAD_DIM), lambda b, i: (0, 0, 0), **const),
            pl.BlockSpec((2, n_h, Q_BLOCK, 2 * LANES), lambda b, i: (0, 0, 0, 0), **const),
            pl.BlockSpec(memory_space=pltpu.SMEM),
        ],
        out_specs=pl.BlockSpec((Q_BLOCK, n_h * HEAD_DIM), lambda b, i: (b * nq + i, 0)),
        out_shape=jax.ShapeDtypeStruct((m, n_h * HEAD_DIM), BF16),
        scratch_shapes=[
            pltpu.VMEM((seq // LANES, Q_BLOCK, LANES), F32),
            pltpu.VMEM((seq // LANES, LANES, Q_BLOCK), F32),
            pltpu.VMEM((rows, c_lat), BF16),
            pltpu.VMEM((rows, c_lat), F32),
            pltpu.VMEM((rows, LANES), F32),
            pltpu.VMEM((rows, LANES), F32),
            pltpu.VMEM((rows, tk), BF16),
            pltpu.VMEM((rows, LANES), F32),
        ],
        compiler_params=_params(("parallel", "arbitrary")),
        name="dsa_mixer",
    )(qkv, qkv, wi, ki, ckv, wuk, wuv, tab, bfar)


def _fox_kernel(q_ref, k_ref, v_ref, cq_ref, ck_ref, o_ref, acc_ref, m_ref, l_ref, *, tq, hpb):
    hp = pl.program_id(1)
    i = pl.program_id(2)
    c1 = (HEAD_DIM ** -0.5) * LOG2E
    lane = lax.broadcasted_iota(jnp.int32, cq_ref.shape, 1)
    cq_all = cq_ref[...] * LOG2E
    cq = [jnp.sum(jnp.where(lane == hp * hpb + hh, cq_all, 0.0), axis=-1, keepdims=True)
          for hh in range(hpb)]

    def step(kb, first):
        causal = first
        k0 = pl.multiple_of(kb * tq, tq)
        logits = []
        for hh in range(hpb):
            ls = slice(hh * HEAD_DIM, (hh + 1) * HEAD_DIM)
            logits.append(lax.dot_general(q_ref[:, ls], k_ref[pl.ds(k0, tq), ls],
                                          (((1,), (1,)), ((), ())), preferred_element_type=F32))
        sh, m_new = [], []
        for hh in range(hpb):
            ck = ck_ref[pl.ds(hp * hpb + hh, 1), pl.ds(kb, 1), :].reshape(1, tq) * LOG2E
            s = logits[hh] * c1 + (cq[hh] - ck)
            if causal:
                row = lax.broadcasted_iota(jnp.int32, (tq, tq), 0)
                col = lax.broadcasted_iota(jnp.int32, (tq, tq), 1)
                s = jnp.where(col <= row, s, NEG)
            sh.append(s)
            row_max = jnp.broadcast_to(jnp.max(s, axis=-1, keepdims=True), (tq, LANES))
            m_new.append(row_max if first else jnp.maximum(m_ref[hh], row_max))
        for hh in range(hpb):
            ls = slice(hh * HEAD_DIM, (hh + 1) * HEAD_DIM)
            p = jnp.exp2(sh[hh] - _widen(m_new[hh], tq))
            row_sum = jnp.broadcast_to(jnp.sum(p, axis=-1, keepdims=True), (tq, LANES))
            pv = jnp.dot(p.astype(BF16), v_ref[pl.ds(k0, tq), ls], preferred_element_type=F32)
            if first:
                l_ref[hh] = row_sum
                acc_ref[hh] = pv
            else:
                a = jnp.exp2(m_ref[hh] - m_new[hh])
                l_ref[hh] = a * l_ref[hh] + row_sum
                acc_ref[hh] = acc_ref[hh] * a + pv
            m_ref[hh] = m_new[hh]

    def full_block(kb, carry):
        step(kb, False)
        return carry

    step(i, True)
    lax.fori_loop(0, i, full_block, 0)
    for hh in range(hpb):
        o_ref[:, hh * HEAD_DIM:(hh + 1) * HEAD_DIM] = (acc_ref[hh] / l_ref[hh]).astype(o_ref.dtype)


def _fox(qkv, cum_tok, cum_blk, *, batch, seq, n_h, q_blk0, k_blk0, v_blk0):
    m = qkv.shape[0]
    tq = _tile(seq, 512)
    nq = seq // tq
    hpb = math.gcd(n_h, FOX_HEADS_PER_BLOCK)
    assert q_blk0 % hpb == 0 and k_blk0 % hpb == 0 and v_blk0 % hpb == 0
    wb = hpb * HEAD_DIM
    kern = functools.partial(_fox_kernel, tq=tq, hpb=hpb)
    return pl.pallas_call(
        kern,
        grid=(batch, n_h // hpb, nq),
        in_specs=[
            pl.BlockSpec((tq, wb), lambda b, h, i: (b * nq + i, q_blk0 // hpb + h)),
            pl.BlockSpec((seq, wb), lambda b, h, i: (b, k_blk0 // hpb + h)),
            pl.BlockSpec((seq, wb), lambda b, h, i: (b, v_blk0 // hpb + h)),
            pl.BlockSpec((tq, n_h), lambda b, h, i: (b * nq + i, 0)),
            pl.BlockSpec((n_h, nq, tq), lambda b, h, i: (0, b, 0)),
        ],
        out_specs=pl.BlockSpec((tq, wb), lambda b, h, i: (b * nq + i, h)),
        out_shape=jax.ShapeDtypeStruct((m, n_h * HEAD_DIM), BF16),
        scratch_shapes=[
            pltpu.VMEM((hpb, tq, HEAD_DIM), F32),
            pltpu.VMEM((hpb, tq, LANES), F32),
            pltpu.VMEM((hpb, tq, LANES), F32),
        ],
        compiler_params=_params(("parallel", "parallel", "arbitrary")),
        name="fox_mixer",
    )(qkv, qkv, qkv, cum_tok, cum_blk)


def _outproj_kernel(oa_ref, ob_ref, w_ref, res_hbm, g_ref, b_ref, h_out, hb_out,
                    h_ref, hb_ref, sems_in, sem_h, sem_hb, *, alpha, n_ka, n_k, rows, col_chunk):
    i = pl.program_id(0)
    k = pl.program_id(1)
    tm, d = h_ref.shape
    row0 = pl.multiple_of(i * tm, tm)

    @pl.when(k == 0)
    def _():
        _load_rows_scaled(res_hbm, row0, h_ref, None, sems_in, alpha, rows)

    def accumulate(src_ref):
        src = src_ref[...]
        for c in range(d // col_chunk):
            sl = slice(c * col_chunk, (c + 1) * col_chunk)
            h_ref[:, sl] += jnp.dot(src, w_ref[:, sl], preferred_element_type=F32)

    @pl.when(k < n_ka)
    def _():
        accumulate(oa_ref)

    @pl.when(k >= n_ka)
    def _():
        accumulate(ob_ref)

    @pl.when(k == n_k - 1)
    def _():
        _ln_epilogue(h_ref, hb_ref, g_ref, b_ref, rows, h_out, hb_out, row0, sem_h, sem_hb)


def _outproj_ln(oa, ob, w, res, g, b, alpha):
    m, wa = oa.shape
    wbw = ob.shape[1]
    d = w.shape[1]
    tm = _tile(m, 1024)
    tk = _tile(math.gcd(wa, wbw), 512)
    n_ka, n_kb = wa // tk, wbw // tk
    n_k = n_ka + n_kb
    kern = functools.partial(_outproj_kernel, alpha=alpha, n_ka=n_ka, n_k=n_k,
                             rows=_tile(tm, LN_ROWS_PER_STEP), col_chunk=_tile(d, 512))
    hbm = pl.BlockSpec(memory_space=pl.ANY)
    return pl.pallas_call(
        kern,
        grid=(m // tm, n_k),
        in_specs=[
            pl.BlockSpec((tm, tk), lambda i, k: (i, jnp.minimum(k, n_ka - 1))),
            pl.BlockSpec((tm, tk), lambda i, k: (i, jnp.maximum(k - n_ka, 0))),
            pl.BlockSpec((tk, d), lambda i, k: (k, 0)),
            hbm,
            pl.BlockSpec((1, d), lambda i, k: (0, 0)),
            pl.BlockSpec((1, d), lambda i, k: (0, 0)),
        ],
        out_specs=[hbm, hbm],
        out_shape=[jax.ShapeDtypeStruct((m, d), F32), jax.ShapeDtypeStruct((m, d), BF16)],
        scratch_shapes=_ln_tile_scratch(tm, d),
        compiler_params=_params(("parallel", "arbitrary")),
        name="outproj_ln",
    )(oa, ob, w, res, g, b)


def _t5_bucket(rel):
    nb = N_BUCKETS // 2
    max_exact = nb // 2
    ret = jnp.where(rel > 0, nb, 0)
    n = jnp.abs(rel)
    nf = jnp.maximum(n, 1).astype(jnp.float32)
    large = max_exact + (jnp.log(nf / max_exact) / math.log(MAX_DISTANCE / max_exact)
                         * (nb - max_exact)).astype(jnp.int32)
    large = jnp.minimum(large, nb - 1)
    return ret + jnp.where(n < max_exact, n, large)


def _bias_tables(rel_bias):
    n_h = rel_bias.shape[1]
    width = 2 * LANES
    period = 2 * width
    m = jnp.arange(period, dtype=jnp.int32)
    m = jnp.where(m >= width, m - period, m)
    rel = jnp.stack([m - LANES, m])
    u = jnp.moveaxis(rel_bias[_t5_bucket(rel)], -1, 1).astype(F32)
    flat = jnp.tile(u, (1, 1, Q_BLOCK))[..., :Q_BLOCK * (period - 1)]
    tab = flat.reshape(2, n_h, Q_BLOCK, period - 1)[..., :width]
    far = rel_bias[_t5_bucket(jnp.full((), -(MAX_DISTANCE + 1), jnp.int32))].astype(F32)
    return tab, far


def _hybrid_mixer_ln(h, hb, w_in, b_f, kv_norm_g, idx_k_g, idx_k_b, w_uk, w_uv, rel_bias, w_out,
                     ln_g, ln_b, alpha, batch, seq):
    m, d = h.shape
    n_h = w_uk.shape[0]
    c_lat = w_uk.shape[2]
    n_hf = b_f.shape[0]
    w_dsa = n_h * HEAD_DIM
    w_fox = n_hf * HEAD_DIM
    d_in = w_in.shape[1]
    n_ih = (d_in - w_dsa - c_lat - IDX_DIM - 3 * w_fox - n_hf) // (IDX_DIM + 1)
    w_qi = n_ih * IDX_DIM
    assert w_dsa + c_lat + w_qi + IDX_DIM + n_ih + 3 * w_fox + n_hf == d_in
    assert w_qi % w_dsa == 0 and seq % (2 * LANES) == 0

    o_ckv = w_dsa
    o_qi = o_ckv + c_lat
    o_ki = o_qi + w_qi
    o_fox = o_ki + IDX_DIM + n_ih
    o_f = o_fox + 3 * w_fox

    w_t = jnp.swapaxes(w_in, 0, 1)
    qkv = _proj(hb, w_t, ((o_qi, o_ki), (0, w_dsa), (o_fox, o_f)))
    ckv, ki, wi, cum_t = _proj_small(
        hb, w_t, b_f.reshape(n_hf, 1).astype(F32),
        kv_norm_g.reshape(1, c_lat), idx_k_g.reshape(1, IDX_DIM), idx_k_b.reshape(1, IDX_DIM),
        o_ckv=o_ckv, o_ki=o_ki, o_f=o_f, c_lat=c_lat, n_ih=n_ih, seq=seq)

    tab, far = _bias_tables(rel_bias)
    tab, far = tab * LOG2E, far * LOG2E
    o_a = _dsa(qkv, wi, ki, ckv, w_uk.astype(BF16), w_uv.astype(BF16), tab, far,
               batch=batch, seq=seq, n_ih=n_ih, n_h=n_h, qi_blk=0, qa_blk=w_qi // w_dsa)

    tq = _tile(seq, 512)
    blk0 = (w_qi + w_dsa) // HEAD_DIM
    o_b = _fox(qkv, cum_t.T, cum_t.reshape(n_hf, m // tq, tq), batch=batch, seq=seq, n_h=n_hf,
               q_blk0=blk0, k_blk0=blk0 + n_hf, v_blk0=blk0 + 2 * n_hf)

    return _outproj_ln(o_a, o_b, w_out.astype(BF16), h, ln_g.reshape(1, d), ln_b.reshape(1, d), alpha)


def kernel(x, ffn1_w_gate, ffn1_w_up, ffn1_w_down, ln1_g, ln1_b, w_in, b_f, kv_norm_g, idx_k_g,
           idx_k_b, w_uk, w_uv, rel_bias, w_out, ln2_g, ln2_b, ffn2_w_gate, ffn2_w_up,
           ffn2_w_down, ln3_g, ln3_b):
    batch, seq, d = x.shape
    depth = ffn1_w_gate.shape[0]
    alpha = (2.0 * depth) ** 0.25
    h = x.reshape(batch * seq, d)
    for l in range(depth):
        h, hb = _ffn_ln(h, ffn1_w_gate[l].astype(BF16), ffn1_w_up[l].astype(BF16),
                        ffn1_w_down[l], ln1_g[l].reshape(1, d), ln1_b[l].reshape(1, d),
                        alpha)
        h, hb = _hybrid_mixer_ln(h, hb, w_in[l], b_f[l], kv_norm_g[l], idx_k_g[l], idx_k_b[l],
                                 w_uk[l], w_uv[l], rel_bias, w_out[l], ln2_g[l], ln2_b[l], alpha,
                                 batch, seq)
        h, _ = _ffn_ln(h, ffn2_w_gate[l].astype(BF16), ffn2_w_up[l].astype(BF16),
                       ffn2_w_down[l], ln3_g[l].reshape(1, d), ln3_b[l].reshape(1, d),
                       alpha)
    return h.reshape(batch, seq, d)
```

```python
import functools
import math

import jax
import jax.numpy as jnp
import numpy as np
from jax import lax
from jax.experimental import pallas as pl
from jax.experimental.pallas import tpu as pltpu

CHUNK = 64
Q_BLOCK = 128
HEAD_DIM = 128
IDX_DIM = 128
TOPK_MAX = 256
N_BUCKETS = 32
MAX_DISTANCE = 128
LN_EPS = 1e-5
RMS_EPS = 1e-6
NEG = -1e30
LOG2E = math.log2(math.e)
DSA_HEADS_PER_GROUP = 4
FOX_HEADS_PER_BLOCK = 4
LN_ROWS_PER_STEP = 64
LN_ROW_GROUPS = 4
TILE_LOAD_CHUNKS = 8

LANES = 128
VMEM_LIMIT_BYTES = 56 * 1024 * 1024

BF16 = jnp.bfloat16
F32 = jnp.float32

INT_MIN = -(2 ** 31)
CODE_NEG_INF = -(2 ** 31) + 0x7FFFFF
F32_LOWEST = float(np.finfo(np.float32).min)


def _tile(dim, pref):
    t = min(dim, pref)
    while dim % t:
        t //= 2
    return t


def _params(sem):
    return pltpu.CompilerParams(dimension_semantics=sem, vmem_limit_bytes=VMEM_LIMIT_BYTES)


def _widen(x, width):
    reps = width // x.shape[1]
    return x if reps == 1 else jnp.concatenate([x] * reps, axis=-1)


def _layer_norm_rows(y, g, b):
    mu = jnp.mean(y, axis=-1, keepdims=True)
    d = y - mu
    var = jnp.mean(d * d, axis=-1, keepdims=True)
    return d * lax.rsqrt(var + LN_EPS) * g + b


def _load_rows_scaled(src_hbm, row0, acc_ref, xb_ref, sems, alpha, rows):
    chunk = acc_ref.shape[0] // TILE_LOAD_CHUNKS

    def copy(c):
        return pltpu.make_async_copy(src_hbm.at[pl.ds(row0 + c * chunk, chunk), :],
                                     acc_ref.at[pl.ds(c * chunk, chunk), :], sems.at[c])

    for c in range(TILE_LOAD_CHUNKS):
        copy(c).start()
    for c in range(TILE_LOAD_CHUNKS):
        copy(c).wait()

        def body(r, carry, c=c):
            r0 = pl.multiple_of(c * chunk + r * rows, rows)
            xr = acc_ref[pl.ds(r0, rows), :]
            if xb_ref is not None:
                xb_ref[pl.ds(r0, rows), :] = xr.astype(BF16)
            acc_ref[pl.ds(r0, rows), :] = alpha * xr
            return carry

        lax.fori_loop(0, chunk // rows, body, 0)


def _ln_epilogue(h_ref, hb_ref, g_ref, b_ref, rows, h_out, hb_out, row0, sem_h, sem_hb):
    g = g_ref[...]
    b = b_ref[...]
    sub = rows // LN_ROW_GROUPS
    n_steps = h_ref.shape[0] // rows

    def out_copies(r0):
        return (pltpu.make_async_copy(h_ref.at[pl.ds(r0, rows), :],
                                      h_out.at[pl.ds(row0 + r0, rows), :], sem_h),
                pltpu.make_async_copy(hb_ref.at[pl.ds(r0, rows), :],
                                      hb_out.at[pl.ds(row0 + r0, rows), :], sem_hb))

    def body(r, carry):
        r0 = pl.multiple_of(r * rows, rows)
        sls = [pl.ds(r0 + k * sub, sub) for k in range(LN_ROW_GROUPS)]
        mus = [jnp.mean(h_ref[sl, :], axis=-1, keepdims=True) for sl in sls]
        rstds = []
        for sl, mu in zip(sls, mus):
            d = h_ref[sl, :] - mu
            rstds.append(lax.rsqrt(jnp.mean(d * d, axis=-1, keepdims=True) + LN_EPS))
        for sl, mu, rstd in zip(sls, mus, rstds):
            out = (h_ref[sl, :] - mu) * rstd * g + b
            h_ref[sl, :] = out
            hb_ref[sl, :] = out.astype(BF16)
        for cp in out_copies(r0):
            cp.start()
        return carry

    lax.fori_loop(0, n_steps, body, 0)

    def drain(r, carry):
        for cp in out_copies(pl.multiple_of(r * rows, rows)):
            cp.wait()
        return carry

    lax.fori_loop(0, n_steps, drain, 0)


def _ln_tile_scratch(tm, d):
    return [pltpu.VMEM((tm, d), F32), pltpu.VMEM((tm, d), BF16),
            pltpu.SemaphoreType.DMA((TILE_LOAD_CHUNKS,)),
            pltpu.SemaphoreType.DMA(()), pltpu.SemaphoreType.DMA(())]


def _ffn_kernel(x_hbm, wg_ref, wu_ref, wd_ref, g_ref, b_ref, h_out, hb_out,
                h_ref, xb_ref, sems_in, sem_h, sem_hb, *, alpha, n_f, rows, col_chunk):
    i = pl.program_id(0)
    f = pl.program_id(1)
    tm, d = h_ref.shape
    row0 = pl.multiple_of(i * tm, tm)

    @pl.when(f == 0)
    def _():
        _load_rows_scaled(x_hbm, row0, h_ref, xb_ref, sems_in, alpha, rows)

    xb = xb_ref[...]
    gate = jnp.dot(xb, wg_ref[...].astype(BF16), preferred_element_type=F32)
    up = jnp.dot(xb, wu_ref[...].astype(BF16), preferred_element_type=F32)
    act = (0.5 * (gate * jax.nn.sigmoid(gate)) * up).astype(BF16)
    wd = wd_ref[...].astype(BF16)
    for c in range(d // col_chunk):
        sl = slice(c * col_chunk, (c + 1) * col_chunk)
        h_ref[:, sl] += jnp.dot(act, wd[:, sl], preferred_element_type=F32)

    @pl.when(f == n_f - 1)
    def _():
        _ln_epilogue(h_ref, xb_ref, g_ref, b_ref, rows, h_out, hb_out, row0, sem_h, sem_hb)


def _ffn_ln(x, wg, wu, wd, g, b, alpha):
    m, d = x.shape
    f = wg.shape[1]
    tm = _tile(m, 1024)
    tf = _tile(f, 256)
    n_f = f // tf
    kern = functools.partial(_ffn_kernel, alpha=alpha, n_f=n_f, rows=_tile(tm, LN_ROWS_PER_STEP),
                             col_chunk=_tile(d, 512))
    hbm = pl.BlockSpec(memory_space=pl.ANY)
    return pl.pallas_call(
        kern,
        grid=(m // tm, n_f),
        in_specs=[
            hbm,
            pl.BlockSpec((d, tf), lambda i, j: (0, j)),
            pl.BlockSpec((d, tf), lambda i, j: (0, j)),
            pl.BlockSpec((tf, d), lambda i, j: (j, 0)),
            pl.BlockSpec((1, d), lambda i, j: (0, 0)),
            pl.BlockSpec((1, d), lambda i, j: (0, 0)),
        ],
        out_specs=[hbm, hbm],
        out_shape=[jax.ShapeDtypeStruct((m, d), F32), jax.ShapeDtypeStruct((m, d), BF16)],
        scratch_shapes=_ln_tile_scratch(tm, d),
        compiler_params=_params(("parallel", "arbitrary")),
        name="ffn_ln",
    )(x, wg, wu, wd, g, b)


def _proj_kernel(offs_ref, x_ref, wt_ref, o_ref):
    del offs_ref
    o_ref[...] = lax.dot_general(x_ref[...], wt_ref[...].astype(BF16), (((1,), (1,)), ((), ())),
                                 preferred_element_type=F32).astype(o_ref.dtype)


def _proj(xb, w_t, col_ranges):
    m, d = xb.shape
    tm = _tile(m, 1024)
    tn = _tile(math.gcd(*[b - a for a, b in col_ranges]), 512)
    offsets = [a + k * tn for a, b in col_ranges for k in range((b - a) // tn)]
    assert all(o % 8 == 0 for o in offsets)
    grid_spec = pltpu.PrefetchScalarGridSpec(
        num_scalar_prefetch=1,
        grid=(m // tm, len(offsets)),
        in_specs=[
            pl.BlockSpec((tm, d), lambda i, j, offs: (i, 0)),
            pl.BlockSpec((pl.Element(tn), pl.Element(d)),
                         lambda i, j, offs: (pl.multiple_of(offs[j], 8), 0)),
        ],
        out_specs=pl.BlockSpec((tm, tn), lambda i, j, offs: (i, j)),
    )
    return pl.pallas_call(
        _proj_kernel,
        grid_spec=grid_spec,
        out_shape=jax.ShapeDtypeStruct((m, len(offsets) * tn), BF16),
        compiler_params=_params(("parallel", "arbitrary")),
        name="proj_main",
    )(jnp.asarray(offsets, jnp.int32), xb, w_t)


def _split3(v):
    hi = v.astype(BF16)
    r1 = v - hi.astype(F32)
    mid = r1.astype(BF16)
    lo = (r1 - mid.astype(F32)).astype(BF16)
    return hi, mid, lo


def _proj_small_kernel(x_ref, wc_ref, wk_ref, wft_ref, bf_ref, kvg_ref, kg_ref, kb_ref,
                       ckv_ref, ki_ref, wi_ref, cum_ref, carry_ref, *, n_ih, tiles_per_seq):
    i = pl.program_id(0)
    x = x_ref[...]
    tm = x.shape[0]
    nt = (((1,), (1,)), ((), ()))
    c = lax.dot_general(x, wc_ref[...].astype(BF16), nt, preferred_element_type=F32)
    c = c * lax.rsqrt(jnp.mean(c * c, axis=-1, keepdims=True) + RMS_EPS) * kvg_ref[...]
    ckv_ref[...] = c.astype(BF16)
    kw = lax.dot_general(x, wk_ref[...].astype(BF16), nt, preferred_element_type=F32)
    ki_ref[...] = _layer_norm_rows(kw[:, :IDX_DIM], kg_ref[...], kb_ref[...]).astype(BF16)
    wi_ref[...] = kw[:, IDX_DIM:IDX_DIM + n_ih]

    ft = lax.dot_general(wft_ref[...].astype(BF16), x, nt, preferred_element_type=F32)
    z = ft + bf_ref[...]
    log_f = jnp.minimum(z, 0.0) - jnp.log1p(jnp.exp(-jnp.abs(z)))

    @pl.when(i % tiles_per_seq == 0)
    def _():
        carry_ref[...] = jnp.zeros_like(carry_ref)

    row = lax.broadcasted_iota(jnp.int32, (tm, tm), 0)
    col = lax.broadcasted_iota(jnp.int32, (tm, tm), 1)
    tri = jnp.where(row <= col, 1.0, 0.0).astype(BF16)
    hi, mid, lo = _split3(log_f)
    cs = (jnp.dot(hi, tri, preferred_element_type=F32)
          + jnp.dot(mid, tri, preferred_element_type=F32)
          + jnp.dot(lo, tri, preferred_element_type=F32))
    cum = cs + carry_ref[:, 0:1]
    cum_ref[...] = cum
    carry_ref[...] = jnp.broadcast_to(cum[:, tm - 1:tm], carry_ref.shape)


def _proj_small(xb, w_t, bf, kvg, kg, kb, *, o_ckv, o_ki, o_f, c_lat, n_ih, seq):
    m, d = xb.shape
    hf = bf.shape[0]
    tm = _tile(seq, 512)
    kern = functools.partial(_proj_small_kernel, n_ih=n_ih, tiles_per_seq=seq // tm)

    def rows(start, count):
        return pl.BlockSpec((pl.Element(count), pl.Element(d)), lambda i: (start, 0),
                            pipeline_mode=pl.Buffered(1))

    return pl.pallas_call(
        kern,
        grid=(m // tm,),
        in_specs=[
            pl.BlockSpec((tm, d), lambda i: (i, 0)),
            rows(o_ckv, c_lat),
            rows(o_ki, IDX_DIM + n_ih),
            rows(o_f, hf),
            pl.BlockSpec((hf, 1), lambda i: (0, 0)),
            pl.BlockSpec((1, c_lat), lambda i: (0, 0)),
            pl.BlockSpec((1, IDX_DIM), lambda i: (0, 0)),
            pl.BlockSpec((1, IDX_DIM), lambda i: (0, 0)),
        ],
        out_specs=[
            pl.BlockSpec((tm, c_lat), lambda i: (i, 0)),
            pl.BlockSpec((tm, IDX_DIM), lambda i: (i, 0)),
            pl.BlockSpec((tm, n_ih), lambda i: (i, 0)),
            pl.BlockSpec((hf, tm), lambda i: (0, i)),
        ],
        out_shape=[
            jax.ShapeDtypeStruct((m, c_lat), BF16),
            jax.ShapeDtypeStruct((m, IDX_DIM), BF16),
            jax.ShapeDtypeStruct((m, n_ih), F32),
            jax.ShapeDtypeStruct((hf, m), F32),
        ],
        scratch_shapes=[pltpu.VMEM((hf, LANES), F32)],
        compiler_params=_params(("arbitrary",)),
        name="proj_small",
    )(xb, w_t, w_t, w_t, bf, kvg, kg, kb)


def _dsa_kernel(qi_ref, qa_ref, wi_ref, ki_ref, ckv_ref, wuk_ref, wuv_ref, tab_ref, bfar_ref,
                o_ref, score_ref, score_t_ref, qlat_ref, acc_ref, m_ref, l_ref, p_ref, alpha_ref, *,
                n_ih, n_h, k_sel, tk, hpg):
    i = pl.program_id(1)
    start = i * Q_BLOCK
    qb = Q_BLOCK
    bpc = tk // LANES
    scale = HEAD_DIM ** -0.5
    w_fold = (IDX_DIM ** -0.5) * (n_ih ** -0.5)

    wcol = wi_ref[...] * w_fold
    row = lax.broadcasted_iota(jnp.int32, (qb, tk), 0)
    col = lax.broadcasted_iota(jnp.int32, (qb, tk), 1)
    limit = start + jnp.where(row < CHUNK, CHUNK, 2 * CHUNK)
    n_chunks = (i + bpc) // bpc

    def score_chunk(c, carry):
        k0 = pl.multiple_of(c * tk, tk)
        kblk = ki_ref[pl.ds(k0, tk), :]
        sc = jnp.zeros((qb, tk), F32)
        for h in range(n_ih):
            s = lax.dot_general(qi_ref[:, h * IDX_DIM:(h + 1) * IDX_DIM], kblk,
                                (((1,), (1,)), ((), ())), preferred_element_type=F32)
            sc = sc + jnp.maximum(s, 0.0) * wcol[:, h:h + 1]
        sc = jnp.where(col + k0 < limit, sc, -jnp.inf)
        for j in range(bpc):
            blk = sc[:, j * LANES:(j + 1) * LANES]
            score_ref[c * bpc + j] = blk
            score_t_ref[c * bpc + j] = blk.T
        return carry

    lax.fori_loop(0, n_chunks, score_chunk, 0)

    def code_to_float(code):
        bits = jnp.where(code >= 0, code, code ^ 0x7FFFFFFF)
        return lax.bitcast_convert_type(bits, F32)

    def bit_body(bi, code):
        cand = code + lax.shift_left(jnp.int32(1), 31 - bi)
        cf = jnp.concatenate([code_to_float(cand)] * (LANES // 8), axis=0)

        def count_chunk(c, cnt):
            for j in range(bpc):
                hit = jnp.where(score_t_ref[c * bpc + j] >= cf, 1.0, 0.0)
                cnt = cnt + jnp.sum(hit.reshape(LANES // 8, 8, qb), axis=0)
            return cnt

        cnt = lax.fori_loop(0, n_chunks, count_chunk, jnp.zeros((8, qb), F32))
        total = jnp.sum(cnt, axis=0, keepdims=True)
        return jnp.where(total >= k_sel, cand, code)

    code = lax.fori_loop(0, 32, bit_body, jnp.full((8, qb), INT_MIN, jnp.int32))
    thr_row = jnp.where(code <= CODE_NEG_INF, F32_LOWEST, code_to_float(code))
    thr = jnp.concatenate([thr_row] * (qb // 8), axis=0).T

    for h in range(n_h):
        ql = jnp.dot(qa_ref[:, h * HEAD_DIM:(h + 1) * HEAD_DIM], wuk_ref[h],
                     preferred_element_type=F32)
        qlat_ref[h * qb:(h + 1) * qb, :] = ql.astype(BF16)

    def flash_step(k0, width, sel, bias_of_head, first):
        cblk = ckv_ref[pl.ds(k0, width), :]
        s_of = {}
        for g in range(n_h // hpg):
            sg = lax.dot_general(qlat_ref[g * hpg * qb:(g + 1) * hpg * qb, :], cblk,
                                 (((1,), (1,)), ((), ())), preferred_element_type=F32)
            for hh in range(hpg):
                s_of[g * hpg + hh] = sg[hh * qb:(hh + 1) * qb, :]
        for g in range(n_h // hpg):
            rg = slice(g * hpg * qb, (g + 1) * hpg * qb)
            heads = [g * hpg + hh for hh in range(hpg)]
            sh, m_new = {}, {}
            for h in heads:
                rs = slice(h * qb, (h + 1) * qb)
                sh[h] = jnp.where(sel, s_of[h] * (scale * LOG2E) + bias_of_head(h), NEG)
                row_max = jnp.broadcast_to(jnp.max(sh[h], axis=-1, keepdims=True), (qb, LANES))
                m_new[h] = row_max if first else jnp.maximum(m_ref[rs, :], row_max)
            for h in heads:
                rs = slice(h * qb, (h + 1) * qb)
                p = jnp.exp2(sh[h] - _widen(m_new[h], width))
                row_sum = jnp.broadcast_to(jnp.sum(p, axis=-1, keepdims=True), (qb, LANES))
                if first:
                    l_ref[rs, :] = row_sum
                else:
                    a = jnp.exp2(m_ref[rs, :] - m_new[h])
                    l_ref[rs, :] = a * l_ref[rs, :] + row_sum
                    alpha_ref[rs, :] = a
                m_ref[rs, :] = m_new[h]
                p_ref[rs, 0:width] = p.astype(BF16)
            pv = jnp.dot(p_ref[rg, 0:width], cblk, preferred_element_type=F32)
            if first:
                acc_ref[rg, :] = pv
            else:
                acc_ref[rg, :] = acc_ref[rg, :] * _widen(alpha_ref[rg, :], pv.shape[1]) + pv

    wb = jnp.maximum(i - 1, 0)
    tsel = jnp.where(i == 0, 1, 0)
    scw = jnp.concatenate([score_ref[wb], score_ref[wb + 1]], axis=-1)
    selw = scw >= _widen(thr, 2 * LANES)
    flash_step(pl.multiple_of(wb * LANES, LANES), 2 * LANES, selw, lambda h: tab_ref[tsel, h],
               first=True)

    far_end = (i - 1) * qb

    def far_chunk(c, carry):
        k0 = pl.multiple_of(c * tk, tk)
        sc = jnp.concatenate([score_ref[c * bpc + j] for j in range(bpc)], axis=-1)
        sel = (sc >= _widen(thr, tk)) & (col + k0 < far_end)
        flash_step(k0, tk, sel, lambda h: bfar_ref[h], first=False)
        return carry

    n_far = jnp.maximum((i - 1 + bpc - 1) // bpc, 0)
    lax.fori_loop(0, n_far, far_chunk, 0)

    for h in range(n_h):
        rs = slice(h * qb, (h + 1) * qb)
        o_lat = (acc_ref[rs, :] / _widen(l_ref[rs, :], acc_ref.shape[1])).astype(BF16)
        o_ref[:, h * HEAD_DIM:(h + 1) * HEAD_DIM] = jnp.dot(
            o_lat, wuv_ref[h], preferred_element_type=F32).astype(o_ref.dtype)


def _dsa(qkv, wi, ki, ckv, wuk, wuv, tab, bfar, *, batch, seq, n_ih, n_h, qi_blk, qa_blk):
    m = qkv.shape[0]
    c_lat = ckv.shape[1]
    nq = seq // Q_BLOCK
    k_sel = min(TOPK_MAX, seq // 4)
    tk = _tile(seq, 512)
    rows = n_h * Q_BLOCK
    kern = functools.partial(_dsa_kernel, n_ih=n_ih, n_h=n_h, k_sel=k_sel, tk=tk,
                             hpg=math.gcd(n_h, DSA_HEADS_PER_GROUP))
    const = dict(pipeline_mode=pl.Buffered(1))
    return pl.pallas_call(
        kern,
        grid=(batch, nq),
        in_specs=[
            pl.BlockSpec((Q_BLOCK, n_ih * IDX_DIM), lambda b, i: (b * nq + i, qi_blk)),
            pl.BlockSpec((Q_BLOCK, n_h * HEAD_DIM), lambda b, i: (b * nq + i, qa_blk)),
            pl.BlockSpec((Q_BLOCK, n_ih), lambda b, i: (b * nq + i, 0)),
            pl.BlockSpec((seq, IDX_DIM), lambda b, i: (b, 0)),
            pl.BlockSpec((seq, c_lat), lambda b, i: (b, 0)),
            pl.BlockSpec((n_h, HEAD_DIM, c_lat), lambda b, i: (0, 0, 0), **const),
            pl.BlockSpec((n_h, c_lat, HEAD_DIM), lambda b, i: (0, 0, 0), **const),
            pl.BlockSpec((2, n_h, Q_BLOCK, 2 * LANES), lambda b, i: (0, 0, 0, 0), **const),
            pl.BlockSpec(memory_space=pltpu.SMEM),
        ],
        out_specs=pl.BlockSpec((Q_BLOCK, n_h * HEAD_DIM), lambda b, i: (b * nq + i, 0)),
        out_shape=jax.ShapeDtypeStruct((m, n_h * HEAD_DIM), BF16),
        scratch_shapes=[
            pltpu.VMEM((seq // LANES, Q_BLOCK, LANES), F32),
            pltpu.VMEM((seq // LANES, LANES, Q_BLOCK), F32),
            pltpu.VMEM((rows, c_lat), BF16),
            pltpu.VMEM((rows, c_lat), F32),
            pltpu.VMEM((rows, LANES), F32),
            pltpu.VMEM((rows, LANES), F32),
            pltpu.VMEM((rows, tk), BF16),
            pltpu.VMEM((rows, LANES), F32),
        ],
        compiler_params=_params(("parallel", "arbitrary")),
        name="dsa_mixer",
    )(qkv, qkv, wi, ki, ckv, wuk, wuv, tab, bfar)


def _fox_kernel(q_ref, k_ref, v_ref, cq_ref, ck_ref, o_ref, acc_ref, m_ref, l_ref, *, tq, hpb):
    hp = pl.program_id(1)
    i = pl.program_id(2)
    c1 = (HEAD_DIM ** -0.5) * LOG2E
    lane = lax.broadcasted_iota(jnp.int32, cq_ref.shape, 1)
    cq_all = cq_ref[...] * LOG2E
    cq = [jnp.sum(jnp.where(lane == hp * hpb + hh, cq_all, 0.0), axis=-1, keepdims=True)
          for hh in range(hpb)]

    def step(kb, first):
        causal = first
        k0 = pl.multiple_of(kb * tq, tq)
        logits = []
        for hh in range(hpb):
            ls = slice(hh * HEAD_DIM, (hh + 1) * HEAD_DIM)
            logits.append(lax.dot_general(q_ref[:, ls], k_ref[pl.ds(k0, tq), ls],
                                          (((1,), (1,)), ((), ())), preferred_element_type=F32))
        sh, m_new = [], []
        for hh in range(hpb):
            ck = ck_ref[pl.ds(hp * hpb + hh, 1), pl.ds(kb, 1), :].reshape(1, tq) * LOG2E
            s = logits[hh] * c1 + (cq[hh] - ck)
            if causal:
                row = lax.broadcasted_iota(jnp.int32, (tq, tq), 0)
                col = lax.broadcasted_iota(jnp.int32, (tq, tq), 1)
                s = jnp.where(col <= row, s, NEG)
            sh.append(s)
            row_max = jnp.broadcast_to(jnp.max(s, axis=-1, keepdims=True), (tq, LANES))
            m_new.append(row_max if first else jnp.maximum(m_ref[hh], row_max))
        for hh in range(hpb):
            ls = slice(hh * HEAD_DIM, (hh + 1) * HEAD_DIM)
            p = jnp.exp2(sh[hh] - _widen(m_new[hh], tq))
            row_sum = jnp.broadcast_to(jnp.sum(p, axis=-1, keepdims=True), (tq, LANES))
            pv = jnp.dot(p.astype(BF16), v_ref[pl.ds(k0, tq), ls], preferred_element_type=F32)
            if first:
                l_ref[hh] = row_sum
                acc_ref[hh] = pv
            else:
                a = jnp.exp2(m_ref[hh] - m_new[hh])
                l_ref[hh] = a * l_ref[hh] + row_sum
                acc_ref[hh] = acc_ref[hh] * a + pv
            m_ref[hh] = m_new[hh]

    def full_block(kb, carry):
        step(kb, False)
        return carry

    step(i, True)
    lax.fori_loop(0, i, full_block, 0)
    for hh in range(hpb):
        o_ref[:, hh * HEAD_DIM:(hh + 1) * HEAD_DIM] = (acc_ref[hh] / l_ref[hh]).astype(o_ref.dtype)


def _fox(qkv, cum_tok, cum_blk, *, batch, seq, n_h, q_blk0, k_blk0, v_blk0):
    m = qkv.shape[0]
    tq = _tile(seq, 512)
    nq = seq // tq
    hpb = math.gcd(n_h, FOX_HEADS_PER_BLOCK)
    assert q_blk0 % hpb == 0 and k_blk0 % hpb == 0 and v_blk0 % hpb == 0
    wb = hpb * HEAD_DIM
    kern = functools.partial(_fox_kernel, tq=tq, hpb=hpb)
    return pl.pallas_call(
        kern,
        grid=(batch, n_h // hpb, nq),
        in_specs=[
            pl.BlockSpec((tq, wb), lambda b, h, i: (b * nq + i, q_blk0 // hpb + h)),
            pl.BlockSpec((seq, wb), lambda b, h, i: (b, k_blk0 // hpb + h)),
            pl.BlockSpec((seq, wb), lambda b, h, i: (b, v_blk0 // hpb + h)),
            pl.BlockSpec((tq, n_h), lambda b, h, i: (b * nq + i, 0)),
            pl.BlockSpec((n_h, nq, tq), lambda b, h, i: (0, b, 0)),
        ],
        out_specs=pl.BlockSpec((tq, wb), lambda b, h, i: (b * nq + i, h)),
        out_shape=jax.ShapeDtypeStruct((m, n_h * HEAD_DIM), BF16),
        scratch_shapes=[
            pltpu.VMEM((hpb, tq, HEAD_DIM), F32),
            pltpu.VMEM((hpb, tq, LANES), F32),
            pltpu.VMEM((hpb, tq, LANES), F32),
        ],
        compiler_params=_params(("parallel", "parallel", "arbitrary")),
        name="fox_mixer",
    )(qkv, qkv, qkv, cum_tok, cum_blk)


def _outproj_kernel(oa_ref, ob_ref, w_ref, res_hbm, g_ref, b_ref, h_out, hb_out,
                    h_ref, hb_ref, sems_in, sem_h, sem_hb, *, alpha, n_ka, n_k, rows, col_chunk):
    i = pl.program_id(0)
    k = pl.program_id(1)
    tm, d = h_ref.shape
    row0 = pl.multiple_of(i * tm, tm)

    @pl.when(k == 0)
    def _():
        _load_rows_scaled(res_hbm, row0, h_ref, None, sems_in, alpha, rows)

    def accumulate(src_ref):
        src = src_ref[...]
        for c in range(d // col_chunk):
            sl = slice(c * col_chunk, (c + 1) * col_chunk)
            h_ref[:, sl] += jnp.dot(src, w_ref[:, sl], preferred_element_type=F32)

    @pl.when(k < n_ka)
    def _():
        accumulate(oa_ref)

    @pl.when(k >= n_ka)
    def _():
        accumulate(ob_ref)

    @pl.when(k == n_k - 1)
    def _():
        _ln_epilogue(h_ref, hb_ref, g_ref, b_ref, rows, h_out, hb_out, row0, sem_h, sem_hb)


def _outproj_ln(oa, ob, w, res, g, b, alpha):
    m, wa = oa.shape
    wbw = ob.shape[1]
    d = w.shape[1]
    tm = _tile(m, 1024)
    tk = _tile(math.gcd(wa, wbw), 512)
    n_ka, n_kb = wa // tk, wbw // tk
    n_k = n_ka + n_kb
    kern = functools.partial(_outproj_kernel, alpha=alpha, n_ka=n_ka, n_k=n_k,
                             rows=_tile(tm, LN_ROWS_PER_STEP), col_chunk=_tile(d, 512))
    hbm = pl.BlockSpec(memory_space=pl.ANY)
    return pl.pallas_call(
        kern,
        grid=(m // tm, n_k),
        in_specs=[
            pl.BlockSpec((tm, tk), lambda i, k: (i, jnp.minimum(k, n_ka - 1))),
            pl.BlockSpec((tm, tk), lambda i, k: (i, jnp.maximum(k - n_ka, 0))),
            pl.BlockSpec((tk, d), lambda i, k: (k, 0)),
            hbm,
            pl.BlockSpec((1, d), lambda i, k: (0, 0)),
            pl.BlockSpec((1, d), lambda i, k: (0, 0)),
        ],
        out_specs=[hbm, hbm],
        out_shape=[jax.ShapeDtypeStruct((m, d), F32), jax.ShapeDtypeStruct((m, d), BF16)],
        scratch_shapes=_ln_tile_scratch(tm, d),
        compiler_params=_params(("parallel", "arbitrary")),
        name="outproj_ln",
    )(oa, ob, w, res, g, b)


def _t5_bucket(rel):
    nb = N_BUCKETS // 2
    max_exact = nb // 2
    ret = jnp.where(rel > 0, nb, 0)
    n = jnp.abs(rel)
    nf = jnp.maximum(n, 1).astype(jnp.float32)
    large = max_exact + (jnp.log(nf / max_exact) / math.log(MAX_DISTANCE / max_exact)
                         * (nb - max_exact)).astype(jnp.int32)
    large = jnp.minimum(large, nb - 1)
    return ret + jnp.where(n < max_exact, n, large)


def _bias_tables(rel_bias):
    n_h = rel_bias.shape[1]
    width = 2 * LANES
    period = 2 * width
    m = jnp.arange(period, dtype=jnp.int32)
    m = jnp.where(m >= width, m - period, m)
    rel = jnp.stack([m - LANES, m])
    u = jnp.moveaxis(rel_bias[_t5_bucket(rel)], -1, 1).astype(F32)
    flat = jnp.tile(u, (1, 1, Q_BLOCK))[..., :Q_BLOCK * (period - 1)]
    tab = flat.reshape(2, n_h, Q_BLOCK, period - 1)[..., :width]
    far = rel_bias[_t5_bucket(jnp.full((), -(MAX_DISTANCE + 1), jnp.int32))].astype(F32)
    return tab, far


def _hybrid_mixer_ln(h, hb, w_in, b_f, kv_norm_g, idx_k_g, idx_k_b, w_uk, w_uv, rel_bias, w_out,
                     ln_g, ln_b, alpha, batch, seq):
    m, d = h.shape
    n_h = w_uk.shape[0]
    c_lat = w_uk.shape[2]
    n_hf = b_f.shape[0]
    w_dsa = n_h * HEAD_DIM
    w_fox = n_hf * HEAD_DIM
    d_in = w_in.shape[1]
    n_ih = (d_in - w_dsa - c_lat - IDX_DIM - 3 * w_fox - n_hf) // (IDX_DIM + 1)
    w_qi = n_ih * IDX_DIM
    assert w_dsa + c_lat + w_qi + IDX_DIM + n_ih + 3 * w_fox + n_hf == d_in
    assert w_qi % w_dsa == 0 and seq % (2 * LANES) == 0

    o_ckv = w_dsa
    o_qi = o_ckv + c_lat
    o_ki = o_qi + w_qi
    o_fox = o_ki + IDX_DIM + n_ih
    o_f = o_fox + 3 * w_fox

    w_t = jnp.swapaxes(w_in, 0, 1)
    qkv = _proj(hb, w_t, ((o_qi, o_ki), (0, w_dsa), (o_fox, o_f)))
    ckv, ki, wi, cum_t = _proj_small(
        hb, w_t, b_f.reshape(n_hf, 1).astype(F32),
        kv_norm_g.reshape(1, c_lat), idx_k_g.reshape(1, IDX_DIM), idx_k_b.reshape(1, IDX_DIM),
        o_ckv=o_ckv, o_ki=o_ki, o_f=o_f, c_lat=c_lat, n_ih=n_ih, seq=seq)

    tab, far = _bias_tables(rel_bias)
    tab, far = tab * LOG2E, far * LOG2E
    o_a = _dsa(qkv, wi, ki, ckv, w_uk.astype(BF16), w_uv.astype(BF16), tab, far,
               batch=batch, seq=seq, n_ih=n_ih, n_h=n_h, qi_blk=0, qa_blk=w_qi // w_dsa)

    tq = _tile(seq, 512)
    blk0 = (w_qi + w_dsa) // HEAD_DIM
    o_b = _fox(qkv, cum_t.T, cum_t.reshape(n_hf, m // tq, tq), batch=batch, seq=seq, n_h=n_hf,
               q_blk0=blk0, k_blk0=blk0 + n_hf, v_blk0=blk0 + 2 * n_hf)

    return _outproj_ln(o_a, o_b, w_out.astype(BF16), h, ln_g.reshape(1, d), ln_b.reshape(1, d), alpha)


def kernel(x, ffn1_w_gate, ffn1_w_up, ffn1_w_down, ln1_g, ln1_b, w_in, b_f, kv_norm_g, idx_k_g,
           idx_k_b, w_uk, w_uv, rel_bias, w_out, ln2_g, ln2_b, ffn2_w_gate, ffn2_w_up,
           ffn2_w_down, ln3_g, ln3_b):
    batch, seq, d = x.shape
    depth = ffn1_w_gate.shape[0]
    alpha = (2.0 * depth) ** 0.25
    h = x.reshape(batch * seq, d)
    for l in range(depth):
        h, hb = _ffn_ln(h, ffn1_w_gate[l], ffn1_w_up[l],
                        ffn1_w_down[l], ln1_g[l].reshape(1, d), ln1_b[l].reshape(1, d),
                        alpha)
        h, hb = _hybrid_mixer_ln(h, hb, w_in[l], b_f[l], kv_norm_g[l], idx_k_g[l], idx_k_b[l],
                                 w_uk[l], w_uv[l], rel_bias, w_out[l], ln2_g[l], ln2_b[l], alpha,
                                 batch, seq)
        h, _ = _ffn_ln(h, ffn2_w_gate[l], ffn2_w_up[l],
                       ffn2_w_down[l], ln3_g[l].reshape(1, d), ln3_b[l].reshape(1, d),
                       alpha)
    return h.reshape(batch, seq, d)
```

```python
import functools
import math

import jax
import jax.numpy as jnp
import numpy as np
from jax import lax
from jax.experimental import pallas as pl
from jax.experimental.pallas import tpu as pltpu

CHUNK = 64
Q_BLOCK = 128
HEAD_DIM = 128
IDX_DIM = 128
TOPK_MAX = 256
N_BUCKETS = 32
MAX_DISTANCE = 128
LN_EPS = 1e-5
RMS_EPS = 1e-6
NEG = -1e30
LOG2E = math.log2(math.e)
DSA_HEADS_PER_GROUP = 4
FOX_HEADS_PER_BLOCK = 4
LN_ROWS_PER_STEP = 128
LN_ROW_GROUPS = 8
TILE_LOAD_CHUNKS = 8

LANES = 128
VMEM_LIMIT_BYTES = 56 * 1024 * 1024

BF16 = jnp.bfloat16
F32 = jnp.float32

INT_MIN = -(2 ** 31)
CODE_NEG_INF = -(2 ** 31) + 0x7FFFFF
F32_LOWEST = float(np.finfo(np.float32).min)


def _tile(dim, pref):
    t = min(dim, pref)
    while dim % t:
        t //= 2
    return t


def _params(sem):
    return pltpu.CompilerParams(dimension_semantics=sem, vmem_limit_bytes=VMEM_LIMIT_BYTES)


def _widen(x, width):
    reps = width // x.shape[1]
    return x if reps == 1 else jnp.concatenate([x] * reps, axis=-1)


def _layer_norm_rows(y, g, b):
    mu = jnp.mean(y, axis=-1, keepdims=True)
    d = y - mu
    var = jnp.mean(d * d, axis=-1, keepdims=True)
    return d * lax.rsqrt(var + LN_EPS) * g + b


def _load_rows_scaled(src_hbm, row0, acc_ref, xb_ref, sems, alpha, rows):
    chunk = acc_ref.shape[0] // TILE_LOAD_CHUNKS

    def copy(c):
        return pltpu.make_async_copy(src_hbm.at[pl.ds(row0 + c * chunk, chunk), :],
                                     acc_ref.at[pl.ds(c * chunk, chunk), :], sems.at[c])

    for c in range(TILE_LOAD_CHUNKS):
        copy(c).start()
    for c in range(TILE_LOAD_CHUNKS):
        copy(c).wait()

        def body(r, carry, c=c):
            r0 = pl.multiple_of(c * chunk + r * rows, rows)
            xr = acc_ref[pl.ds(r0, rows), :]
            if xb_ref is not None:
                xb_ref[pl.ds(r0, rows), :] = xr.astype(BF16)
            acc_ref[pl.ds(r0, rows), :] = alpha * xr
            return carry

        lax.fori_loop(0, chunk // rows, body, 0)


def _ln_epilogue(h_ref, hb_ref, g_ref, b_ref, rows, h_out, hb_out, row0, sem_h, sem_hb):
    g = g_ref[...]
    b = b_ref[...]
    sub = rows // LN_ROW_GROUPS
    n_steps = h_ref.shape[0] // rows

    def out_copies(r0):
        return (pltpu.make_async_copy(h_ref.at[pl.ds(r0, rows), :],
                                      h_out.at[pl.ds(row0 + r0, rows), :], sem_h),
                pltpu.make_async_copy(hb_ref.at[pl.ds(r0, rows), :],
                                      hb_out.at[pl.ds(row0 + r0, rows), :], sem_hb))

    def body(r, carry):
        r0 = pl.multiple_of(r * rows, rows)
        sls = [pl.ds(r0 + k * sub, sub) for k in range(LN_ROW_GROUPS)]
        mus = [jnp.mean(h_ref[sl, :], axis=-1, keepdims=True) for sl in sls]
        rstds = []
        for sl, mu in zip(sls, mus):
            d = h_ref[sl, :] - mu
            rstds.append(lax.rsqrt(jnp.mean(d * d, axis=-1, keepdims=True) + LN_EPS))
        for sl, mu, rstd in zip(sls, mus, rstds):
            out = (h_ref[sl, :] - mu) * rstd * g + b
            h_ref[sl, :] = out
            hb_ref[sl, :] = out.astype(BF16)
        for cp in out_copies(r0):
            cp.start()
        return carry

    lax.fori_loop(0, n_steps, body, 0)

    def drain(r, carry):
        for cp in out_copies(pl.multiple_of(r * rows, rows)):
            cp.wait()
        return carry

    lax.fori_loop(0, n_steps, drain, 0)


def _ln_tile_scratch(tm, d):
    return [pltpu.VMEM((tm, d), F32), pltpu.VMEM((tm, d), BF16),
            pltpu.SemaphoreType.DMA((TILE_LOAD_CHUNKS,)),
            pltpu.SemaphoreType.DMA(()), pltpu.SemaphoreType.DMA(())]


def _ffn_kernel(x_hbm, wg_ref, wu_ref, wd_ref, g_ref, b_ref, h_out, hb_out,
                h_ref, xb_ref, sems_in, sem_h, sem_hb, *, alpha, n_f, rows, col_chunk):
    i = pl.program_id(0)
    f = pl.program_id(1)
    tm, d = h_ref.shape
    row0 = pl.multiple_of(i * tm, tm)

    @pl.when(f == 0)
    def _():
        _load_rows_scaled(x_hbm, row0, h_ref, xb_ref, sems_in, alpha, rows)

    xb = xb_ref[...]
    gate = jnp.dot(xb, wg_ref[...].astype(BF16), preferred_element_type=F32)
    up = jnp.dot(xb, wu_ref[...].astype(BF16), preferred_element_type=F32)
    act = (0.5 * (gate * jax.nn.sigmoid(gate)) * up).astype(BF16)
    wd = wd_ref[...].astype(BF16)
    for c in range(d // col_chunk):
        sl = slice(c * col_chunk, (c + 1) * col_chunk)
        h_ref[:, sl] += jnp.dot(act, wd[:, sl], preferred_element_type=F32)

    @pl.when(f == n_f - 1)
    def _():
        _ln_epilogue(h_ref, xb_ref, g_ref, b_ref, rows, h_out, hb_out, row0, sem_h, sem_hb)


def _ffn_ln(x, wg, wu, wd, g, b, alpha):
    m, d = x.shape
    f = wg.shape[1]
    tm = _tile(m, 1024)
    tf = _tile(f, 256)
    n_f = f // tf
    kern = functools.partial(_ffn_kernel, alpha=alpha, n_f=n_f, rows=_tile(tm, LN_ROWS_PER_STEP),
                             col_chunk=_tile(d, 512))
    hbm = pl.BlockSpec(memory_space=pl.ANY)
    return pl.pallas_call(
        kern,
        grid=(m // tm, n_f),
        in_specs=[
            hbm,
            pl.BlockSpec((d, tf), lambda i, j: (0, j)),
            pl.BlockSpec((d, tf), lambda i, j: (0, j)),
            pl.BlockSpec((tf, d), lambda i, j: (j, 0)),
            pl.BlockSpec((1, d), lambda i, j: (0, 0)),
            pl.BlockSpec((1, d), lambda i, j: (0, 0)),
        ],
        out_specs=[hbm, hbm],
        out_shape=[jax.ShapeDtypeStruct((m, d), F32), jax.ShapeDtypeStruct((m, d), BF16)],
        scratch_shapes=_ln_tile_scratch(tm, d),
        compiler_params=_params(("parallel", "arbitrary")),
        name="ffn_ln",
    )(x, wg, wu, wd, g, b)


def _proj_kernel(offs_ref, scales_ref, x_ref, wt_ref, o_ref):
    del offs_ref
    y = lax.dot_general(x_ref[...], wt_ref[...].astype(BF16), (((1,), (1,)), ((), ())),
                        preferred_element_type=F32)
    o_ref[...] = (y * scales_ref[pl.program_id(1)]).astype(o_ref.dtype)


def _proj(xb, w_t, col_ranges):
    m, d = xb.shape
    tm = _tile(m, 1024)
    tn = _tile(math.gcd(*[b - a for a, b, _ in col_ranges]), 512)
    tiles = [(a + k * tn, sc) for a, b, sc in col_ranges for k in range((b - a) // tn)]
    assert all(o % 8 == 0 for o, _ in tiles)
    grid_spec = pltpu.PrefetchScalarGridSpec(
        num_scalar_prefetch=2,
        grid=(m // tm, len(tiles)),
        in_specs=[
            pl.BlockSpec((tm, d), lambda i, j, offs, scales: (i, 0)),
            pl.BlockSpec((pl.Element(tn), pl.Element(d)),
                         lambda i, j, offs, scales: (pl.multiple_of(offs[j], 8), 0)),
        ],
        out_specs=pl.BlockSpec((tm, tn), lambda i, j, offs, scales: (i, j)),
    )
    return pl.pallas_call(
        _proj_kernel,
        grid_spec=grid_spec,
        out_shape=jax.ShapeDtypeStruct((m, len(tiles) * tn), BF16),
        compiler_params=_params(("parallel", "arbitrary")),
        name="proj_main",
    )(jnp.asarray([o for o, _ in tiles], jnp.int32), jnp.asarray([sc for _, sc in tiles], F32),
      xb, w_t)


def _split3(v):
    hi = v.astype(BF16)
    r1 = v - hi.astype(F32)
    mid = r1.astype(BF16)
    lo = (r1 - mid.astype(F32)).astype(BF16)
    return hi, mid, lo


def _proj_small_kernel(x_ref, wc_ref, wk_ref, wft_ref, bf_ref, kvg_ref, kg_ref, kb_ref,
                       ckv_ref, ki_ref, wi_ref, cum_ref, carry_ref, *, n_ih, tiles_per_seq):
    i = pl.program_id(0)
    x = x_ref[...]
    tm = x.shape[0]
    nt = (((1,), (1,)), ((), ()))
    c = lax.dot_general(x, wc_ref[...].astype(BF16), nt, preferred_element_type=F32)
    c = c * lax.rsqrt(jnp.mean(c * c, axis=-1, keepdims=True) + RMS_EPS) * kvg_ref[...]
    ckv_ref[...] = c.astype(BF16)
    kw = lax.dot_general(x, wk_ref[...].astype(BF16), nt, preferred_element_type=F32)
    ki_ref[...] = _layer_norm_rows(kw[:, :IDX_DIM], kg_ref[...], kb_ref[...]).astype(BF16)
    wi_ref[...] = kw[:, IDX_DIM:IDX_DIM + n_ih]

    ft = lax.dot_general(wft_ref[...].astype(BF16), x, nt, preferred_element_type=F32)
    z = ft + bf_ref[...]
    log_f = jnp.minimum(z, 0.0) - jnp.log1p(jnp.exp(-jnp.abs(z)))

    @pl.when(i % tiles_per_seq == 0)
    def _():
        carry_ref[...] = jnp.zeros_like(carry_ref)

    row = lax.broadcasted_iota(jnp.int32, (tm, tm), 0)
    col = lax.broadcasted_iota(jnp.int32, (tm, tm), 1)
    tri = jnp.where(row <= col, 1.0, 0.0).astype(BF16)
    hi, mid, lo = _split3(log_f)
    cs = (jnp.dot(hi, tri, preferred_element_type=F32)
          + jnp.dot(mid, tri, preferred_element_type=F32)
          + jnp.dot(lo, tri, preferred_element_type=F32))
    cum = cs + carry_ref[:, 0:1]
    cum_ref[...] = cum
    carry_ref[...] = jnp.broadcast_to(cum[:, tm - 1:tm], carry_ref.shape)


def _proj_small(xb, w_t, bf, kvg, kg, kb, *, o_ckv, o_ki, o_f, c_lat, n_ih, seq):
    m, d = xb.shape
    hf = bf.shape[0]
    tm = _tile(seq, 512)
    kern = functools.partial(_proj_small_kernel, n_ih=n_ih, tiles_per_seq=seq // tm)

    def rows(start, count):
        return pl.BlockSpec((pl.Element(count), pl.Element(d)), lambda i: (start, 0),
                            pipeline_mode=pl.Buffered(1))

    return pl.pallas_call(
        kern,
        grid=(m // tm,),
        in_specs=[
            pl.BlockSpec((tm, d), lambda i: (i, 0)),
            rows(o_ckv, c_lat),
            rows(o_ki, IDX_DIM + n_ih),
            rows(o_f, hf),
            pl.BlockSpec((hf, 1), lambda i: (0, 0)),
            pl.BlockSpec((1, c_lat), lambda i: (0, 0)),
            pl.BlockSpec((1, IDX_DIM), lambda i: (0, 0)),
            pl.BlockSpec((1, IDX_DIM), lambda i: (0, 0)),
        ],
        out_specs=[
            pl.BlockSpec((tm, c_lat), lambda i: (i, 0)),
            pl.BlockSpec((tm, IDX_DIM), lambda i: (i, 0)),
            pl.BlockSpec((tm, n_ih), lambda i: (i, 0)),
            pl.BlockSpec((hf, tm), lambda i: (0, i)),
        ],
        out_shape=[
            jax.ShapeDtypeStruct((m, c_lat), BF16),
            jax.ShapeDtypeStruct((m, IDX_DIM), BF16),
            jax.ShapeDtypeStruct((m, n_ih), F32),
            jax.ShapeDtypeStruct((hf, m), F32),
        ],
        scratch_shapes=[pltpu.VMEM((hf, LANES), F32)],
        compiler_params=_params(("arbitrary",)),
        name="proj_small",
    )(xb, w_t, w_t, w_t, bf, kvg, kg, kb)


def _dsa_kernel(qi_ref, qa_ref, wi_ref, ki_ref, ckv_ref, wuk_ref, wuv_ref, tab_ref, bfar_ref,
                o_ref, score_ref, score_t_ref, qlat_ref, acc_ref, m_ref, l_ref, p_ref, alpha_ref, *,
                n_ih, n_h, k_sel, tk, hpg):
    i = pl.program_id(1)
    start = i * Q_BLOCK
    qb = Q_BLOCK
    bpc = tk // LANES
    scale = HEAD_DIM ** -0.5
    w_fold = (IDX_DIM ** -0.5) * (n_ih ** -0.5)

    wcol = wi_ref[...] * w_fold
    row = lax.broadcasted_iota(jnp.int32, (qb, tk), 0)
    col = lax.broadcasted_iota(jnp.int32, (qb, tk), 1)
    limit = start + jnp.where(row < CHUNK, CHUNK, 2 * CHUNK)
    n_chunks = (i + bpc) // bpc

    def score_chunk(c, carry):
        k0 = pl.multiple_of(c * tk, tk)
        kblk = ki_ref[pl.ds(k0, tk), :]
        sc = jnp.zeros((qb, tk), F32)
        for h in range(n_ih):
            s = lax.dot_general(qi_ref[:, h * IDX_DIM:(h + 1) * IDX_DIM], kblk,
                                (((1,), (1,)), ((), ())), preferred_element_type=F32)
            sc = sc + jnp.maximum(s, 0.0) * wcol[:, h:h + 1]
        sc = jnp.where(col + k0 < limit, sc, -jnp.inf)
        for j in range(bpc):
            blk = sc[:, j * LANES:(j + 1) * LANES]
            score_ref[c * bpc + j] = blk
            score_t_ref[c * bpc + j] = blk.T
        return carry

    lax.fori_loop(0, n_chunks, score_chunk, 0)

    def code_to_float(code):
        bits = jnp.where(code >= 0, code, code ^ 0x7FFFFFFF)
        return lax.bitcast_convert_type(bits, F32)

    def bit_body(bi, code):
        cand = code + lax.shift_left(jnp.int32(1), 31 - bi)
        cf = jnp.concatenate([code_to_float(cand)] * (LANES // 8), axis=0)

        def count_chunk(c, cnt):
            for j in range(bpc):
                hit = jnp.where(score_t_ref[c * bpc + j] >= cf, 1.0, 0.0)
                cnt = cnt + jnp.sum(hit.reshape(LANES // 8, 8, qb), axis=0)
            return cnt

        cnt = lax.fori_loop(0, n_chunks, count_chunk, jnp.zeros((8, qb), F32))
        total = jnp.sum(cnt, axis=0, keepdims=True)
        return jnp.where(total >= k_sel, cand, code)

    code = lax.fori_loop(0, 32, bit_body, jnp.full((8, qb), INT_MIN, jnp.int32))
    thr_row = jnp.where(code <= CODE_NEG_INF, F32_LOWEST, code_to_float(code))
    thr = jnp.concatenate([thr_row] * (qb // 8), axis=0).T

    for h in range(n_h):
        ql = jnp.dot(qa_ref[:, h * HEAD_DIM:(h + 1) * HEAD_DIM], wuk_ref[h],
                     preferred_element_type=F32)
        qlat_ref[h * qb:(h + 1) * qb, :] = (ql * (scale * LOG2E)).astype(BF16)

    def flash_step(k0, width, sel, bias_of_head, first):
        cblk = ckv_ref[pl.ds(k0, width), :]
        s_of = {}
        for g in range(n_h // hpg):
            sg = lax.dot_general(qlat_ref[g * hpg * qb:(g + 1) * hpg * qb, :], cblk,
                                 (((1,), (1,)), ((), ())), preferred_element_type=F32)
            for hh in range(hpg):
                s_of[g * hpg + hh] = sg[hh * qb:(hh + 1) * qb, :]
        for g in range(n_h // hpg):
            rg = slice(g * hpg * qb, (g + 1) * hpg * qb)
            heads = [g * hpg + hh for hh in range(hpg)]
            sh, m_new = {}, {}
            for h in heads:
                rs = slice(h * qb, (h + 1) * qb)
                sh[h] = jnp.where(sel, s_of[h] + bias_of_head(h), NEG)
                row_max = jnp.broadcast_to(jnp.max(sh[h], axis=-1, keepdims=True), (qb, LANES))
                m_new[h] = row_max if first else jnp.maximum(m_ref[rs, :], row_max)
            for h in heads:
                rs = slice(h * qb, (h + 1) * qb)
                p = jnp.exp2(sh[h] - _widen(m_new[h], width))
                row_sum = jnp.broadcast_to(jnp.sum(p, axis=-1, keepdims=True), (qb, LANES))
                if first:
                    l_ref[rs, :] = row_sum
                else:
                    a = jnp.exp2(m_ref[rs, :] - m_new[h])
                    l_ref[rs, :] = a * l_ref[rs, :] + row_sum
                    alpha_ref[rs, :] = a
                m_ref[rs, :] = m_new[h]
                p_ref[rs, 0:width] = p.astype(BF16)
            pv = jnp.dot(p_ref[rg, 0:width], cblk, preferred_element_type=F32)
            if first:
                acc_ref[rg, :] = pv
            else:
                acc_ref[rg, :] = acc_ref[rg, :] * _widen(alpha_ref[rg, :], pv.shape[1]) + pv

    wb = jnp.maximum(i - 1, 0)
    tsel = jnp.where(i == 0, 1, 0)
    scw = jnp.concatenate([score_ref[wb], score_ref[wb + 1]], axis=-1)
    selw = scw >= _widen(thr, 2 * LANES)
    flash_step(pl.multiple_of(wb * LANES, LANES), 2 * LANES, selw, lambda h: tab_ref[tsel, h],
               first=True)

    far_end = (i - 1) * qb

    def far_chunk(c, carry):
        k0 = pl.multiple_of(c * tk, tk)
        sc = jnp.concatenate([score_ref[c * bpc + j] for j in range(bpc)], axis=-1)
        sel = (sc >= _widen(thr, tk)) & (col + k0 < far_end)
        flash_step(k0, tk, sel, lambda h: bfar_ref[h], first=False)
        return carry

    n_far = jnp.maximum((i - 1 + bpc - 1) // bpc, 0)
    lax.fori_loop(0, n_far, far_chunk, 0)

    for h in range(n_h):
        rs = slice(h * qb, (h + 1) * qb)
        o_lat = (acc_ref[rs, :] / _widen(l_ref[rs, :], acc_ref.shape[1])).astype(BF16)
        o_ref[:, h * HEAD_DIM:(h + 1) * HEAD_DIM] = jnp.dot(
            o_lat, wuv_ref[h], preferred_element_type=F32).astype(o_ref.dtype)


def _dsa(qkv, wi, ki, ckv, wuk, wuv, tab, bfar, *, batch, seq, n_ih, n_h, qi_blk, qa_blk):
    m = qkv.shape[0]
    c_lat = ckv.shape[1]
    nq = seq // Q_BLOCK
    k_sel = min(TOPK_MAX, seq // 4)
    tk = _tile(seq, 512)
    rows = n_h * Q_BLOCK
    kern = functools.partial(_dsa_kernel, n_ih=n_ih, n_h=n_h, k_sel=k_sel, tk=tk,
                             hpg=math.gcd(n_h, DSA_HEADS_PER_GROUP))
    const = dict(pipeline_mode=pl.Buffered(1))
    return pl.pallas_call(
        kern,
        grid=(batch, nq),
        in_specs=[
            pl.BlockSpec((Q_BLOCK, n_ih * IDX_DIM), lambda b, i: (b * nq + i, qi_blk)),
            pl.BlockSpec((Q_BLOCK, n_h * HEAD_DIM), lambda b, i: (b * nq + i, qa_blk)),
            pl.BlockSpec((Q_BLOCK, n_ih), lambda b, i: (b * nq + i, 0)),
            pl.BlockSpec((seq, IDX_DIM), lambda b, i: (b, 0)),
            pl.BlockSpec((seq, c_lat), lambda b, i: (b, 0)),
            pl.BlockSpec((n_h, HEAD_DIM, c_lat), lambda b, i: (0, 0, 0), **const),
            pl.BlockSpec((n_h, c_lat, HEAD_DIM), lambda b, i: (0, 0, 0), **const),
            pl.BlockSpec((2, n_h, Q_BLOCK, 2 * LANES), lambda b, i: (0, 0, 0, 0), **const),
            pl.BlockSpec(memory_space=pltpu.SMEM),
        ],
        out_specs=pl.BlockSpec((Q_BLOCK, n_h * HEAD_DIM), lambda b, i: (b * nq + i, 0)),
        out_shape=jax.ShapeDtypeStruct((m, n_h * HEAD_DIM), BF16),
        scratch_shapes=[
            pltpu.VMEM((seq // LANES, Q_BLOCK, LANES), F32),
            pltpu.VMEM((seq // LANES, LANES, Q_BLOCK), F32),
            pltpu.VMEM((rows, c_lat), BF16),
            pltpu.VMEM((rows, c_lat), F32),
            pltpu.VMEM((rows, LANES), F32),
            pltpu.VMEM((rows, LANES), F32),
            pltpu.VMEM((rows, tk), BF16),
            pltpu.VMEM((rows, LANES), F32),
        ],
        compiler_params=_params(("parallel", "arbitrary")),
        name="dsa_mixer",
    )(qkv, qkv, wi, ki, ckv, wuk, wuv, tab, bfar)


def _fox_kernel(q_ref, k_ref, v_ref, cq_ref, ck_ref, o_ref, acc_ref, m_ref, l_ref, *, tq, hpb):
    hp = pl.program_id(1)
    i = pl.program_id(2)
    lane = lax.broadcasted_iota(jnp.int32, cq_ref.shape, 1)
    cq_all = cq_ref[...] * LOG2E
    cq = [jnp.sum(jnp.where(lane == hp * hpb + hh, cq_all, 0.0), axis=-1, keepdims=True)
          for hh in range(hpb)]

    def step(kb, first):
        causal = first
        k0 = pl.multiple_of(kb * tq, tq)
        logits = []
        for hh in range(hpb):
            ls = slice(hh * HEAD_DIM, (hh + 1) * HEAD_DIM)
            logits.append(lax.dot_general(q_ref[:, ls], k_ref[pl.ds(k0, tq), ls],
                                          (((1,), (1,)), ((), ())), preferred_element_type=F32))
        sh, m_new = [], []
        for hh in range(hpb):
            ck = ck_ref[pl.ds(hp * hpb + hh, 1), pl.ds(kb, 1), :].reshape(1, tq) * LOG2E
            s = logits[hh] + (cq[hh] - ck)
            if causal:
                row = lax.broadcasted_iota(jnp.int32, (tq, tq), 0)
                col = lax.broadcasted_iota(jnp.int32, (tq, tq), 1)
                s = jnp.where(col <= row, s, NEG)
            sh.append(s)
            row_max = jnp.broadcast_to(jnp.max(s, axis=-1, keepdims=True), (tq, LANES))
            m_new.append(row_max if first else jnp.maximum(m_ref[hh], row_max))
        for hh in range(hpb):
            ls = slice(hh * HEAD_DIM, (hh + 1) * HEAD_DIM)
            p = jnp.exp2(sh[hh] - _widen(m_new[hh], tq))
            row_sum = jnp.broadcast_to(jnp.sum(p, axis=-1, keepdims=True), (tq, LANES))
            pv = jnp.dot(p.astype(BF16), v_ref[pl.ds(k0, tq), ls], preferred_element_type=F32)
            if first:
                l_ref[hh] = row_sum
                acc_ref[hh] = pv
            else:
                a = jnp.exp2(m_ref[hh] - m_new[hh])
                l_ref[hh] = a * l_ref[hh] + row_sum
                acc_ref[hh] = acc_ref[hh] * a + pv
            m_ref[hh] = m_new[hh]

    def full_block(kb, carry):
        step(kb, False)
        return carry

    step(i, True)
    lax.fori_loop(0, i, full_block, 0)
    for hh in range(hpb):
        o_ref[:, hh * HEAD_DIM:(hh + 1) * HEAD_DIM] = (acc_ref[hh] / l_ref[hh]).astype(o_ref.dtype)


def _fox(qkv, cum_tok, cum_blk, *, batch, seq, n_h, q_blk0, k_blk0, v_blk0):
    m = qkv.shape[0]
    tq = _tile(seq, 512)
    nq = seq // tq
    hpb = math.gcd(n_h, FOX_HEADS_PER_BLOCK)
    assert q_blk0 % hpb == 0 and k_blk0 % hpb == 0 and v_blk0 % hpb == 0
    wb = hpb * HEAD_DIM
    kern = functools.partial(_fox_kernel, tq=tq, hpb=hpb)
    return pl.pallas_call(
        kern,
        grid=(batch, n_h // hpb, nq),
        in_specs=[
            pl.BlockSpec((tq, wb), lambda b, h, i: (b * nq + i, q_blk0 // hpb + h)),
            pl.BlockSpec((seq, wb), lambda b, h, i: (b, k_blk0 // hpb + h)),
            pl.BlockSpec((seq, wb), lambda b, h, i: (b, v_blk0 // hpb + h)),
            pl.BlockSpec((tq, n_h), lambda b, h, i: (b * nq + i, 0)),
            pl.BlockSpec((n_h, nq, tq), lambda b, h, i: (0, b, 0)),
        ],
        out_specs=pl.BlockSpec((tq, wb), lambda b, h, i: (b * nq + i, h)),
        out_shape=jax.ShapeDtypeStruct((m, n_h * HEAD_DIM), BF16),
        scratch_shapes=[
            pltpu.VMEM((hpb, tq, HEAD_DIM), F32),
            pltpu.VMEM((hpb, tq, LANES), F32),
            pltpu.VMEM((hpb, tq, LANES), F32),
        ],
        compiler_params=_params(("parallel", "parallel", "arbitrary")),
        name="fox_mixer",
    )(qkv, qkv, qkv, cum_tok, cum_blk)


def _outproj_kernel(oa_ref, ob_ref, w_ref, res_hbm, g_ref, b_ref, h_out, hb_out,
                    h_ref, hb_ref, sems_in, sem_h, sem_hb, *, alpha, n_ka, n_k, rows, col_chunk):
    i = pl.program_id(0)
    k = pl.program_id(1)
    tm, d = h_ref.shape
    row0 = pl.multiple_of(i * tm, tm)

    @pl.when(k == 0)
    def _():
        _load_rows_scaled(res_hbm, row0, h_ref, None, sems_in, alpha, rows)

    def accumulate(src_ref):
        src = src_ref[...]
        for c in range(d // col_chunk):
            sl = slice(c * col_chunk, (c + 1) * col_chunk)
            h_ref[:, sl] += jnp.dot(src, w_ref[:, sl].astype(BF16), preferred_element_type=F32)

    @pl.when(k < n_ka)
    def _():
        accumulate(oa_ref)

    @pl.when(k >= n_ka)
    def _():
        accumulate(ob_ref)

    @pl.when(k == n_k - 1)
    def _():
        _ln_epilogue(h_ref, hb_ref, g_ref, b_ref, rows, h_out, hb_out, row0, sem_h, sem_hb)


def _outproj_ln(oa, ob, w, res, g, b, alpha):
    m, wa = oa.shape
    wbw = ob.shape[1]
    d = w.shape[1]
    tm = _tile(m, 1024)
    tk = _tile(math.gcd(wa, wbw), 512)
    n_ka, n_kb = wa // tk, wbw // tk
    n_k = n_ka + n_kb
    kern = functools.partial(_outproj_kernel, alpha=alpha, n_ka=n_ka, n_k=n_k,
                             rows=_tile(tm, LN_ROWS_PER_STEP), col_chunk=_tile(d, 512))
    hbm = pl.BlockSpec(memory_space=pl.ANY)
    return pl.pallas_call(
        kern,
        grid=(m // tm, n_k),
        in_specs=[
            pl.BlockSpec((tm, tk), lambda i, k: (i, jnp.minimum(k, n_ka - 1))),
            pl.BlockSpec((tm, tk), lambda i, k: (i, jnp.maximum(k - n_ka, 0))),
            pl.BlockSpec((tk, d), lambda i, k: (k, 0)),
            hbm,
            pl.BlockSpec((1, d), lambda i, k: (0, 0)),
            pl.BlockSpec((1, d), lambda i, k: (0, 0)),
        ],
        out_specs=[hbm, hbm],
        out_shape=[jax.ShapeDtypeStruct((m, d), F32), jax.ShapeDtypeStruct((m, d), BF16)],
        scratch_shapes=_ln_tile_scratch(tm, d),
        compiler_params=_params(("parallel", "arbitrary")),
        name="outproj_ln",
    )(oa, ob, w, res, g, b)


def _t5_bucket(rel):
    nb = N_BUCKETS // 2
    max_exact = nb // 2
    ret = jnp.where(rel > 0, nb, 0)
    n = jnp.abs(rel)
    nf = jnp.maximum(n, 1).astype(jnp.float32)
    large = max_exact + (jnp.log(nf / max_exact) / math.log(MAX_DISTANCE / max_exact)
                         * (nb - max_exact)).astype(jnp.int32)
    large = jnp.minimum(large, nb - 1)
    return ret + jnp.where(n < max_exact, n, large)


def _bias_tables(rel_bias):
    n_h = rel_bias.shape[1]
    width = 2 * LANES
    period = 2 * width
    m = jnp.arange(period, dtype=jnp.int32)
    m = jnp.where(m >= width, m - period, m)
    rel = jnp.stack([m - LANES, m])
    u = jnp.moveaxis(rel_bias[_t5_bucket(rel)], -1, 1).astype(F32)
    flat = jnp.tile(u, (1, 1, Q_BLOCK))[..., :Q_BLOCK * (period - 1)]
    tab = flat.reshape(2, n_h, Q_BLOCK, period - 1)[..., :width]
    far = rel_bias[_t5_bucket(jnp.full((), -(MAX_DISTANCE + 1), jnp.int32))].astype(F32)
    return tab, far


def _hybrid_mixer_ln(h, hb, w_in, b_f, kv_norm_g, idx_k_g, idx_k_b, w_uk, w_uv, rel_bias, w_out,
                     ln_g, ln_b, alpha, batch, seq):
    m, d = h.shape
    n_h = w_uk.shape[0]
    c_lat = w_uk.shape[2]
    n_hf = b_f.shape[0]
    w_dsa = n_h * HEAD_DIM
    w_fox = n_hf * HEAD_DIM
    d_in = w_in.shape[1]
    n_ih = (d_in - w_dsa - c_lat - IDX_DIM - 3 * w_fox - n_hf) // (IDX_DIM + 1)
    w_qi = n_ih * IDX_DIM
    assert w_dsa + c_lat + w_qi + IDX_DIM + n_ih + 3 * w_fox + n_hf == d_in
    assert w_qi % w_dsa == 0 and seq % (2 * LANES) == 0

    o_ckv = w_dsa
    o_qi = o_ckv + c_lat
    o_ki = o_qi + w_qi
    o_fox = o_ki + IDX_DIM + n_ih
    o_f = o_fox + 3 * w_fox

    w_t = jnp.swapaxes(w_in, 0, 1)
    fox_q_scale = (HEAD_DIM ** -0.5) * LOG2E
    qkv = _proj(hb, w_t, ((o_qi, o_ki, 1.0), (0, w_dsa, 1.0),
                          (o_fox, o_fox + w_fox, fox_q_scale), (o_fox + w_fox, o_f, 1.0)))
    ckv, ki, wi, cum_t = _proj_small(
        hb, w_t, b_f.reshape(n_hf, 1).astype(F32),
        kv_norm_g.reshape(1, c_lat), idx_k_g.reshape(1, IDX_DIM), idx_k_b.reshape(1, IDX_DIM),
        o_ckv=o_ckv, o_ki=o_ki, o_f=o_f, c_lat=c_lat, n_ih=n_ih, seq=seq)

    tab, far = _bias_tables(rel_bias)
    tab, far = tab * LOG2E, far * LOG2E
    o_a = _dsa(qkv, wi, ki, ckv, w_uk.astype(BF16), w_uv.astype(BF16), tab, far,
               batch=batch, seq=seq, n_ih=n_ih, n_h=n_h, qi_blk=0, qa_blk=w_qi // w_dsa)

    tq = _tile(seq, 512)
    blk0 = (w_qi + w_dsa) // HEAD_DIM
    o_b = _fox(qkv, cum_t.T, cum_t.reshape(n_hf, m // tq, tq), batch=batch, seq=seq, n_h=n_hf,
               q_blk0=blk0, k_blk0=blk0 + n_hf, v_blk0=blk0 + 2 * n_hf)

    return _outproj_ln(o_a, o_b, w_out, h, ln_g.reshape(1, d), ln_b.reshape(1, d), alpha)


def kernel(x, ffn1_w_gate, ffn1_w_up, ffn1_w_down, ln1_g, ln1_b, w_in, b_f, kv_norm_g, idx_k_g,
           idx_k_b, w_uk, w_uv, rel_bias, w_out, ln2_g, ln2_b, ffn2_w_gate, ffn2_w_up,
           ffn2_w_down, ln3_g, ln3_b):
    batch, seq, d = x.shape
    depth = ffn1_w_gate.shape[0]
    alpha = (2.0 * depth) ** 0.25
    h = x.reshape(batch * seq, d)
    for l in range(depth):
        h, hb = _ffn_ln(h, ffn1_w_gate[l], ffn1_w_up[l],
                        ffn1_w_down[l], ln1_g[l].reshape(1, d), ln1_b[l].reshape(1, d),
                        alpha)
        h, hb = _hybrid_mixer_ln(h, hb, w_in[l], b_f[l], kv_norm_g[l], idx_k_g[l], idx_k_b[l],
                                 w_uk[l], w_uv[l], rel_bias, w_out[l], ln2_g[l], ln2_b[l], alpha,
                                 batch, seq)
        h, _ = _ffn_ln(h, ffn2_w_gate[l], ffn2_w_up[l],
                       ffn2_w_down[l], ln3_g[l].reshape(1, d), ln3_b[l].reshape(1, d),
                       alpha)
    return h.reshape(batch, seq, d)
```

```python
import functools
import math

import jax
import jax.numpy as jnp
import numpy as np
from jax import lax
from jax.experimental import pallas as pl
from jax.experimental.pallas import tpu as pltpu

CHUNK = 64
Q_BLOCK = 128
HEAD_DIM = 128
IDX_DIM = 128
TOPK_MAX = 256
N_BUCKETS = 32
MAX_DISTANCE = 128
LN_EPS = 1e-5
RMS_EPS = 1e-6
NEG = -1e30
LOG2E = math.log2(math.e)
DSA_HEADS_PER_GROUP = 4
FOX_HEADS_PER_BLOCK = 4
LN_ROWS_PER_STEP = 128
LN_ROW_GROUPS = 8
TILE_LOAD_CHUNKS = 8

LANES = 128
VMEM_LIMIT_BYTES = 56 * 1024 * 1024

BF16 = jnp.bfloat16
F32 = jnp.float32

INT_MIN = -(2 ** 31)
CODE_NEG_INF = -(2 ** 31) + 0x7FFFFF
F32_LOWEST = float(np.finfo(np.float32).min)


def _tile(dim, pref):
    t = min(dim, pref)
    while dim % t:
        t //= 2
    return t


def _params(sem):
    return pltpu.CompilerParams(dimension_semantics=sem, vmem_limit_bytes=VMEM_LIMIT_BYTES)


def _widen(x, width):
    reps = width // x.shape[1]
    return x if reps == 1 else jnp.concatenate([x] * reps, axis=-1)


def _layer_norm_rows(y, g, b):
    mu = jnp.mean(y, axis=-1, keepdims=True)
    d = y - mu
    var = jnp.mean(d * d, axis=-1, keepdims=True)
    return d * lax.rsqrt(var + LN_EPS) * g + b


def _load_rows_scaled(src_hbm, row0, acc_ref, xb_ref, sems, alpha, rows):
    chunk = acc_ref.shape[0] // TILE_LOAD_CHUNKS

    def copy(c):
        return pltpu.make_async_copy(src_hbm.at[pl.ds(row0 + c * chunk, chunk), :],
                                     acc_ref.at[pl.ds(c * chunk, chunk), :], sems.at[c])

    for c in range(TILE_LOAD_CHUNKS):
        copy(c).start()
    for c in range(TILE_LOAD_CHUNKS):
        copy(c).wait()

        def body(r, carry, c=c):
            r0 = pl.multiple_of(c * chunk + r * rows, rows)
            xr = acc_ref[pl.ds(r0, rows), :]
            if xb_ref is not None:
                xb_ref[pl.ds(r0, rows), :] = xr.astype(BF16)
            acc_ref[pl.ds(r0, rows), :] = alpha * xr
            return carry

        lax.fori_loop(0, chunk // rows, body, 0)


def _ln_epilogue(h_ref, hb_ref, g_ref, b_ref, rows, h_out, hb_out, row0, sem_h, sem_hb):
    g = g_ref[...]
    b = b_ref[...]
    sub = rows // LN_ROW_GROUPS
    n_steps = h_ref.shape[0] // rows

    def out_copies(r0):
        return (pltpu.make_async_copy(h_ref.at[pl.ds(r0, rows), :],
                                      h_out.at[pl.ds(row0 + r0, rows), :], sem_h),
                pltpu.make_async_copy(hb_ref.at[pl.ds(r0, rows), :],
                                      hb_out.at[pl.ds(row0 + r0, rows), :], sem_hb))

    def body(r, carry):
        r0 = pl.multiple_of(r * rows, rows)
        sls = [pl.ds(r0 + k * sub, sub) for k in range(LN_ROW_GROUPS)]
        mus = [jnp.mean(h_ref[sl, :], axis=-1, keepdims=True) for sl in sls]
        rstds = []
        for sl, mu in zip(sls, mus):
            d = h_ref[sl, :] - mu
            rstds.append(lax.rsqrt(jnp.mean(d * d, axis=-1, keepdims=True) + LN_EPS))
        for sl, mu, rstd in zip(sls, mus, rstds):
            out = (h_ref[sl, :] - mu) * rstd * g + b
            h_ref[sl, :] = out
            hb_ref[sl, :] = out.astype(BF16)
        for cp in out_copies(r0):
            cp.start()
        return carry

    lax.fori_loop(0, n_steps, body, 0)

    def drain(r, carry):
        for cp in out_copies(pl.multiple_of(r * rows, rows)):
            cp.wait()
        return carry

    lax.fori_loop(0, n_steps, drain, 0)


def _ln_tile_scratch(tm, d):
    return [pltpu.VMEM((tm, d), F32), pltpu.VMEM((tm, d), BF16),
            pltpu.SemaphoreType.DMA((TILE_LOAD_CHUNKS,)),
            pltpu.SemaphoreType.DMA(()), pltpu.SemaphoreType.DMA(())]


def _ffn_kernel(x_hbm, wg_hbm, wu_hbm, wd_hbm, g_ref, b_ref, h_out, hb_out,
                h_ref, xb_ref, sems_in, sem_h, sem_hb, wg_buf, wu_buf, wd_buf, w_sems, *,
                alpha, n_f, tf, rows, col_chunk):
    tm, d = h_ref.shape
    row0 = pl.multiple_of(pl.program_id(0) * tm, tm)

    def weight_copies(f, slot):
        c0 = pl.multiple_of(f * tf, tf)
        return (pltpu.make_async_copy(wg_hbm.at[:, pl.ds(c0, tf)], wg_buf.at[slot], w_sems.at[0, slot]),
                pltpu.make_async_copy(wu_hbm.at[:, pl.ds(c0, tf)], wu_buf.at[slot], w_sems.at[1, slot]),
                pltpu.make_async_copy(wd_hbm.at[pl.ds(c0, tf), :], wd_buf.at[slot], w_sems.at[2, slot]))

    for cp in weight_copies(0, 0):
        cp.start()
    _load_rows_scaled(x_hbm, row0, h_ref, xb_ref, sems_in, alpha, rows)

    def step(f, carry):
        slot = f % 2

        @pl.when(f + 1 < n_f)
        def _():
            for cp in weight_copies(f + 1, 1 - slot):
                cp.start()

        for cp in weight_copies(f, slot):
            cp.wait()
        xb = xb_ref[...]
        gate = jnp.dot(xb, wg_buf[slot].astype(BF16), preferred_element_type=F32)
        up = jnp.dot(xb, wu_buf[slot].astype(BF16), preferred_element_type=F32)
        act = (0.5 * (gate * jax.nn.sigmoid(gate)) * up).astype(BF16)
        wd = wd_buf[slot].astype(BF16)
        for c in range(d // col_chunk):
            sl = slice(c * col_chunk, (c + 1) * col_chunk)
            h_ref[:, sl] += jnp.dot(act, wd[:, sl], preferred_element_type=F32)
        return carry

    lax.fori_loop(0, n_f, step, 0)
    _ln_epilogue(h_ref, xb_ref, g_ref, b_ref, rows, h_out, hb_out, row0, sem_h, sem_hb)


def _ffn_ln(x, wg, wu, wd, g, b, alpha):
    m, d = x.shape
    f = wg.shape[1]
    tm = _tile(m, 1024)
    tf = _tile(f, 256)
    n_f = f // tf
    kern = functools.partial(_ffn_kernel, alpha=alpha, n_f=n_f, tf=tf,
                             rows=_tile(tm, LN_ROWS_PER_STEP), col_chunk=_tile(d, 512))
    hbm = pl.BlockSpec(memory_space=pl.ANY)
    return pl.pallas_call(
        kern,
        grid=(m // tm,),
        in_specs=[hbm, hbm, hbm, hbm,
                  pl.BlockSpec((1, d), lambda i: (0, 0)), pl.BlockSpec((1, d), lambda i: (0, 0))],
        out_specs=[hbm, hbm],
        out_shape=[jax.ShapeDtypeStruct((m, d), F32), jax.ShapeDtypeStruct((m, d), BF16)],
        scratch_shapes=_ln_tile_scratch(tm, d) + [
            pltpu.VMEM((2, d, tf), F32), pltpu.VMEM((2, d, tf), F32), pltpu.VMEM((2, tf, d), F32),
            pltpu.SemaphoreType.DMA((3, 2))],
        compiler_params=_params(("parallel",)),
        name="ffn_ln",
    )(x, wg, wu, wd, g, b)


def _proj_kernel(offs_ref, scales_ref, x_ref, wt_ref, o_ref):
    del offs_ref
    y = lax.dot_general(x_ref[...], wt_ref[...].astype(BF16), (((1,), (1,)), ((), ())),
                        preferred_element_type=F32)
    o_ref[...] = (y * scales_ref[pl.program_id(1)]).astype(o_ref.dtype)


def _proj(xb, w_t, col_ranges):
    m, d = xb.shape
    tm = _tile(m, 1024)
    tn = _tile(math.gcd(*[b - a for a, b, _ in col_ranges]), 512)
    tiles = [(a + k * tn, sc) for a, b, sc in col_ranges for k in range((b - a) // tn)]
    assert all(o % 8 == 0 for o, _ in tiles)
    grid_spec = pltpu.PrefetchScalarGridSpec(
        num_scalar_prefetch=2,
        grid=(m // tm, len(tiles)),
        in_specs=[
            pl.BlockSpec((tm, d), lambda i, j, offs, scales: (i, 0)),
            pl.BlockSpec((pl.Element(tn), pl.Element(d)),
                         lambda i, j, offs, scales: (pl.multiple_of(offs[j], 8), 0)),
        ],
        out_specs=pl.BlockSpec((tm, tn), lambda i, j, offs, scales: (i, j)),
    )
    return pl.pallas_call(
        _proj_kernel,
        grid_spec=grid_spec,
        out_shape=jax.ShapeDtypeStruct((m, len(tiles) * tn), BF16),
        compiler_params=_params(("parallel", "arbitrary")),
        name="proj_main",
    )(jnp.asarray([o for o, _ in tiles], jnp.int32), jnp.asarray([sc for _, sc in tiles], F32),
      xb, w_t)


def _split3(v):
    hi = v.astype(BF16)
    r1 = v - hi.astype(F32)
    mid = r1.astype(BF16)
    lo = (r1 - mid.astype(F32)).astype(BF16)
    return hi, mid, lo


def _proj_small_kernel(x_ref, wc_ref, wk_ref, wft_ref, bf_ref, kvg_ref, kg_ref, kb_ref,
                       ckv_ref, ki_ref, wi_ref, cum_ref, carry_ref, *, n_ih, tiles_per_seq):
    i = pl.program_id(0)
    x = x_ref[...]
    tm = x.shape[0]
    nt = (((1,), (1,)), ((), ()))
    c = lax.dot_general(x, wc_ref[...].astype(BF16), nt, preferred_element_type=F32)
    c = c * lax.rsqrt(jnp.mean(c * c, axis=-1, keepdims=True) + RMS_EPS) * kvg_ref[...]
    ckv_ref[...] = c.astype(BF16)
    kw = lax.dot_general(x, wk_ref[...].astype(BF16), nt, preferred_element_type=F32)
    ki_ref[...] = _layer_norm_rows(kw[:, :IDX_DIM], kg_ref[...], kb_ref[...]).astype(BF16)
    wi_ref[...] = kw[:, IDX_DIM:IDX_DIM + n_ih]

    ft = lax.dot_general(wft_ref[...].astype(BF16), x, nt, preferred_element_type=F32)
    z = ft + bf_ref[...]
    log_f = jnp.minimum(z, 0.0) - jnp.log1p(jnp.exp(-jnp.abs(z)))

    @pl.when(i % tiles_per_seq == 0)
    def _():
        carry_ref[...] = jnp.zeros_like(carry_ref)

    row = lax.broadcasted_iota(jnp.int32, (tm, tm), 0)
    col = lax.broadcasted_iota(jnp.int32, (tm, tm), 1)
    tri = jnp.where(row <= col, 1.0, 0.0).astype(BF16)
    hi, mid, lo = _split3(log_f)
    cs = (jnp.dot(hi, tri, preferred_element_type=F32)
          + jnp.dot(mid, tri, preferred_element_type=F32)
          + jnp.dot(lo, tri, preferred_element_type=F32))
    cum = cs + carry_ref[:, 0:1]
    cum_ref[...] = cum
    carry_ref[...] = jnp.broadcast_to(cum[:, tm - 1:tm], carry_ref.shape)


def _proj_small(xb, w_t, bf, kvg, kg, kb, *, o_ckv, o_ki, o_f, c_lat, n_ih, seq):
    m, d = xb.shape
    hf = bf.shape[0]
    tm = _tile(seq, 512)
    kern = functools.partial(_proj_small_kernel, n_ih=n_ih, tiles_per_seq=seq // tm)

    def rows(start, count):
        return pl.BlockSpec((pl.Element(count), pl.Element(d)), lambda i: (start, 0),
                            pipeline_mode=pl.Buffered(1))

    return pl.pallas_call(
        kern,
        grid=(m // tm,),
        in_specs=[
            pl.BlockSpec((tm, d), lambda i: (i, 0)),
            rows(o_ckv, c_lat),
            rows(o_ki, IDX_DIM + n_ih),
            rows(o_f, hf),
            pl.BlockSpec((hf, 1), lambda i: (0, 0)),
            pl.BlockSpec((1, c_lat), lambda i: (0, 0)),
            pl.BlockSpec((1, IDX_DIM), lambda i: (0, 0)),
            pl.BlockSpec((1, IDX_DIM), lambda i: (0, 0)),
        ],
        out_specs=[
            pl.BlockSpec((tm, c_lat), lambda i: (i, 0)),
            pl.BlockSpec((tm, IDX_DIM), lambda i: (i, 0)),
            pl.BlockSpec((tm, n_ih), lambda i: (i, 0)),
            pl.BlockSpec((hf, tm), lambda i: (0, i)),
        ],
        out_shape=[
            jax.ShapeDtypeStruct((m, c_lat), BF16),
            jax.ShapeDtypeStruct((m, IDX_DIM), BF16),
            jax.ShapeDtypeStruct((m, n_ih), F32),
            jax.ShapeDtypeStruct((hf, m), F32),
        ],
        scratch_shapes=[pltpu.VMEM((hf, LANES), F32)],
        compiler_params=_params(("arbitrary",)),
        name="proj_small",
    )(xb, w_t, w_t, w_t, bf, kvg, kg, kb)


def _dsa_kernel(qi_ref, qa_ref, wi_ref, ki_ref, ckv_ref, wuk_ref, wuv_ref, tab_ref, bfar_ref,
                o_ref, score_ref, score_t_ref, qlat_ref, acc_ref, m_ref, l_ref, p_ref, alpha_ref, *,
                n_ih, n_h, k_sel, tk, hpg):
    i = pl.program_id(1)
    start = i * Q_BLOCK
    qb = Q_BLOCK
    bpc = tk // LANES
    scale = HEAD_DIM ** -0.5
    w_fold = (IDX_DIM ** -0.5) * (n_ih ** -0.5)

    wcol = wi_ref[...] * w_fold
    row = lax.broadcasted_iota(jnp.int32, (qb, tk), 0)
    col = lax.broadcasted_iota(jnp.int32, (qb, tk), 1)
    limit = start + jnp.where(row < CHUNK, CHUNK, 2 * CHUNK)
    n_chunks = (i + bpc) // bpc

    def score_chunk(c, carry):
        k0 = pl.multiple_of(c * tk, tk)
        kblk = ki_ref[pl.ds(k0, tk), :]
        sc = jnp.zeros((qb, tk), F32)
        for h in range(n_ih):
            s = lax.dot_general(qi_ref[:, h * IDX_DIM:(h + 1) * IDX_DIM], kblk,
                                (((1,), (1,)), ((), ())), preferred_element_type=F32)
            sc = sc + jnp.maximum(s, 0.0) * wcol[:, h:h + 1]
        sc = jnp.where(col + k0 < limit, sc, -jnp.inf)
        for j in range(bpc):
            blk = sc[:, j * LANES:(j + 1) * LANES]
            score_ref[c * bpc + j] = blk
            score_t_ref[c * bpc + j] = blk.T
        return carry

    lax.fori_loop(0, n_chunks, score_chunk, 0)

    def code_to_float(code):
        bits = jnp.where(code >= 0, code, code ^ 0x7FFFFFFF)
        return lax.bitcast_convert_type(bits, F32)

    def bit_body(bi, code):
        cand = code + lax.shift_left(jnp.int32(1), 31 - bi)
        cf = jnp.concatenate([code_to_float(cand)] * (LANES // 8), axis=0)

        def count_chunk(c, cnt):
            for j in range(bpc):
                hit = jnp.where(score_t_ref[c * bpc + j] >= cf, 1.0, 0.0)
                cnt = cnt + jnp.sum(hit.reshape(LANES // 8, 8, qb), axis=0)
            return cnt

        cnt = lax.fori_loop(0, n_chunks, count_chunk, jnp.zeros((8, qb), F32))
        total = jnp.sum(cnt, axis=0, keepdims=True)
        return jnp.where(total >= k_sel, cand, code)

    code = lax.fori_loop(0, 32, bit_body, jnp.full((8, qb), INT_MIN, jnp.int32))
    thr_row = jnp.where(code <= CODE_NEG_INF, F32_LOWEST, code_to_float(code))
    thr = jnp.concatenate([thr_row] * (qb // 8), axis=0).T

    for h in range(n_h):
        ql = jnp.dot(qa_ref[:, h * HEAD_DIM:(h + 1) * HEAD_DIM], wuk_ref[h],
                     preferred_element_type=F32)
        qlat_ref[h * qb:(h + 1) * qb, :] = (ql * (scale * LOG2E)).astype(BF16)

    def flash_step(k0, width, sel, bias_of_head, first):
        cblk = ckv_ref[pl.ds(k0, width), :]
        s_of = {}
        for g in range(n_h // hpg):
            sg = lax.dot_general(qlat_ref[g * hpg * qb:(g + 1) * hpg * qb, :], cblk,
                                 (((1,), (1,)), ((), ())), preferred_element_type=F32)
            for hh in range(hpg):
                s_of[g * hpg + hh] = sg[hh * qb:(hh + 1) * qb, :]
        for g in range(n_h // hpg):
            rg = slice(g * hpg * qb, (g + 1) * hpg * qb)
            heads = [g * hpg + hh for hh in range(hpg)]
            sh, m_new = {}, {}
            for h in heads:
                rs = slice(h * qb, (h + 1) * qb)
                sh[h] = jnp.where(sel, s_of[h] + bias_of_head(h), NEG)
                row_max = jnp.broadcast_to(jnp.max(sh[h], axis=-1, keepdims=True), (qb, LANES))
                m_new[h] = row_max if first else jnp.maximum(m_ref[rs, :], row_max)
            for h in heads:
                rs = slice(h * qb, (h + 1) * qb)
                p = jnp.exp2(sh[h] - _widen(m_new[h], width))
                row_sum = jnp.broadcast_to(jnp.sum(p, axis=-1, keepdims=True), (qb, LANES))
                if first:
                    l_ref[rs, :] = row_sum
                else:
                    a = jnp.exp2(m_ref[rs, :] - m_new[h])
                    l_ref[rs, :] = a * l_ref[rs, :] + row_sum
                    alpha_ref[rs, :] = a
                m_ref[rs, :] = m_new[h]
                p_ref[rs, 0:width] = p.astype(BF16)
            pv = jnp.dot(p_ref[rg, 0:width], cblk, preferred_element_type=F32)
            if first:
                acc_ref[rg, :] = pv
            else:
                acc_ref[rg, :] = acc_ref[rg, :] * _widen(alpha_ref[rg, :], pv.shape[1]) + pv

    wb = jnp.maximum(i - 1, 0)
    tsel = jnp.where(i == 0, 1, 0)
    scw = jnp.concatenate([score_ref[wb], score_ref[wb + 1]], axis=-1)
    selw = scw >= _widen(thr, 2 * LANES)
    flash_step(pl.multiple_of(wb * LANES, LANES), 2 * LANES, selw, lambda h: tab_ref[tsel, h],
               first=True)

    far_end = (i - 1) * qb

    def far_chunk(c, carry):
        k0 = pl.multiple_of(c * tk, tk)
        sc = jnp.concatenate([score_ref[c * bpc + j] for j in range(bpc)], axis=-1)
        sel = (sc >= _widen(thr, tk)) & (col + k0 < far_end)
        flash_step(k0, tk, sel, lambda h: bfar_ref[h], first=False)
        return carry

    n_far = jnp.maximum((i - 1 + bpc - 1) // bpc, 0)
    lax.fori_loop(0, n_far, far_chunk, 0)

    for h in range(n_h):
        rs = slice(h * qb, (h + 1) * qb)
        o_lat = (acc_ref[rs, :] / _widen(l_ref[rs, :], acc_ref.shape[1])).astype(BF16)
        o_ref[:, h * HEAD_DIM:(h + 1) * HEAD_DIM] = jnp.dot(
            o_lat, wuv_ref[h], preferred_element_type=F32).astype(o_ref.dtype)


def _dsa(qkv, wi, ki, ckv, wuk, wuv, tab, bfar, *, batch, seq, n_ih, n_h, qi_blk, qa_blk):
    m = qkv.shape[0]
    c_lat = ckv.shape[1]
    nq = seq // Q_BLOCK
    k_sel = min(TOPK_MAX, seq // 4)
    tk = _tile(seq, 512)
    rows = n_h * Q_BLOCK
    kern = functools.partial(_dsa_kernel, n_ih=n_ih, n_h=n_h, k_sel=k_sel, tk=tk,
                             hpg=math.gcd(n_h, DSA_HEADS_PER_GROUP))
    const = dict(pipeline_mode=pl.Buffered(1))
    return pl.pallas_call(
        kern,
        grid=(batch, nq),
        in_specs=[
            pl.BlockSpec((Q_BLOCK, n_ih * IDX_DIM), lambda b, i: (b * nq + i, qi_blk)),
            pl.BlockSpec((Q_BLOCK, n_h * HEAD_DIM), lambda b, i: (b * nq + i, qa_blk)),
            pl.BlockSpec((Q_BLOCK, n_ih), lambda b, i: (b * nq + i, 0)),
            pl.BlockSpec((seq, IDX_DIM), lambda b, i: (b, 0)),
            pl.BlockSpec((seq, c_lat), lambda b, i: (b, 0)),
            pl.BlockSpec((n_h, HEAD_DIM, c_lat), lambda b, i: (0, 0, 0), **const),
            pl.BlockSpec((n_h, c_lat, HEAD_DIM), lambda b, i: (0, 0, 0), **const),
            pl.BlockSpec((2, n_h, Q_BLOCK, 2 * LANES), lambda b, i: (0, 0, 0, 0), **const),
            pl.BlockSpec(memory_space=pltpu.SMEM),
        ],
        out_specs=pl.BlockSpec((Q_BLOCK, n_h * HEAD_DIM), lambda b, i: (b * nq + i, 0)),
        out_shape=jax.ShapeDtypeStruct((m, n_h * HEAD_DIM), BF16),
        scratch_shapes=[
            pltpu.VMEM((seq // LANES, Q_BLOCK, LANES), F32),
            pltpu.VMEM((seq // LANES, LANES, Q_BLOCK), F32),
            pltpu.VMEM((rows, c_lat), BF16),
            pltpu.VMEM((rows, c_lat), F32),
            pltpu.VMEM((rows, LANES), F32),
            pltpu.VMEM((rows, LANES), F32),
            pltpu.VMEM((rows, tk), BF16),
            pltpu.VMEM((rows, LANES), F32),
        ],
        compiler_params=_params(("parallel", "arbitrary")),
        name="dsa_mixer",
    )(qkv, qkv, wi, ki, ckv, wuk, wuv, tab, bfar)


def _fox_kernel(q_ref, k_ref, v_ref, cq_ref, ck_ref, o_ref, acc_ref, m_ref, l_ref, *, tq, hpb):
    hp = pl.program_id(1)
    i = pl.program_id(2)
    lane = lax.broadcasted_iota(jnp.int32, cq_ref.shape, 1)
    cq_all = cq_ref[...] * LOG2E
    cq = [jnp.sum(jnp.where(lane == hp * hpb + hh, cq_all, 0.0), axis=-1, keepdims=True)
          for hh in range(hpb)]

    def step(kb, first):
        causal = first
        k0 = pl.multiple_of(kb * tq, tq)
        logits = []
        for hh in range(hpb):
            ls = slice(hh * HEAD_DIM, (hh + 1) * HEAD_DIM)
            logits.append(lax.dot_general(q_ref[:, ls], k_ref[pl.ds(k0, tq), ls],
                                          (((1,), (1,)), ((), ())), preferred_element_type=F32))
        sh, m_new = [], []
        for hh in range(hpb):
            ck = ck_ref[pl.ds(hp * hpb + hh, 1), pl.ds(kb, 1), :].reshape(1, tq) * LOG2E
            s = logits[hh] + (cq[hh] - ck)
            if causal:
                row = lax.broadcasted_iota(jnp.int32, (tq, tq), 0)
                col = lax.broadcasted_iota(jnp.int32, (tq, tq), 1)
                s = jnp.where(col <= row, s, NEG)
            sh.append(s)
            row_max = jnp.broadcast_to(jnp.max(s, axis=-1, keepdims=True), (tq, LANES))
            m_new.append(row_max if first else jnp.maximum(m_ref[hh], row_max))
        for hh in range(hpb):
            ls = slice(hh * HEAD_DIM, (hh + 1) * HEAD_DIM)
            p = jnp.exp2(sh[hh] - _widen(m_new[hh], tq))
            row_sum = jnp.broadcast_to(jnp.sum(p, axis=-1, keepdims=True), (tq, LANES))
            pv = jnp.dot(p.astype(BF16), v_ref[pl.ds(k0, tq), ls], preferred_element_type=F32)
            if first:
                l_ref[hh] = row_sum
                acc_ref[hh] = pv
            else:
                a = jnp.exp2(m_ref[hh] - m_new[hh])
                l_ref[hh] = a * l_ref[hh] + row_sum
                acc_ref[hh] = acc_ref[hh] * a + pv
            m_ref[hh] = m_new[hh]

    def full_block(kb, carry):
        step(kb, False)
        return carry

    step(i, True)
    lax.fori_loop(0, i, full_block, 0)
    for hh in range(hpb):
        o_ref[:, hh * HEAD_DIM:(hh + 1) * HEAD_DIM] = (acc_ref[hh] / l_ref[hh]).astype(o_ref.dtype)


def _fox(qkv, cum_tok, cum_blk, *, batch, seq, n_h, q_blk0, k_blk0, v_blk0):
    m = qkv.shape[0]
    tq = _tile(seq, 512)
    nq = seq // tq
    hpb = math.gcd(n_h, FOX_HEADS_PER_BLOCK)
    assert q_blk0 % hpb == 0 and k_blk0 % hpb == 0 and v_blk0 % hpb == 0
    wb = hpb * HEAD_DIM
    kern = functools.partial(_fox_kernel, tq=tq, hpb=hpb)
    return pl.pallas_call(
        kern,
        grid=(batch, n_h // hpb, nq),
        in_specs=[
            pl.BlockSpec((tq, wb), lambda b, h, i: (b * nq + i, q_blk0 // hpb + h)),
            pl.BlockSpec((seq, wb), lambda b, h, i: (b, k_blk0 // hpb + h)),
            pl.BlockSpec((seq, wb), lambda b, h, i: (b, v_blk0 // hpb + h)),
            pl.BlockSpec((tq, n_h), lambda b, h, i: (b * nq + i, 0)),
            pl.BlockSpec((n_h, nq, tq), lambda b, h, i: (0, b, 0)),
        ],
        out_specs=pl.BlockSpec((tq, wb), lambda b, h, i: (b * nq + i, h)),
        out_shape=jax.ShapeDtypeStruct((m, n_h * HEAD_DIM), BF16),
        scratch_shapes=[
            pltpu.VMEM((hpb, tq, HEAD_DIM), F32),
            pltpu.VMEM((hpb, tq, LANES), F32),
            pltpu.VMEM((hpb, tq, LANES), F32),
        ],
        compiler_params=_params(("parallel", "parallel", "arbitrary")),
        name="fox_mixer",
    )(qkv, qkv, qkv, cum_tok, cum_blk)


def _outproj_kernel(oa_ref, ob_ref, w_ref, res_hbm, g_ref, b_ref, h_out, hb_out,
                    h_ref, hb_ref, sems_in, sem_h, sem_hb, *, alpha, n_ka, n_k, rows, col_chunk):
    i = pl.program_id(0)
    k = pl.program_id(1)
    tm, d = h_ref.shape
    row0 = pl.multiple_of(i * tm, tm)

    @pl.when(k == 0)
    def _():
        _load_rows_scaled(res_hbm, row0, h_ref, None, sems_in, alpha, rows)

    def accumulate(src_ref):
        src = src_ref[...]
        for c in range(d // col_chunk):
            sl = slice(c * col_chunk, (c + 1) * col_chunk)
            h_ref[:, sl] += jnp.dot(src, w_ref[:, sl].astype(BF16), preferred_element_type=F32)

    @pl.when(k < n_ka)
    def _():
        accumulate(oa_ref)

    @pl.when(k >= n_ka)
    def _():
        accumulate(ob_ref)

    @pl.when(k == n_k - 1)
    def _():
        _ln_epilogue(h_ref, hb_ref, g_ref, b_ref, rows, h_out, hb_out, row0, sem_h, sem_hb)


def _outproj_ln(oa, ob, w, res, g, b, alpha):
    m, wa = oa.shape
    wbw = ob.shape[1]
    d = w.shape[1]
    tm = _tile(m, 1024)
    tk = _tile(math.gcd(wa, wbw), 512)
    n_ka, n_kb = wa // tk, wbw // tk
    n_k = n_ka + n_kb
    kern = functools.partial(_outproj_kernel, alpha=alpha, n_ka=n_ka, n_k=n_k,
                             rows=_tile(tm, LN_ROWS_PER_STEP), col_chunk=_tile(d, 512))
    hbm = pl.BlockSpec(memory_space=pl.ANY)
    return pl.pallas_call(
        kern,
        grid=(m // tm, n_k),
        in_specs=[
            pl.BlockSpec((tm, tk), lambda i, k: (i, jnp.minimum(k, n_ka - 1))),
            pl.BlockSpec((tm, tk), lambda i, k: (i, jnp.maximum(k - n_ka, 0))),
            pl.BlockSpec((tk, d), lambda i, k: (k, 0)),
            hbm,
            pl.BlockSpec((1, d), lambda i, k: (0, 0)),
            pl.BlockSpec((1, d), lambda i, k: (0, 0)),
        ],
        out_specs=[hbm, hbm],
        out_shape=[jax.ShapeDtypeStruct((m, d), F32), jax.ShapeDtypeStruct((m, d), BF16)],
        scratch_shapes=_ln_tile_scratch(tm, d),
        compiler_params=_params(("parallel", "arbitrary")),
        name="outproj_ln",
    )(oa, ob, w, res, g, b)


def _t5_bucket(rel):
    nb = N_BUCKETS // 2
    max_exact = nb // 2
    ret = jnp.where(rel > 0, nb, 0)
    n = jnp.abs(rel)
    nf = jnp.maximum(n, 1).astype(jnp.float32)
    large = max_exact + (jnp.log(nf / max_exact) / math.log(MAX_DISTANCE / max_exact)
                         * (nb - max_exact)).astype(jnp.int32)
    large = jnp.minimum(large, nb - 1)
    return ret + jnp.where(n < max_exact, n, large)


def _bias_tables(rel_bias):
    n_h = rel_bias.shape[1]
    width = 2 * LANES
    period = 2 * width
    m = jnp.arange(period, dtype=jnp.int32)
    m = jnp.where(m >= width, m - period, m)
    rel = jnp.stack([m - LANES, m])
    u = jnp.moveaxis(rel_bias[_t5_bucket(rel)], -1, 1).astype(F32)
    flat = jnp.tile(u, (1, 1, Q_BLOCK))[..., :Q_BLOCK * (period - 1)]
    tab = flat.reshape(2, n_h, Q_BLOCK, period - 1)[..., :width]
    far = rel_bias[_t5_bucket(jnp.full((), -(MAX_DISTANCE + 1), jnp.int32))].astype(F32)
    return tab, far


def _hybrid_mixer_ln(h, hb, w_in, b_f, kv_norm_g, idx_k_g, idx_k_b, w_uk, w_uv, rel_bias, w_out,
                     ln_g, ln_b, alpha, batch, seq):
    m, d = h.shape
    n_h = w_uk.shape[0]
    c_lat = w_uk.shape[2]
    n_hf = b_f.shape[0]
    w_dsa = n_h * HEAD_DIM
    w_fox = n_hf * HEAD_DIM
    d_in = w_in.shape[1]
    n_ih = (d_in - w_dsa - c_lat - IDX_DIM - 3 * w_fox - n_hf) // (IDX_DIM + 1)
    w_qi = n_ih * IDX_DIM
    assert w_dsa + c_lat + w_qi + IDX_DIM + n_ih + 3 * w_fox + n_hf == d_in
    assert w_qi % w_dsa == 0 and seq % (2 * LANES) == 0

    o_ckv = w_dsa
    o_qi = o_ckv + c_lat
    o_ki = o_qi + w_qi
    o_fox = o_ki + IDX_DIM + n_ih
    o_f = o_fox + 3 * w_fox

    w_t = jnp.swapaxes(w_in, 0, 1)
    fox_q_scale = (HEAD_DIM ** -0.5) * LOG2E
    qkv = _proj(hb, w_t, ((o_qi, o_ki, 1.0), (0, w_dsa, 1.0),
                          (o_fox, o_fox + w_fox, fox_q_scale), (o_fox + w_fox, o_f, 1.0)))
    ckv, ki, wi, cum_t = _proj_small(
        hb, w_t, b_f.reshape(n_hf, 1).astype(F32),
        kv_norm_g.reshape(1, c_lat), idx_k_g.reshape(1, IDX_DIM), idx_k_b.reshape(1, IDX_DIM),
        o_ckv=o_ckv, o_ki=o_ki, o_f=o_f, c_lat=c_lat, n_ih=n_ih, seq=seq)

    tab, far = _bias_tables(rel_bias)
    tab, far = tab * LOG2E, far * LOG2E
    o_a = _dsa(qkv, wi, ki, ckv, w_uk.astype(BF16), w_uv.astype(BF16), tab, far,
               batch=batch, seq=seq, n_ih=n_ih, n_h=n_h, qi_blk=0, qa_blk=w_qi // w_dsa)

    tq = _tile(seq, 512)
    blk0 = (w_qi + w_dsa) // HEAD_DIM
    o_b = _fox(qkv, cum_t.T, cum_t.reshape(n_hf, m // tq, tq), batch=batch, seq=seq, n_h=n_hf,
               q_blk0=blk0, k_blk0=blk0 + n_hf, v_blk0=blk0 + 2 * n_hf)

    return _outproj_ln(o_a, o_b, w_out, h, ln_g.reshape(1, d), ln_b.reshape(1, d), alpha)


def kernel(x, ffn1_w_gate, ffn1_w_up, ffn1_w_down, ln1_g, ln1_b, w_in, b_f, kv_norm_g, idx_k_g,
           idx_k_b, w_uk, w_uv, rel_bias, w_out, ln2_g, ln2_b, ffn2_w_gate, ffn2_w_up,
           ffn2_w_down, ln3_g, ln3_b):
    batch, seq, d = x.shape
    depth = ffn1_w_gate.shape[0]
    alpha = (2.0 * depth) ** 0.25
    h = x.reshape(batch * seq, d)
    for l in range(depth):
        h, hb = _ffn_ln(h, ffn1_w_gate[l], ffn1_w_up[l],
                        ffn1_w_down[l], ln1_g[l].reshape(1, d), ln1_b[l].reshape(1, d),
                        alpha)
        h, hb = _hybrid_mixer_ln(h, hb, w_in[l], b_f[l], kv_norm_g[l], idx_k_g[l], idx_k_b[l],
                                 w_uk[l], w_uv[l], rel_bias, w_out[l], ln2_g[l], ln2_b[l], alpha,
                                 batch, seq)
        h, _ = _ffn_ln(h, ffn2_w_gate[l], ffn2_w_up[l],
                       ffn2_w_down[l], ln3_g[l].reshape(1, d), ln3_b[l].reshape(1, d),
                       alpha)
    return h.reshape(batch, seq, d)
```

```python
import functools
import math

import jax
import jax.numpy as jnp
import numpy as np
from jax import lax
from jax.experimental import pallas as pl
from jax.experimental.pallas import tpu as pltpu

CHUNK = 64
Q_BLOCK = 128
HEAD_DIM = 128
IDX_DIM = 128
TOPK_MAX = 256
N_BUCKETS = 32
MAX_DISTANCE = 128
LN_EPS = 1e-5
RMS_EPS = 1e-6
NEG = -1e30
LOG2E = math.log2(math.e)
DSA_HEADS_PER_GROUP = 4
FOX_HEADS_PER_BLOCK = 4
LN_ROWS_PER_STEP = 128
LN_ROW_GROUPS = 8
TILE_LOAD_CHUNKS = 8

LANES = 128
VMEM_LIMIT_BYTES = 56 * 1024 * 1024

BF16 = jnp.bfloat16
F32 = jnp.float32

INT_MIN = -(2 ** 31)
CODE_NEG_INF = -(2 ** 31) + 0x7FFFFF
F32_LOWEST = float(np.finfo(np.float32).min)


def _tile(dim, pref):
    t = min(dim, pref)
    while dim % t:
        t //= 2
    return t


def _params(sem):
    return pltpu.CompilerParams(dimension_semantics=sem, vmem_limit_bytes=VMEM_LIMIT_BYTES)


def _widen(x, width):
    reps = width // x.shape[1]
    return x if reps == 1 else jnp.concatenate([x] * reps, axis=-1)


def _layer_norm_rows(y, g, b):
    mu = jnp.mean(y, axis=-1, keepdims=True)
    d = y - mu
    var = jnp.mean(d * d, axis=-1, keepdims=True)
    return d * lax.rsqrt(var + LN_EPS) * g + b


def _load_rows_scaled(src_hbm, row0, acc_ref, xb_ref, sems, alpha, rows):
    chunk = acc_ref.shape[0] // TILE_LOAD_CHUNKS

    def copy(c):
        return pltpu.make_async_copy(src_hbm.at[pl.ds(row0 + c * chunk, chunk), :],
                                     acc_ref.at[pl.ds(c * chunk, chunk), :], sems.at[c])

    for c in range(TILE_LOAD_CHUNKS):
        copy(c).start()
    for c in range(TILE_LOAD_CHUNKS):
        copy(c).wait()

        def body(r, carry, c=c):
            r0 = pl.multiple_of(c * chunk + r * rows, rows)
            xr = acc_ref[pl.ds(r0, rows), :]
            if xb_ref is not None:
                xb_ref[pl.ds(r0, rows), :] = xr.astype(BF16)
            acc_ref[pl.ds(r0, rows), :] = alpha * xr
            return carry

        lax.fori_loop(0, chunk // rows, body, 0)


def _ln_epilogue(h_ref, hb_ref, g_ref, b_ref, rows, h_out, hb_out, row0, sem_h, sem_hb):
    g = g_ref[...]
    b = b_ref[...]
    sub = rows // LN_ROW_GROUPS
    n_steps = h_ref.shape[0] // rows

    def out_copies(r0):
        return (pltpu.make_async_copy(h_ref.at[pl.ds(r0, rows), :],
                                      h_out.at[pl.ds(row0 + r0, rows), :], sem_h),
                pltpu.make_async_copy(hb_ref.at[pl.ds(r0, rows), :],
                                      hb_out.at[pl.ds(row0 + r0, rows), :], sem_hb))

    def body(r, carry):
        r0 = pl.multiple_of(r * rows, rows)
        sls = [pl.ds(r0 + k * sub, sub) for k in range(LN_ROW_GROUPS)]
        mus = [jnp.mean(h_ref[sl, :], axis=-1, keepdims=True) for sl in sls]
        rstds = []
        for sl, mu in zip(sls, mus):
            d = h_ref[sl, :] - mu
            rstds.append(lax.rsqrt(jnp.mean(d * d, axis=-1, keepdims=True) + LN_EPS))
        for sl, mu, rstd in zip(sls, mus, rstds):
            out = (h_ref[sl, :] - mu) * rstd * g + b
            h_ref[sl, :] = out
            hb_ref[sl, :] = out.astype(BF16)
        for cp in out_copies(r0):
            cp.start()
        return carry

    lax.fori_loop(0, n_steps, body, 0)

    def drain(r, carry):
        for cp in out_copies(pl.multiple_of(r * rows, rows)):
            cp.wait()
        return carry

    lax.fori_loop(0, n_steps, drain, 0)


def _ln_tile_scratch(tm, d):
    return [pltpu.VMEM((tm, d), F32), pltpu.VMEM((tm, d), BF16),
            pltpu.SemaphoreType.DMA((TILE_LOAD_CHUNKS,)),
            pltpu.SemaphoreType.DMA(()), pltpu.SemaphoreType.DMA(())]


def _ffn_kernel(x_hbm, wg_hbm, wu_hbm, wd_hbm, g_ref, b_ref, h_out, hb_out,
                h_ref, xb_ref, sems_in, sem_h, sem_hb, wg_buf, wu_buf, wd_buf, w_sems, *,
                alpha, n_f, tf, rows, col_chunk):
    i = pl.program_id(0)
    tm, d = h_ref.shape
    row0 = pl.multiple_of(i * tm, tm)

    def weight_copies(f, slot):
        c0 = pl.multiple_of(f * tf, tf)
        return (pltpu.make_async_copy(wg_hbm.at[:, pl.ds(c0, tf)], wg_buf.at[slot], w_sems.at[0, slot]),
                pltpu.make_async_copy(wu_hbm.at[:, pl.ds(c0, tf)], wu_buf.at[slot], w_sems.at[1, slot]),
                pltpu.make_async_copy(wd_hbm.at[pl.ds(c0, tf), :], wd_buf.at[slot], w_sems.at[2, slot]))

    @pl.when(i == 0)
    def _():
        for cp in weight_copies(0, 0):
            cp.start()

    _load_rows_scaled(x_hbm, row0, h_ref, xb_ref, sems_in, alpha, rows)

    def step(f, carry):
        slot = (i * n_f + f) % 2

        @pl.when((f + 1 < n_f) | (i + 1 < pl.num_programs(0)))
        def _():
            nxt = jnp.where(f + 1 < n_f, f + 1, 0)
            for cp in weight_copies(nxt, 1 - slot):
                cp.start()

        for cp in weight_copies(f, slot):
            cp.wait()
        xb = xb_ref[...]
        gate = jnp.dot(xb, wg_buf[slot].astype(BF16), preferred_element_type=F32)
        up = jnp.dot(xb, wu_buf[slot].astype(BF16), preferred_element_type=F32)
        act = (0.5 * (gate * jax.nn.sigmoid(gate)) * up).astype(BF16)
        wd = wd_buf[slot].astype(BF16)
        for c in range(d // col_chunk):
            sl = slice(c * col_chunk, (c + 1) * col_chunk)
            h_ref[:, sl] += jnp.dot(act, wd[:, sl], preferred_element_type=F32)
        return carry

    lax.fori_loop(0, n_f, step, 0)
    _ln_epilogue(h_ref, xb_ref, g_ref, b_ref, rows, h_out, hb_out, row0, sem_h, sem_hb)


def _ffn_ln(x, wg, wu, wd, g, b, alpha):
    m, d = x.shape
    f = wg.shape[1]
    tm = _tile(m, 1024)
    tf = _tile(f, 256)
    n_f = f // tf
    kern = functools.partial(_ffn_kernel, alpha=alpha, n_f=n_f, tf=tf,
                             rows=_tile(tm, LN_ROWS_PER_STEP), col_chunk=_tile(d, 512))
    hbm = pl.BlockSpec(memory_space=pl.ANY)
    return pl.pallas_call(
        kern,
        grid=(m // tm,),
        in_specs=[hbm, hbm, hbm, hbm,
                  pl.BlockSpec((1, d), lambda i: (0, 0)), pl.BlockSpec((1, d), lambda i: (0, 0))],
        out_specs=[hbm, hbm],
        out_shape=[jax.ShapeDtypeStruct((m, d), F32), jax.ShapeDtypeStruct((m, d), BF16)],
        scratch_shapes=_ln_tile_scratch(tm, d) + [
            pltpu.VMEM((2, d, tf), F32), pltpu.VMEM((2, d, tf), F32), pltpu.VMEM((2, tf, d), F32),
            pltpu.SemaphoreType.DMA((3, 2))],
        compiler_params=_params(("arbitrary",)),
        name="ffn_ln",
    )(x, wg, wu, wd, g, b)


def _proj_kernel(offs_ref, scales_ref, x_ref, wt_ref, o_ref):
    del offs_ref
    y = lax.dot_general(x_ref[...], wt_ref[...].astype(BF16), (((1,), (1,)), ((), ())),
                        preferred_element_type=F32)
    o_ref[...] = (y * scales_ref[pl.program_id(1)]).astype(o_ref.dtype)


def _proj(xb, w_t, col_ranges):
    m, d = xb.shape
    tm = _tile(m, 1024)
    tn = _tile(math.gcd(*[b - a for a, b, _ in col_ranges]), 512)
    tiles = [(a + k * tn, sc) for a, b, sc in col_ranges for k in range((b - a) // tn)]
    assert all(o % 8 == 0 for o, _ in tiles)
    grid_spec = pltpu.PrefetchScalarGridSpec(
        num_scalar_prefetch=2,
        grid=(m // tm, len(tiles)),
        in_specs=[
            pl.BlockSpec((tm, d), lambda i, j, offs, scales: (i, 0)),
            pl.BlockSpec((pl.Element(tn), pl.Element(d)),
                         lambda i, j, offs, scales: (pl.multiple_of(offs[j], 8), 0)),
        ],
        out_specs=pl.BlockSpec((tm, tn), lambda i, j, offs, scales: (i, j)),
    )
    return pl.pallas_call(
        _proj_kernel,
        grid_spec=grid_spec,
        out_shape=jax.ShapeDtypeStruct((m, len(tiles) * tn), BF16),
        compiler_params=_params(("parallel", "arbitrary")),
        name="proj_main",
    )(jnp.asarray([o for o, _ in tiles], jnp.int32), jnp.asarray([sc for _, sc in tiles], F32),
      xb, w_t)


def _split3(v):
    hi = v.astype(BF16)
    r1 = v - hi.astype(F32)
    mid = r1.astype(BF16)
    lo = (r1 - mid.astype(F32)).astype(BF16)
    return hi, mid, lo


def _proj_small_kernel(x_ref, wc_ref, wk_ref, wft_ref, bf_ref, kvg_ref, kg_ref, kb_ref,
                       ckv_ref, ki_ref, wi_ref, cum_ref, carry_ref, *, n_ih, tiles_per_seq):
    i = pl.program_id(0)
    x = x_ref[...]
    tm = x.shape[0]
    nt = (((1,), (1,)), ((), ()))
    c = lax.dot_general(x, wc_ref[...].astype(BF16), nt, preferred_element_type=F32)
    c = c * lax.rsqrt(jnp.mean(c * c, axis=-1, keepdims=True) + RMS_EPS) * kvg_ref[...]
    ckv_ref[...] = c.astype(BF16)
    kw = lax.dot_general(x, wk_ref[...].astype(BF16), nt, preferred_element_type=F32)
    ki_ref[...] = _layer_norm_rows(kw[:, :IDX_DIM], kg_ref[...], kb_ref[...]).astype(BF16)
    wi_ref[...] = kw[:, IDX_DIM:IDX_DIM + n_ih]

    ft = lax.dot_general(wft_ref[...].astype(BF16), x, nt, preferred_element_type=F32)
    z = ft + bf_ref[...]
    log_f = jnp.minimum(z, 0.0) - jnp.log1p(jnp.exp(-jnp.abs(z)))

    @pl.when(i % tiles_per_seq == 0)
    def _():
        carry_ref[...] = jnp.zeros_like(carry_ref)

    row = lax.broadcasted_iota(jnp.int32, (tm, tm), 0)
    col = lax.broadcasted_iota(jnp.int32, (tm, tm), 1)
    tri = jnp.where(row <= col, 1.0, 0.0).astype(BF16)
    hi, mid, lo = _split3(log_f)
    cs = (jnp.dot(hi, tri, preferred_element_type=F32)
          + jnp.dot(mid, tri, preferred_element_type=F32)
          + jnp.dot(lo, tri, preferred_element_type=F32))
    cum = cs + carry_ref[:, 0:1]
    cum_ref[...] = cum
    carry_ref[...] = jnp.broadcast_to(cum[:, tm - 1:tm], carry_ref.shape)


def _proj_small(xb, w_t, bf, kvg, kg, kb, *, o_ckv, o_ki, o_f, c_lat, n_ih, seq):
    m, d = xb.shape
    hf = bf.shape[0]
    tm = _tile(seq, 512)
    kern = functools.partial(_proj_small_kernel, n_ih=n_ih, tiles_per_seq=seq // tm)

    def rows(start, count):
        return pl.BlockSpec((pl.Element(count), pl.Element(d)), lambda i: (start, 0),
                            pipeline_mode=pl.Buffered(1))

    return pl.pallas_call(
        kern,
        grid=(m // tm,),
        in_specs=[
            pl.BlockSpec((tm, d), lambda i: (i, 0)),
            rows(o_ckv, c_lat),
            rows(o_ki, IDX_DIM + n_ih),
            rows(o_f, hf),
            pl.BlockSpec((hf, 1), lambda i: (0, 0)),
            pl.BlockSpec((1, c_lat), lambda i: (0, 0)),
            pl.BlockSpec((1, IDX_DIM), lambda i: (0, 0)),
            pl.BlockSpec((1, IDX_DIM), lambda i: (0, 0)),
        ],
        out_specs=[
            pl.BlockSpec((tm, c_lat), lambda i: (i, 0)),
            pl.BlockSpec((tm, IDX_DIM), lambda i: (i, 0)),
            pl.BlockSpec((tm, n_ih), lambda i: (i, 0)),
            pl.BlockSpec((hf, tm), lambda i: (0, i)),
        ],
        out_shape=[
            jax.ShapeDtypeStruct((m, c_lat), BF16),
            jax.ShapeDtypeStruct((m, IDX_DIM), BF16),
            jax.ShapeDtypeStruct((m, n_ih), F32),
            jax.ShapeDtypeStruct((hf, m), F32),
        ],
        scratch_shapes=[pltpu.VMEM((hf, LANES), F32)],
        compiler_params=_params(("arbitrary",)),
        name="proj_small",
    )(xb, w_t, w_t, w_t, bf, kvg, kg, kb)


def _dsa_kernel(qi_ref, qa_ref, wi_ref, ki_ref, ckv_ref, wuk_ref, wuv_ref, tab_ref, bfar_ref,
                o_ref, score_ref, score_t_ref, qlat_ref, acc_ref, m_ref, l_ref, p_ref, alpha_ref, *,
                n_ih, n_h, k_sel, tk, hpg):
    i = pl.program_id(1)
    start = i * Q_BLOCK
    qb = Q_BLOCK
    bpc = tk // LANES
    scale = HEAD_DIM ** -0.5
    w_fold = (IDX_DIM ** -0.5) * (n_ih ** -0.5)

    wcol = wi_ref[...] * w_fold
    row = lax.broadcasted_iota(jnp.int32, (qb, tk), 0)
    col = lax.broadcasted_iota(jnp.int32, (qb, tk), 1)
    limit = start + jnp.where(row < CHUNK, CHUNK, 2 * CHUNK)
    n_chunks = (i + bpc) // bpc

    def score_chunk(c, carry):
        k0 = pl.multiple_of(c * tk, tk)
        kblk = ki_ref[pl.ds(k0, tk), :]
        sc = jnp.zeros((qb, tk), F32)
        for h in range(n_ih):
            s = lax.dot_general(qi_ref[:, h * IDX_DIM:(h + 1) * IDX_DIM], kblk,
                                (((1,), (1,)), ((), ())), preferred_element_type=F32)
            sc = sc + jnp.maximum(s, 0.0) * wcol[:, h:h + 1]
        sc = jnp.where(col + k0 < limit, sc, -jnp.inf)
        for j in range(bpc):
            blk = sc[:, j * LANES:(j + 1) * LANES]
            score_ref[c * bpc + j] = blk
            score_t_ref[c * bpc + j] = blk.T
        return carry

    lax.fori_loop(0, n_chunks, score_chunk, 0)

    def code_to_float(code):
        bits = jnp.where(code >= 0, code, code ^ 0x7FFFFFFF)
        return lax.bitcast_convert_type(bits, F32)

    def bit_body(bi, code):
        cand = code + lax.shift_left(jnp.int32(1), 31 - bi)
        cf = jnp.concatenate([code_to_float(cand)] * (LANES // 8), axis=0)

        def count_chunk(c, cnt):
            for j in range(bpc):
                hit = jnp.where(score_t_ref[c * bpc + j] >= cf, 1.0, 0.0)
                cnt = cnt + jnp.sum(hit.reshape(LANES // 8, 8, qb), axis=0)
            return cnt

        cnt = lax.fori_loop(0, n_chunks, count_chunk, jnp.zeros((8, qb), F32))
        total = jnp.sum(cnt, axis=0, keepdims=True)
        return jnp.where(total >= k_sel, cand, code)

    code = lax.fori_loop(0, 32, bit_body, jnp.full((8, qb), INT_MIN, jnp.int32))
    thr_row = jnp.where(code <= CODE_NEG_INF, F32_LOWEST, code_to_float(code))
    thr = jnp.concatenate([thr_row] * (qb // 8), axis=0).T

    for h in range(n_h):
        ql = jnp.dot(qa_ref[:, h * HEAD_DIM:(h + 1) * HEAD_DIM], wuk_ref[h],
                     preferred_element_type=F32)
        qlat_ref[h * qb:(h + 1) * qb, :] = (ql * (scale * LOG2E)).astype(BF16)

    def flash_step(k0, width, sel, bias_of_head, first):
        cblk = ckv_ref[pl.ds(k0, width), :]
        s_of = {}
        for g in range(n_h // hpg):
            sg = lax.dot_general(qlat_ref[g * hpg * qb:(g + 1) * hpg * qb, :], cblk,
                                 (((1,), (1,)), ((), ())), preferred_element_type=F32)
            for hh in range(hpg):
                s_of[g * hpg + hh] = sg[hh * qb:(hh + 1) * qb, :]
        for g in range(n_h // hpg):
            rg = slice(g * hpg * qb, (g + 1) * hpg * qb)
            heads = [g * hpg + hh for hh in range(hpg)]
            sh, m_new = {}, {}
            for h in heads:
                rs = slice(h * qb, (h + 1) * qb)
                sh[h] = jnp.where(sel, s_of[h] + bias_of_head(h), NEG)
                row_max = jnp.broadcast_to(jnp.max(sh[h], axis=-1, keepdims=True), (qb, LANES))
                m_new[h] = row_max if first else jnp.maximum(m_ref[rs, :], row_max)
            for h in heads:
                rs = slice(h * qb, (h + 1) * qb)
                p = jnp.exp2(sh[h] - _widen(m_new[h], width))
                row_sum = jnp.broadcast_to(jnp.sum(p, axis=-1, keepdims=True), (qb, LANES))
                if first:
                    l_ref[rs, :] = row_sum
                else:
                    a = jnp.exp2(m_ref[rs, :] - m_new[h])
                    l_ref[rs, :] = a * l_ref[rs, :] + row_sum
                    alpha_ref[rs, :] = a
                m_ref[rs, :] = m_new[h]
                p_ref[rs, 0:width] = p.astype(BF16)
            pv = jnp.dot(p_ref[rg, 0:width], cblk, preferred_element_type=F32)
            if first:
                acc_ref[rg, :] = pv
            else:
                acc_ref[rg, :] = acc_ref[rg, :] * _widen(alpha_ref[rg, :], pv.shape[1]) + pv

    wb = jnp.maximum(i - 1, 0)
    tsel = jnp.where(i == 0, 1, 0)
    scw = jnp.concatenate([score_ref[wb], score_ref[wb + 1]], axis=-1)
    selw = scw >= _widen(thr, 2 * LANES)
    flash_step(pl.multiple_of(wb * LANES, LANES), 2 * LANES, selw, lambda h: tab_ref[tsel, h],
               first=True)

    far_end = (i - 1) * qb

    def far_chunk(c, carry):
        k0 = pl.multiple_of(c * tk, tk)
        sc = jnp.concatenate([score_ref[c * bpc + j] for j in range(bpc)], axis=-1)
        sel = (sc >= _widen(thr, tk)) & (col + k0 < far_end)
        flash_step(k0, tk, sel, lambda h: bfar_ref[h], first=False)
        return carry

    n_far = jnp.maximum((i - 1 + bpc - 1) // bpc, 0)
    lax.fori_loop(0, n_far, far_chunk, 0)

    for h in range(n_h):
        rs = slice(h * qb, (h + 1) * qb)
        o_lat = (acc_ref[rs, :] / _widen(l_ref[rs, :], acc_ref.shape[1])).astype(BF16)
        o_ref[:, h * HEAD_DIM:(h + 1) * HEAD_DIM] = jnp.dot(
            o_lat, wuv_ref[h], preferred_element_type=F32).astype(o_ref.dtype)


def _dsa(qkv, wi, ki, ckv, wuk, wuv, tab, bfar, *, batch, seq, n_ih, n_h, qi_blk, qa_blk):
    m = qkv.shape[0]
    c_lat = ckv.shape[1]
    nq = seq // Q_BLOCK
    k_sel = min(TOPK_MAX, seq // 4)
    tk = _tile(seq, 512)
    rows = n_h * Q_BLOCK
    kern = functools.partial(_dsa_kernel, n_ih=n_ih, n_h=n_h, k_sel=k_sel, tk=tk,
                             hpg=math.gcd(n_h, DSA_HEADS_PER_GROUP))
    const = dict(pipeline_mode=pl.Buffered(1))
    return pl.pallas_call(
        kern,
        grid=(batch, nq),
        in_specs=[
            pl.BlockSpec((Q_BLOCK, n_ih * IDX_DIM), lambda b, i: (b * nq + i, qi_blk)),
            pl.BlockSpec((Q_BLOCK, n_h * HEAD_DIM), lambda b, i: (b * nq + i, qa_blk)),
            pl.BlockSpec((Q_BLOCK, n_ih), lambda b, i: (b * nq + i, 0)),
            pl.BlockSpec((seq, IDX_DIM), lambda b, i: (b, 0)),
            pl.BlockSpec((seq, c_lat), lambda b, i: (b, 0)),
            pl.BlockSpec((n_h, HEAD_DIM, c_lat), lambda b, i: (0, 0, 0), **const),
            pl.BlockSpec((n_h, c_lat, HEAD_DIM), lambda b, i: (0, 0, 0), **const),
            pl.BlockSpec((2, n_h, Q_BLOCK, 2 * LANES), lambda b, i: (0, 0, 0, 0), **const),
            pl.BlockSpec(memory_space=pltpu.SMEM),
        ],
        out_specs=pl.BlockSpec((Q_BLOCK, n_h * HEAD_DIM), lambda b, i: (b * nq + i, 0)),
        out_shape=jax.ShapeDtypeStruct((m, n_h * HEAD_DIM), BF16),
        scratch_shapes=[
            pltpu.VMEM((seq // LANES, Q_BLOCK, LANES), F32),
            pltpu.VMEM((seq // LANES, LANES, Q_BLOCK), F32),
            pltpu.VMEM((rows, c_lat), BF16),
            pltpu.VMEM((rows, c_lat), F32),
            pltpu.VMEM((rows, LANES), F32),
            pltpu.VMEM((rows, LANES), F32),
            pltpu.VMEM((rows, tk), BF16),
            pltpu.VMEM((rows, LANES), F32),
        ],
        compiler_params=_params(("parallel", "arbitrary")),
        name="dsa_mixer",
    )(qkv, qkv, wi, ki, ckv, wuk, wuv, tab, bfar)


def _fox_kernel(q_ref, k_ref, v_ref, cq_ref, ck_ref, o_ref, acc_ref, m_ref, l_ref, *, tq, hpb):
    hp = pl.program_id(1)
    i = pl.program_id(2)
    lane = lax.broadcasted_iota(jnp.int32, cq_ref.shape, 1)
    cq_all = cq_ref[...] * LOG2E
    cq = [jnp.sum(jnp.where(lane == hp * hpb + hh, cq_all, 0.0), axis=-1, keepdims=True)
          for hh in range(hpb)]

    def step(kb, first):
        causal = first
        k0 = pl.multiple_of(kb * tq, tq)
        logits = []
        for hh in range(hpb):
            ls = slice(hh * HEAD_DIM, (hh + 1) * HEAD_DIM)
            logits.append(lax.dot_general(q_ref[:, ls], k_ref[pl.ds(k0, tq), ls],
                                          (((1,), (1,)), ((), ())), preferred_element_type=F32))
        sh, m_new = [], []
        for hh in range(hpb):
            ck = ck_ref[pl.ds(hp * hpb + hh, 1), pl.ds(kb, 1), :].reshape(1, tq) * LOG2E
            s = logits[hh] + (cq[hh] - ck)
            if causal:
                row = lax.broadcasted_iota(jnp.int32, (tq, tq), 0)
                col = lax.broadcasted_iota(jnp.int32, (tq, tq), 1)
                s = jnp.where(col <= row, s, NEG)
            sh.append(s)
            row_max = jnp.broadcast_to(jnp.max(s, axis=-1, keepdims=True), (tq, LANES))
            m_new.append(row_max if first else jnp.maximum(m_ref[hh], row_max))
        for hh in range(hpb):
            ls = slice(hh * HEAD_DIM, (hh + 1) * HEAD_DIM)
            p = jnp.exp2(sh[hh] - _widen(m_new[hh], tq))
            row_sum = jnp.broadcast_to(jnp.sum(p, axis=-1, keepdims=True), (tq, LANES))
            pv = jnp.dot(p.astype(BF16), v_ref[pl.ds(k0, tq), ls], preferred_element_type=F32)
            if first:
                l_ref[hh] = row_sum
                acc_ref[hh] = pv
            else:
                a = jnp.exp2(m_ref[hh] - m_new[hh])
                l_ref[hh] = a * l_ref[hh] + row_sum
                acc_ref[hh] = acc_ref[hh] * a + pv
            m_ref[hh] = m_new[hh]

    def full_block(kb, carry):
        step(kb, False)
        return carry

    step(i, True)
    lax.fori_loop(0, i, full_block, 0)
    for hh in range(hpb):
        o_ref[:, hh * HEAD_DIM:(hh + 1) * HEAD_DIM] = (acc_ref[hh] / l_ref[hh]).astype(o_ref.dtype)


def _fox(qkv, cum_tok, cum_blk, *, batch, seq, n_h, q_blk0, k_blk0, v_blk0):
    m = qkv.shape[0]
    tq = _tile(seq, 512)
    nq = seq // tq
    hpb = math.gcd(n_h, FOX_HEADS_PER_BLOCK)
    assert q_blk0 % hpb == 0 and k_blk0 % hpb == 0 and v_blk0 % hpb == 0
    wb = hpb * HEAD_DIM
    kern = functools.partial(_fox_kernel, tq=tq, hpb=hpb)
    return pl.pallas_call(
        kern,
        grid=(batch, n_h // hpb, nq),
        in_specs=[
            pl.BlockSpec((tq, wb), lambda b, h, i: (b * nq + i, q_blk0 // hpb + h)),
            pl.BlockSpec((seq, wb), lambda b, h, i: (b, k_blk0 // hpb + h)),
            pl.BlockSpec((seq, wb), lambda b, h, i: (b, v_blk0 // hpb + h)),
            pl.BlockSpec((tq, n_h), lambda b, h, i: (b * nq + i, 0)),
            pl.BlockSpec((n_h, nq, tq), lambda b, h, i: (0, b, 0)),
        ],
        out_specs=pl.BlockSpec((tq, wb), lambda b, h, i: (b * nq + i, h)),
        out_shape=jax.ShapeDtypeStruct((m, n_h * HEAD_DIM), BF16),
        scratch_shapes=[
            pltpu.VMEM((hpb, tq, HEAD_DIM), F32),
            pltpu.VMEM((hpb, tq, LANES), F32),
            pltpu.VMEM((hpb, tq, LANES), F32),
        ],
        compiler_params=_params(("parallel", "parallel", "arbitrary")),
        name="fox_mixer",
    )(qkv, qkv, qkv, cum_tok, cum_blk)


def _outproj_kernel(oa_ref, ob_ref, w_ref, res_hbm, g_ref, b_ref, h_out, hb_out,
                    h_ref, hb_ref, sems_in, sem_h, sem_hb, *, alpha, n_ka, n_k, rows, col_chunk):
    i = pl.program_id(0)
    k = pl.program_id(1)
    tm, d = h_ref.shape
    row0 = pl.multiple_of(i * tm, tm)

    @pl.when(k == 0)
    def _():
        _load_rows_scaled(res_hbm, row0, h_ref, None, sems_in, alpha, rows)

    def accumulate(src_ref):
        src = src_ref[...]
        for c in range(d // col_chunk):
            sl = slice(c * col_chunk, (c + 1) * col_chunk)
            h_ref[:, sl] += jnp.dot(src, w_ref[:, sl].astype(BF16), preferred_element_type=F32)

    @pl.when(k < n_ka)
    def _():
        accumulate(oa_ref)

    @pl.when(k >= n_ka)
    def _():
        accumulate(ob_ref)

    @pl.when(k == n_k - 1)
    def _():
        _ln_epilogue(h_ref, hb_ref, g_ref, b_ref, rows, h_out, hb_out, row0, sem_h, sem_hb)


def _outproj_ln(oa, ob, w, res, g, b, alpha):
    m, wa = oa.shape
    wbw = ob.shape[1]
    d = w.shape[1]
    tm = _tile(m, 1024)
    tk = _tile(math.gcd(wa, wbw), 512)
    n_ka, n_kb = wa // tk, wbw // tk
    n_k = n_ka + n_kb
    kern = functools.partial(_outproj_kernel, alpha=alpha, n_ka=n_ka, n_k=n_k,
                             rows=_tile(tm, LN_ROWS_PER_STEP), col_chunk=_tile(d, 512))
    hbm = pl.BlockSpec(memory_space=pl.ANY)
    return pl.pallas_call(
        kern,
        grid=(m // tm, n_k),
        in_specs=[
            pl.BlockSpec((tm, tk), lambda i, k: (i, jnp.minimum(k, n_ka - 1))),
            pl.BlockSpec((tm, tk), lambda i, k: (i, jnp.maximum(k - n_ka, 0))),
            pl.BlockSpec((tk, d), lambda i, k: (k, 0)),
            hbm,
            pl.BlockSpec((1, d), lambda i, k: (0, 0)),
            pl.BlockSpec((1, d), lambda i, k: (0, 0)),
        ],
        out_specs=[hbm, hbm],
        out_shape=[jax.ShapeDtypeStruct((m, d), F32), jax.ShapeDtypeStruct((m, d), BF16)],
        scratch_shapes=_ln_tile_scratch(tm, d),
        compiler_params=_params(("parallel", "arbitrary")),
        name="outproj_ln",
    )(oa, ob, w, res, g, b)


def _t5_bucket(rel):
    nb = N_BUCKETS // 2
    max_exact = nb // 2
    ret = jnp.where(rel > 0, nb, 0)
    n = jnp.abs(rel)
    nf = jnp.maximum(n, 1).astype(jnp.float32)
    large = max_exact + (jnp.log(nf / max_exact) / math.log(MAX_DISTANCE / max_exact)
                         * (nb - max_exact)).astype(jnp.int32)
    large = jnp.minimum(large, nb - 1)
    return ret + jnp.where(n < max_exact, n, large)


def _bias_tables(rel_bias):
    n_h = rel_bias.shape[1]
    width = 2 * LANES
    period = 2 * width
    m = jnp.arange(period, dtype=jnp.int32)
    m = jnp.where(m >= width, m - period, m)
    rel = jnp.stack([m - LANES, m])
    u = jnp.moveaxis(rel_bias[_t5_bucket(rel)], -1, 1).astype(F32)
    flat = jnp.tile(u, (1, 1, Q_BLOCK))[..., :Q_BLOCK * (period - 1)]
    tab = flat.reshape(2, n_h, Q_BLOCK, period - 1)[..., :width]
    far = rel_bias[_t5_bucket(jnp.full((), -(MAX_DISTANCE + 1), jnp.int32))].astype(F32)
    return tab, far


def _hybrid_mixer_ln(h, hb, w_in, b_f, kv_norm_g, idx_k_g, idx_k_b, w_uk, w_uv, rel_bias, w_out,
                     ln_g, ln_b, alpha, batch, seq):
    m, d = h.shape
    n_h = w_uk.shape[0]
    c_lat = w_uk.shape[2]
    n_hf = b_f.shape[0]
    w_dsa = n_h * HEAD_DIM
    w_fox = n_hf * HEAD_DIM
    d_in = w_in.shape[1]
    n_ih = (d_in - w_dsa - c_lat - IDX_DIM - 3 * w_fox - n_hf) // (IDX_DIM + 1)
    w_qi = n_ih * IDX_DIM
    assert w_dsa + c_lat + w_qi + IDX_DIM + n_ih + 3 * w_fox + n_hf == d_in
    assert w_qi % w_dsa == 0 and seq % (2 * LANES) == 0

    o_ckv = w_dsa
    o_qi = o_ckv + c_lat
    o_ki = o_qi + w_qi
    o_fox = o_ki + IDX_DIM + n_ih
    o_f = o_fox + 3 * w_fox

    w_t = jnp.swapaxes(w_in, 0, 1)
    fox_q_scale = (HEAD_DIM ** -0.5) * LOG2E
    qkv = _proj(hb, w_t, ((o_qi, o_ki, 1.0), (0, w_dsa, 1.0),
                          (o_fox, o_fox + w_fox, fox_q_scale), (o_fox + w_fox, o_f, 1.0)))
    ckv, ki, wi, cum_t = _proj_small(
        hb, w_t, b_f.reshape(n_hf, 1).astype(F32),
        kv_norm_g.reshape(1, c_lat), idx_k_g.reshape(1, IDX_DIM), idx_k_b.reshape(1, IDX_DIM),
        o_ckv=o_ckv, o_ki=o_ki, o_f=o_f, c_lat=c_lat, n_ih=n_ih, seq=seq)

    tab, far = _bias_tables(rel_bias)
    tab, far = tab * LOG2E, far * LOG2E
    o_a = _dsa(qkv, wi, ki, ckv, w_uk.astype(BF16), w_uv.astype(BF16), tab, far,
               batch=batch, seq=seq, n_ih=n_ih, n_h=n_h, qi_blk=0, qa_blk=w_qi // w_dsa)

    tq = _tile(seq, 512)
    blk0 = (w_qi + w_dsa) // HEAD_DIM
    o_b = _fox(qkv, cum_t.T, cum_t.reshape(n_hf, m // tq, tq), batch=batch, seq=seq, n_h=n_hf,
               q_blk0=blk0, k_blk0=blk0 + n_hf, v_blk0=blk0 + 2 * n_hf)

    return _outproj_ln(o_a, o_b, w_out, h, ln_g.reshape(1, d), ln_b.reshape(1, d), alpha)


def kernel(x, ffn1_w_gate, ffn1_w_up, ffn1_w_down, ln1_g, ln1_b, w_in, b_f, kv_norm_g, idx_k_g,
           idx_k_b, w_uk, w_uv, rel_bias, w_out, ln2_g, ln2_b, ffn2_w_gate, ffn2_w_up,
           ffn2_w_down, ln3_g, ln3_b):
    batch, seq, d = x.shape
    depth = ffn1_w_gate.shape[0]
    alpha = (2.0 * depth) ** 0.25
    h = x.reshape(batch * seq, d)
    for l in range(depth):
        h, hb = _ffn_ln(h, ffn1_w_gate[l], ffn1_w_up[l],
                        ffn1_w_down[l], ln1_g[l].reshape(1, d), ln1_b[l].reshape(1, d),
                        alpha)
        h, hb = _hybrid_mixer_ln(h, hb, w_in[l], b_f[l], kv_norm_g[l], idx_k_g[l], idx_k_b[l],
                                 w_uk[l], w_uv[l], rel_bias, w_out[l], ln2_g[l], ln2_b[l], alpha,
                                 batch, seq)
        h, _ = _ffn_ln(h, ffn2_w_gate[l], ffn2_w_up[l],
                       ffn2_w_down[l], ln3_g[l].reshape(1, d), ln3_b[l].reshape(1, d),
                       alpha)
    return h.reshape(batch, seq, d)
```

```python
import functools
import math

import jax
import jax.numpy as jnp
import numpy as np
from jax import lax
from jax.experimental import pallas as pl
from jax.experimental.pallas import tpu as pltpu

CHUNK = 64
Q_BLOCK = 128
HEAD_DIM = 128
IDX_DIM = 128
TOPK_MAX = 256
N_BUCKETS = 32
MAX_DISTANCE = 128
LN_EPS = 1e-5
RMS_EPS = 1e-6
NEG = -1e30
LOG2E = math.log2(math.e)
DSA_HEADS_PER_GROUP = 4
FOX_HEADS_PER_BLOCK = 4
LN_ROWS_PER_STEP = 128
LN_ROW_GROUPS = 8
TILE_LOAD_CHUNKS = 8

LANES = 128
VMEM_LIMIT_BYTES = 56 * 1024 * 1024

BF16 = jnp.bfloat16
F32 = jnp.float32

INT_MIN = -(2 ** 31)
CODE_NEG_INF = -(2 ** 31) + 0x7FFFFF
F32_LOWEST = float(np.finfo(np.float32).min)


def _tile(dim, pref):
    t = min(dim, pref)
    while dim % t:
        t //= 2
    return t


def _params(sem):
    return pltpu.CompilerParams(dimension_semantics=sem, vmem_limit_bytes=VMEM_LIMIT_BYTES)


def _widen(x, width):
    reps = width // x.shape[1]
    return x if reps == 1 else jnp.concatenate([x] * reps, axis=-1)


def _layer_norm_rows(y, g, b):
    mu = jnp.mean(y, axis=-1, keepdims=True)
    d = y - mu
    var = jnp.mean(d * d, axis=-1, keepdims=True)
    return d * lax.rsqrt(var + LN_EPS) * g + b


def _load_rows_scaled(src_hbm, row0, acc_ref, xb_ref, sems, alpha, rows):
    chunk = acc_ref.shape[0] // TILE_LOAD_CHUNKS

    def copy(c):
        return pltpu.make_async_copy(src_hbm.at[pl.ds(row0 + c * chunk, chunk), :],
                                     acc_ref.at[pl.ds(c * chunk, chunk), :], sems.at[c])

    for c in range(TILE_LOAD_CHUNKS):
        copy(c).start()
    for c in range(TILE_LOAD_CHUNKS):
        copy(c).wait()

        def body(r, carry, c=c):
            r0 = pl.multiple_of(c * chunk + r * rows, rows)
            xr = acc_ref[pl.ds(r0, rows), :]
            if xb_ref is not None:
                xb_ref[pl.ds(r0, rows), :] = xr.astype(BF16)
            acc_ref[pl.ds(r0, rows), :] = alpha * xr
            return carry

        lax.fori_loop(0, chunk // rows, body, 0)


def _ln_epilogue(h_ref, hb_ref, g_ref, b_ref, rows, h_out, hb_out, row0, sem_h, sem_hb):
    g = g_ref[...]
    b = b_ref[...]
    sub = rows // LN_ROW_GROUPS
    n_steps = h_ref.shape[0] // rows

    def out_copies(r0):
        return (pltpu.make_async_copy(h_ref.at[pl.ds(r0, rows), :],
                                      h_out.at[pl.ds(row0 + r0, rows), :], sem_h),
                pltpu.make_async_copy(hb_ref.at[pl.ds(r0, rows), :],
                                      hb_out.at[pl.ds(row0 + r0, rows), :], sem_hb))

    def body(r, carry):
        r0 = pl.multiple_of(r * rows, rows)
        sls = [pl.ds(r0 + k * sub, sub) for k in range(LN_ROW_GROUPS)]
        mus = [jnp.mean(h_ref[sl, :], axis=-1, keepdims=True) for sl in sls]
        rstds = []
        for sl, mu in zip(sls, mus):
            d = h_ref[sl, :] - mu
            rstds.append(lax.rsqrt(jnp.mean(d * d, axis=-1, keepdims=True) + LN_EPS))
        for sl, mu, rstd in zip(sls, mus, rstds):
            out = (h_ref[sl, :] - mu) * rstd * g + b
            h_ref[sl, :] = out
            hb_ref[sl, :] = out.astype(BF16)
        for cp in out_copies(r0):
            cp.start()
        return carry

    lax.fori_loop(0, n_steps, body, 0)

    def drain(r, carry):
        for cp in out_copies(pl.multiple_of(r * rows, rows)):
            cp.wait()
        return carry

    lax.fori_loop(0, n_steps, drain, 0)


def _ln_tile_scratch(tm, d):
    return [pltpu.VMEM((tm, d), F32), pltpu.VMEM((tm, d), BF16),
            pltpu.SemaphoreType.DMA((TILE_LOAD_CHUNKS,)),
            pltpu.SemaphoreType.DMA(()), pltpu.SemaphoreType.DMA(())]


def _ffn_kernel(x_hbm, wg_hbm, wu_hbm, wd_hbm, g_ref, b_ref, h_out, hb_out,
                h_ref, xb_ref, sems_in, sem_h, sem_hb, wg_buf, wu_buf, wd_buf, w_sems, *,
                alpha, n_f, tf, rows, col_chunk):
    i = pl.program_id(0)
    tm, d = h_ref.shape
    row0 = pl.multiple_of(i * tm, tm)

    def weight_copies(f, slot):
        c0 = pl.multiple_of(f * tf, tf)
        return (pltpu.make_async_copy(wg_hbm.at[:, pl.ds(c0, tf)], wg_buf.at[slot], w_sems.at[0, slot]),
                pltpu.make_async_copy(wu_hbm.at[:, pl.ds(c0, tf)], wu_buf.at[slot], w_sems.at[1, slot]),
                pltpu.make_async_copy(wd_hbm.at[pl.ds(c0, tf), :], wd_buf.at[slot], w_sems.at[2, slot]))

    @pl.when(i == 0)
    def _():
        for cp in weight_copies(0, 0):
            cp.start()

    _load_rows_scaled(x_hbm, row0, h_ref, xb_ref, sems_in, alpha, rows)

    def step(f, carry):
        slot = (i * n_f + f) % 2

        @pl.when((f + 1 < n_f) | (i + 1 < pl.num_programs(0)))
        def _():
            nxt = jnp.where(f + 1 < n_f, f + 1, 0)
            for cp in weight_copies(nxt, 1 - slot):
                cp.start()

        for cp in weight_copies(f, slot):
            cp.wait()
        xb = xb_ref[...]
        gate = jnp.dot(xb, wg_buf[slot].astype(BF16), preferred_element_type=F32)
        up = jnp.dot(xb, wu_buf[slot].astype(BF16), preferred_element_type=F32)
        act = (0.5 * (gate * jax.nn.sigmoid(gate)) * up).astype(BF16)
        wd = wd_buf[slot].astype(BF16)
        for c in range(d // col_chunk):
            sl = slice(c * col_chunk, (c + 1) * col_chunk)
            h_ref[:, sl] += jnp.dot(act, wd[:, sl], preferred_element_type=F32)
        return carry

    lax.fori_loop(0, n_f, step, 0)
    _ln_epilogue(h_ref, xb_ref, g_ref, b_ref, rows, h_out, hb_out, row0, sem_h, sem_hb)


def _ffn_ln(x, wg, wu, wd, g, b, alpha):
    m, d = x.shape
    f = wg.shape[1]
    tm = _tile(m, 1024)
    tf = _tile(f, 256)
    n_f = f // tf
    kern = functools.partial(_ffn_kernel, alpha=alpha, n_f=n_f, tf=tf,
                             rows=_tile(tm, LN_ROWS_PER_STEP), col_chunk=_tile(d, 512))
    hbm = pl.BlockSpec(memory_space=pl.ANY)
    return pl.pallas_call(
        kern,
        grid=(m // tm,),
        in_specs=[hbm, hbm, hbm, hbm,
                  pl.BlockSpec((1, d), lambda i: (0, 0)), pl.BlockSpec((1, d), lambda i: (0, 0))],
        out_specs=[hbm, hbm],
        out_shape=[jax.ShapeDtypeStruct((m, d), F32), jax.ShapeDtypeStruct((m, d), BF16)],
        scratch_shapes=_ln_tile_scratch(tm, d) + [
            pltpu.VMEM((2, d, tf), F32), pltpu.VMEM((2, d, tf), F32), pltpu.VMEM((2, tf, d), F32),
            pltpu.SemaphoreType.DMA((3, 2))],
        compiler_params=_params(("arbitrary",)),
        name="ffn_ln",
    )(x, wg, wu, wd, g, b)


def _proj_kernel(offs_ref, scales_ref, x_ref, wt_ref, o_ref):
    del offs_ref
    y = lax.dot_general(x_ref[...], wt_ref[...].astype(BF16), (((1,), (1,)), ((), ())),
                        preferred_element_type=F32)
    o_ref[...] = (y * scales_ref[pl.program_id(1)]).astype(o_ref.dtype)


def _proj(xb, w_t, col_ranges):
    m, d = xb.shape
    tm = _tile(m, 1024)
    tn = _tile(math.gcd(*[b - a for a, b, _ in col_ranges]), 512)
    tiles = [(a + k * tn, sc) for a, b, sc in col_ranges for k in range((b - a) // tn)]
    assert all(o % 8 == 0 for o, _ in tiles)
    grid_spec = pltpu.PrefetchScalarGridSpec(
        num_scalar_prefetch=2,
        grid=(m // tm, len(tiles)),
        in_specs=[
            pl.BlockSpec((tm, d), lambda i, j, offs, scales: (i, 0)),
            pl.BlockSpec((pl.Element(tn), pl.Element(d)),
                         lambda i, j, offs, scales: (pl.multiple_of(offs[j], 8), 0)),
        ],
        out_specs=pl.BlockSpec((tm, tn), lambda i, j, offs, scales: (i, j)),
    )
    return pl.pallas_call(
        _proj_kernel,
        grid_spec=grid_spec,
        out_shape=jax.ShapeDtypeStruct((m, len(tiles) * tn), BF16),
        compiler_params=_params(("parallel", "arbitrary")),
        name="proj_main",
    )(jnp.asarray([o for o, _ in tiles], jnp.int32), jnp.asarray([sc for _, sc in tiles], F32),
      xb, w_t)


def _split3(v):
    hi = v.astype(BF16)
    r1 = v - hi.astype(F32)
    mid = r1.astype(BF16)
    lo = (r1 - mid.astype(F32)).astype(BF16)
    return hi, mid, lo


def _proj_small_kernel(x_ref, wc_ref, wk_ref, wft_ref, bf_ref, kvg_ref, kg_ref, kb_ref,
                       ckv_ref, ki_ref, wi_ref, cum_ref, carry_ref, *, n_ih, tiles_per_seq):
    i = pl.program_id(0)
    x = x_ref[...]
    tm = x.shape[0]
    nt = (((1,), (1,)), ((), ()))
    c = lax.dot_general(x, wc_ref[...].astype(BF16), nt, preferred_element_type=F32)
    c = c * lax.rsqrt(jnp.mean(c * c, axis=-1, keepdims=True) + RMS_EPS) * kvg_ref[...]
    ckv_ref[...] = c.astype(BF16)
    kw = lax.dot_general(x, wk_ref[...].astype(BF16), nt, preferred_element_type=F32)
    ki_ref[...] = _layer_norm_rows(kw[:, :IDX_DIM], kg_ref[...], kb_ref[...]).astype(BF16)
    wi_ref[...] = kw[:, IDX_DIM:IDX_DIM + n_ih]

    ft = lax.dot_general(wft_ref[...].astype(BF16), x, nt, preferred_element_type=F32)
    z = ft + bf_ref[...]
    log_f = jnp.minimum(z, 0.0) - jnp.log1p(jnp.exp(-jnp.abs(z)))

    @pl.when(i % tiles_per_seq == 0)
    def _():
        carry_ref[...] = jnp.zeros_like(carry_ref)

    row = lax.broadcasted_iota(jnp.int32, (tm, tm), 0)
    col = lax.broadcasted_iota(jnp.int32, (tm, tm), 1)
    tri = jnp.where(row <= col, 1.0, 0.0).astype(BF16)
    hi, mid, lo = _split3(log_f)
    cs = (jnp.dot(hi, tri, preferred_element_type=F32)
          + jnp.dot(mid, tri, preferred_element_type=F32)
          + jnp.dot(lo, tri, preferred_element_type=F32))
    cum = cs + carry_ref[:, 0:1]
    cum_ref[...] = cum
    carry_ref[...] = jnp.broadcast_to(cum[:, tm - 1:tm], carry_ref.shape)


def _proj_small(xb, w_t, bf, kvg, kg, kb, *, o_ckv, o_ki, o_f, c_lat, n_ih, seq):
    m, d = xb.shape
    hf = bf.shape[0]
    tm = _tile(seq, 512)
    kern = functools.partial(_proj_small_kernel, n_ih=n_ih, tiles_per_seq=seq // tm)

    def rows(start, count):
        return pl.BlockSpec((pl.Element(count), pl.Element(d)), lambda i: (start, 0),
                            pipeline_mode=pl.Buffered(1))

    return pl.pallas_call(
        kern,
        grid=(m // tm,),
        in_specs=[
            pl.BlockSpec((tm, d), lambda i: (i, 0)),
            rows(o_ckv, c_lat),
            rows(o_ki, IDX_DIM + n_ih),
            rows(o_f, hf),
            pl.BlockSpec((hf, 1), lambda i: (0, 0)),
            pl.BlockSpec((1, c_lat), lambda i: (0, 0)),
            pl.BlockSpec((1, IDX_DIM), lambda i: (0, 0)),
            pl.BlockSpec((1, IDX_DIM), lambda i: (0, 0)),
        ],
        out_specs=[
            pl.BlockSpec((tm, c_lat), lambda i: (i, 0)),
            pl.BlockSpec((tm, IDX_DIM), lambda i: (i, 0)),
            pl.BlockSpec((tm, n_ih), lambda i: (i, 0)),
            pl.BlockSpec((hf, tm), lambda i: (0, i)),
        ],
        out_shape=[
            jax.ShapeDtypeStruct((m, c_lat), BF16),
            jax.ShapeDtypeStruct((m, IDX_DIM), BF16),
            jax.ShapeDtypeStruct((m, n_ih), F32),
            jax.ShapeDtypeStruct((hf, m), F32),
        ],
        scratch_shapes=[pltpu.VMEM((hf, LANES), F32)],
        compiler_params=_params(("arbitrary",)),
        name="proj_small",
    )(xb, w_t, w_t, w_t, bf, kvg, kg, kb)


def _dsa_kernel(qi_ref, qa_ref, wi_ref, ki_ref, ckv_ref, wuk_ref, wuv_ref, tab_ref, bfar_ref,
                o_ref, score_ref, score_t_ref, score_t16_ref, qlat_ref, acc_ref, m_ref, l_ref, p_ref,
                alpha_ref, *,
                n_ih, n_h, k_sel, tk, hpg):
    i = pl.program_id(1)
    start = i * Q_BLOCK
    qb = Q_BLOCK
    bpc = tk // LANES
    scale = HEAD_DIM ** -0.5
    w_fold = (IDX_DIM ** -0.5) * (n_ih ** -0.5)

    wcol = wi_ref[...] * w_fold
    row = lax.broadcasted_iota(jnp.int32, (qb, tk), 0)
    col = lax.broadcasted_iota(jnp.int32, (qb, tk), 1)
    limit = start + jnp.where(row < CHUNK, CHUNK, 2 * CHUNK)
    n_chunks = (i + bpc) // bpc

    def score_chunk(c, carry):
        k0 = pl.multiple_of(c * tk, tk)
        kblk = ki_ref[pl.ds(k0, tk), :]
        sc = jnp.zeros((qb, tk), F32)
        for h in range(n_ih):
            s = lax.dot_general(qi_ref[:, h * IDX_DIM:(h + 1) * IDX_DIM], kblk,
                                (((1,), (1,)), ((), ())), preferred_element_type=F32)
            sc = sc + jnp.maximum(s, 0.0) * wcol[:, h:h + 1]
        sc = jnp.where(col + k0 < limit, sc, -jnp.inf)
        for j in range(bpc):
            blk = sc[:, j * LANES:(j + 1) * LANES]
            score_ref[c * bpc + j] = blk
            blk_t = blk.T
            score_t_ref[c * bpc + j] = blk_t
            score_t16_ref[c * bpc + j] = truncate16(blk_t).astype(BF16)
        return carry

    def code_to_float(code):
        bits = jnp.where(code >= 0, code, code ^ 0x7FFFFFFF)
        return lax.bitcast_convert_type(bits, F32)

    def truncate16(v):
        bits = lax.bitcast_convert_type(v, jnp.int32) & jnp.int32(-65536)
        return lax.bitcast_convert_type(bits, F32)

    lax.fori_loop(0, n_chunks, score_chunk, 0)

    def bit_body16(bi, code):
        cand = code + lax.shift_left(jnp.int32(1), 31 - bi)
        cf = jnp.concatenate([truncate16(code_to_float(cand)).astype(BF16)] * (LANES // 16), axis=0)
        one, zero = jnp.ones((), BF16), jnp.zeros((), BF16)

        def count_chunk(c, cnt):
            for j in range(bpc):
                hit = jnp.where(score_t16_ref[c * bpc + j] >= cf, one, zero)
                part = hit[0:16]
                for r in range(1, LANES // 16):
                    part = part + hit[r * 16:(r + 1) * 16]
                cnt = cnt + part.astype(F32)
            return cnt

        cnt = lax.fori_loop(0, n_chunks, count_chunk, jnp.zeros((16, qb), F32))
        total = jnp.sum(cnt, axis=0, keepdims=True)
        return jnp.where(total >= k_sel, cand, code)

    def bit_body(bi, code):
        cand = code + lax.shift_left(jnp.int32(1), 31 - bi)
        cf = jnp.concatenate([code_to_float(cand)] * (LANES // 8), axis=0)

        def count_chunk(c, cnt):
            for j in range(bpc):
                hit = jnp.where(score_t_ref[c * bpc + j] >= cf, 1.0, 0.0)
                cnt = cnt + jnp.sum(hit.reshape(LANES // 8, 8, qb), axis=0)
            return cnt

        cnt = lax.fori_loop(0, n_chunks, count_chunk, jnp.zeros((8, qb), F32))
        total = jnp.sum(cnt, axis=0, keepdims=True)
        return jnp.where(total >= k_sel, cand, code)

    code = lax.fori_loop(0, 16, bit_body16, jnp.full((16, qb), INT_MIN, jnp.int32))
    code = lax.fori_loop(16, 32, bit_body, code[:8])
    thr_row = jnp.where(code <= CODE_NEG_INF, F32_LOWEST, code_to_float(code))
    thr = jnp.concatenate([thr_row] * (qb // 8), axis=0).T

    for h in range(n_h):
        ql = jnp.dot(qa_ref[:, h * HEAD_DIM:(h + 1) * HEAD_DIM], wuk_ref[h],
                     preferred_element_type=F32)
        qlat_ref[h * qb:(h + 1) * qb, :] = (ql * (scale * LOG2E)).astype(BF16)

    def flash_step(k0, width, sel, bias_of_head, first):
        cblk = ckv_ref[pl.ds(k0, width), :]
        s_of = {}
        for g in range(n_h // hpg):
            sg = lax.dot_general(qlat_ref[g * hpg * qb:(g + 1) * hpg * qb, :], cblk,
                                 (((1,), (1,)), ((), ())), preferred_element_type=F32)
            for hh in range(hpg):
                s_of[g * hpg + hh] = sg[hh * qb:(hh + 1) * qb, :]
        for g in range(n_h // hpg):
            rg = slice(g * hpg * qb, (g + 1) * hpg * qb)
            heads = [g * hpg + hh for hh in range(hpg)]
            sh, m_new = {}, {}
            for h in heads:
                rs = slice(h * qb, (h + 1) * qb)
                sh[h] = jnp.where(sel, s_of[h] + bias_of_head(h), NEG)
                row_max = jnp.broadcast_to(jnp.max(sh[h], axis=-1, keepdims=True), (qb, LANES))
                m_new[h] = row_max if first else jnp.maximum(m_ref[rs, :], row_max)
            for h in heads:
                rs = slice(h * qb, (h + 1) * qb)
                p = jnp.exp2(sh[h] - _widen(m_new[h], width))
                row_sum = jnp.broadcast_to(jnp.sum(p, axis=-1, keepdims=True), (qb, LANES))
                if first:
                    l_ref[rs, :] = row_sum
                else:
                    a = jnp.exp2(m_ref[rs, :] - m_new[h])
                    l_ref[rs, :] = a * l_ref[rs, :] + row_sum
                    alpha_ref[rs, :] = a
                m_ref[rs, :] = m_new[h]
                p_ref[rs, 0:width] = p.astype(BF16)
            pv = jnp.dot(p_ref[rg, 0:width], cblk, preferred_element_type=F32)
            if first:
                acc_ref[rg, :] = pv
            else:
                acc_ref[rg, :] = acc_ref[rg, :] * _widen(alpha_ref[rg, :], pv.shape[1]) + pv

    wb = jnp.maximum(i - 1, 0)
    tsel = jnp.where(i == 0, 1, 0)
    scw = jnp.concatenate([score_ref[wb], score_ref[wb + 1]], axis=-1)
    selw = scw >= _widen(thr, 2 * LANES)
    flash_step(pl.multiple_of(wb * LANES, LANES), 2 * LANES, selw, lambda h: tab_ref[tsel, h],
               first=True)

    far_end = (i - 1) * qb

    def far_chunk(c, carry):
        k0 = pl.multiple_of(c * tk, tk)
        sc = jnp.concatenate([score_ref[c * bpc + j] for j in range(bpc)], axis=-1)
        sel = (sc >= _widen(thr, tk)) & (col + k0 < far_end)
        flash_step(k0, tk, sel, lambda h: bfar_ref[h], first=False)
        return carry

    n_far = jnp.maximum((i - 1 + bpc - 1) // bpc, 0)
    lax.fori_loop(0, n_far, far_chunk, 0)

    for h in range(n_h):
        rs = slice(h * qb, (h + 1) * qb)
        o_lat = (acc_ref[rs, :] / _widen(l_ref[rs, :], acc_ref.shape[1])).astype(BF16)
        o_ref[:, h * HEAD_DIM:(h + 1) * HEAD_DIM] = jnp.dot(
            o_lat, wuv_ref[h], preferred_element_type=F32).astype(o_ref.dtype)


def _dsa(qkv, wi, ki, ckv, wuk, wuv, tab, bfar, *, batch, seq, n_ih, n_h, qi_blk, qa_blk):
    m = qkv.shape[0]
    c_lat = ckv.shape[1]
    nq = seq // Q_BLOCK
    k_sel = min(TOPK_MAX, seq // 4)
    tk = _tile(seq, 512)
    rows = n_h * Q_BLOCK
    kern = functools.partial(_dsa_kernel, n_ih=n_ih, n_h=n_h, k_sel=k_sel, tk=tk,
                             hpg=math.gcd(n_h, DSA_HEADS_PER_GROUP))
    const = dict(pipeline_mode=pl.Buffered(1))
    return pl.pallas_call(
        kern,
        grid=(batch, nq),
        in_specs=[
            pl.BlockSpec((Q_BLOCK, n_ih * IDX_DIM), lambda b, i: (b * nq + i, qi_blk)),
            pl.BlockSpec((Q_BLOCK, n_h * HEAD_DIM), lambda b, i: (b * nq + i, qa_blk)),
            pl.BlockSpec((Q_BLOCK, n_ih), lambda b, i: (b * nq + i, 0)),
            pl.BlockSpec((seq, IDX_DIM), lambda b, i: (b, 0)),
            pl.BlockSpec((seq, c_lat), lambda b, i: (b, 0)),
            pl.BlockSpec((n_h, HEAD_DIM, c_lat), lambda b, i: (0, 0, 0), **const),
            pl.BlockSpec((n_h, c_lat, HEAD_DIM), lambda b, i: (0, 0, 0), **const),
            pl.BlockSpec((2, n_h, Q_BLOCK, 2 * LANES), lambda b, i: (0, 0, 0, 0), **const),
            pl.BlockSpec(memory_space=pltpu.SMEM),
        ],
        out_specs=pl.BlockSpec((Q_BLOCK, n_h * HEAD_DIM), lambda b, i: (b * nq + i, 0)),
        out_shape=jax.ShapeDtypeStruct((m, n_h * HEAD_DIM), BF16),
        scratch_shapes=[
            pltpu.VMEM((seq // LANES, Q_BLOCK, LANES), F32),
            pltpu.VMEM((seq // LANES, LANES, Q_BLOCK), F32),
            pltpu.VMEM((seq // LANES, LANES, Q_BLOCK), BF16),
            pltpu.VMEM((rows, c_lat), BF16),
            pltpu.VMEM((rows, c_lat), F32),
            pltpu.VMEM((rows, LANES), F32),
            pltpu.VMEM((rows, LANES), F32),
            pltpu.VMEM((rows, tk), BF16),
            pltpu.VMEM((rows, LANES), F32),
        ],
        compiler_params=_params(("parallel", "arbitrary")),
        name="dsa_mixer",
    )(qkv, qkv, wi, ki, ckv, wuk, wuv, tab, bfar)


def _fox_kernel(q_ref, k_ref, v_ref, cq_ref, ck_ref, o_ref, acc_ref, m_ref, l_ref, *, tq, hpb):
    hp = pl.program_id(1)
    i = pl.program_id(2)
    lane = lax.broadcasted_iota(jnp.int32, cq_ref.shape, 1)
    cq_all = cq_ref[...] * LOG2E
    cq = [jnp.sum(jnp.where(lane == hp * hpb + hh, cq_all, 0.0), axis=-1, keepdims=True)
          for hh in range(hpb)]

    def step(kb, first):
        causal = first
        k0 = pl.multiple_of(kb * tq, tq)
        logits = []
        for hh in range(hpb):
            ls = slice(hh * HEAD_DIM, (hh + 1) * HEAD_DIM)
            logits.append(lax.dot_general(q_ref[:, ls], k_ref[pl.ds(k0, tq), ls],
                                          (((1,), (1,)), ((), ())), preferred_element_type=F32))
        sh, m_new = [], []
        for hh in range(hpb):
            ck = ck_ref[pl.ds(hp * hpb + hh, 1), pl.ds(kb, 1), :].reshape(1, tq) * LOG2E
            s = logits[hh] + (cq[hh] - ck)
            if causal:
                row = lax.broadcasted_iota(jnp.int32, (tq, tq), 0)
                col = lax.broadcasted_iota(jnp.int32, (tq, tq), 1)
                s = jnp.where(col <= row, s, NEG)
            sh.append(s)
            row_max = jnp.broadcast_to(jnp.max(s, axis=-1, keepdims=True), (tq, LANES))
            m_new.append(row_max if first else jnp.maximum(m_ref[hh], row_max))
        for hh in range(hpb):
            ls = slice(hh * HEAD_DIM, (hh + 1) * HEAD_DIM)
            p = jnp.exp2(sh[hh] - _widen(m_new[hh], tq))
            row_sum = jnp.broadcast_to(jnp.sum(p, axis=-1, keepdims=True), (tq, LANES))
            pv = jnp.dot(p.astype(BF16), v_ref[pl.ds(k0, tq), ls], preferred_element_type=F32)
            if first:
                l_ref[hh] = row_sum
                acc_ref[hh] = pv
            else:
                a = jnp.exp2(m_ref[hh] - m_new[hh])
                l_ref[hh] = a * l_ref[hh] + row_sum
                acc_ref[hh] = acc_ref[hh] * a + pv
            m_ref[hh] = m_new[hh]

    def full_block(kb, carry):
        step(kb, False)
        return carry

    step(i, True)
    lax.fori_loop(0, i, full_block, 0)
    for hh in range(hpb):
        o_ref[:, hh * HEAD_DIM:(hh + 1) * HEAD_DIM] = (acc_ref[hh] / l_ref[hh]).astype(o_ref.dtype)


def _fox(qkv, cum_tok, cum_blk, *, batch, seq, n_h, q_blk0, k_blk0, v_blk0):
    m = qkv.shape[0]
    tq = _tile(seq, 512)
    nq = seq // tq
    hpb = math.gcd(n_h, FOX_HEADS_PER_BLOCK)
    assert q_blk0 % hpb == 0 and k_blk0 % hpb == 0 and v_blk0 % hpb == 0
    wb = hpb * HEAD_DIM
    kern = functools.partial(_fox_kernel, tq=tq, hpb=hpb)
    return pl.pallas_call(
        kern,
        grid=(batch, n_h // hpb, nq),
        in_specs=[
            pl.BlockSpec((tq, wb), lambda b, h, i: (b * nq + i, q_blk0 // hpb + h)),
            pl.BlockSpec((seq, wb), lambda b, h, i: (b, k_blk0 // hpb + h)),
            pl.BlockSpec((seq, wb), lambda b, h, i: (b, v_blk0 // hpb + h)),
            pl.BlockSpec((tq, n_h), lambda b, h, i: (b * nq + i, 0)),
            pl.BlockSpec((n_h, nq, tq), lambda b, h, i: (0, b, 0)),
        ],
        out_specs=pl.BlockSpec((tq, wb), lambda b, h, i: (b * nq + i, h)),
        out_shape=jax.ShapeDtypeStruct((m, n_h * HEAD_DIM), BF16),
        scratch_shapes=[
            pltpu.VMEM((hpb, tq, HEAD_DIM), F32),
            pltpu.VMEM((hpb, tq, LANES), F32),
            pltpu.VMEM((hpb, tq, LANES), F32),
        ],
        compiler_params=_params(("parallel", "parallel", "arbitrary")),
        name="fox_mixer",
    )(qkv, qkv, qkv, cum_tok, cum_blk)


def _outproj_kernel(oa_ref, ob_ref, w_ref, res_hbm, g_ref, b_ref, h_out, hb_out,
                    h_ref, hb_ref, sems_in, sem_h, sem_hb, *, alpha, n_ka, n_k, rows, col_chunk):
    i = pl.program_id(0)
    k = pl.program_id(1)
    tm, d = h_ref.shape
    row0 = pl.multiple_of(i * tm, tm)

    @pl.when(k == 0)
    def _():
        _load_rows_scaled(res_hbm, row0, h_ref, None, sems_in, alpha, rows)

    def accumulate(src_ref):
        src = src_ref[...]
        w = w_ref[...].astype(BF16)
        for c in range(d // col_chunk):
            sl = slice(c * col_chunk, (c + 1) * col_chunk)
            h_ref[:, sl] += jnp.dot(src, w[:, sl], preferred_element_type=F32)

    @pl.when(k < n_ka)
    def _():
        accumulate(oa_ref)

    @pl.when(k >= n_ka)
    def _():
        accumulate(ob_ref)

    @pl.when(k == n_k - 1)
    def _():
        _ln_epilogue(h_ref, hb_ref, g_ref, b_ref, rows, h_out, hb_out, row0, sem_h, sem_hb)


def _outproj_ln(oa, ob, w, res, g, b, alpha):
    m, wa = oa.shape
    wbw = ob.shape[1]
    d = w.shape[1]
    tm = _tile(m, 1024)
    tk = _tile(math.gcd(wa, wbw), 512)
    n_ka, n_kb = wa // tk, wbw // tk
    n_k = n_ka + n_kb
    kern = functools.partial(_outproj_kernel, alpha=alpha, n_ka=n_ka, n_k=n_k,
                             rows=_tile(tm, LN_ROWS_PER_STEP), col_chunk=_tile(d, 512))
    hbm = pl.BlockSpec(memory_space=pl.ANY)
    return pl.pallas_call(
        kern,
        grid=(m // tm, n_k),
        in_specs=[
            pl.BlockSpec((tm, tk), lambda i, k: (i, jnp.minimum(k, n_ka - 1))),
            pl.BlockSpec((tm, tk), lambda i, k: (i, jnp.maximum(k - n_ka, 0))),
            pl.BlockSpec((tk, d), lambda i, k: (k, 0)),
            hbm,
            pl.BlockSpec((1, d), lambda i, k: (0, 0)),
            pl.BlockSpec((1, d), lambda i, k: (0, 0)),
        ],
        out_specs=[hbm, hbm],
        out_shape=[jax.ShapeDtypeStruct((m, d), F32), jax.ShapeDtypeStruct((m, d), BF16)],
        scratch_shapes=_ln_tile_scratch(tm, d),
        compiler_params=_params(("parallel", "arbitrary")),
        name="outproj_ln",
    )(oa, ob, w, res, g, b)


def _t5_bucket(rel):
    nb = N_BUCKETS // 2
    max_exact = nb // 2
    ret = jnp.where(rel > 0, nb, 0)
    n = jnp.abs(rel)
    nf = jnp.maximum(n, 1).astype(jnp.float32)
    large = max_exact + (jnp.log(nf / max_exact) / math.log(MAX_DISTANCE / max_exact)
                         * (nb - max_exact)).astype(jnp.int32)
    large = jnp.minimum(large, nb - 1)
    return ret + jnp.where(n < max_exact, n, large)


def _bias_tables(rel_bias):
    n_h = rel_bias.shape[1]
    width = 2 * LANES
    period = 2 * width
    m = jnp.arange(period, dtype=jnp.int32)
    m = jnp.where(m >= width, m - period, m)
    rel = jnp.stack([m - LANES, m])
    u = jnp.moveaxis(rel_bias[_t5_bucket(rel)], -1, 1).astype(F32)
    flat = jnp.tile(u, (1, 1, Q_BLOCK))[..., :Q_BLOCK * (period - 1)]
    tab = flat.reshape(2, n_h, Q_BLOCK, period - 1)[..., :width]
    far = rel_bias[_t5_bucket(jnp.full((), -(MAX_DISTANCE + 1), jnp.int32))].astype(F32)
    return tab, far


def _hybrid_mixer_ln(h, hb, w_in, b_f, kv_norm_g, idx_k_g, idx_k_b, w_uk, w_uv, rel_bias, w_out,
                     ln_g, ln_b, alpha, batch, seq):
    m, d = h.shape
    n_h = w_uk.shape[0]
    c_lat = w_uk.shape[2]
    n_hf = b_f.shape[0]
    w_dsa = n_h * HEAD_DIM
    w_fox = n_hf * HEAD_DIM
    d_in = w_in.shape[1]
    n_ih = (d_in - w_dsa - c_lat - IDX_DIM - 3 * w_fox - n_hf) // (IDX_DIM + 1)
    w_qi = n_ih * IDX_DIM
    assert w_dsa + c_lat + w_qi + IDX_DIM + n_ih + 3 * w_fox + n_hf == d_in
    assert w_qi % w_dsa == 0 and seq % (2 * LANES) == 0

    o_ckv = w_dsa
    o_qi = o_ckv + c_lat
    o_ki = o_qi + w_qi
    o_fox = o_ki + IDX_DIM + n_ih
    o_f = o_fox + 3 * w_fox

    w_t = jnp.swapaxes(w_in, 0, 1)
    fox_q_scale = (HEAD_DIM ** -0.5) * LOG2E
    qkv = _proj(hb, w_t, ((o_qi, o_ki, 1.0), (0, w_dsa, 1.0),
                          (o_fox, o_fox + w_fox, fox_q_scale), (o_fox + w_fox, o_f, 1.0)))
    ckv, ki, wi, cum_t = _proj_small(
        hb, w_t, b_f.reshape(n_hf, 1).astype(F32),
        kv_norm_g.reshape(1, c_lat), idx_k_g.reshape(1, IDX_DIM), idx_k_b.reshape(1, IDX_DIM),
        o_ckv=o_ckv, o_ki=o_ki, o_f=o_f, c_lat=c_lat, n_ih=n_ih, seq=seq)

    tab, far = _bias_tables(rel_bias)
    tab, far = tab * LOG2E, far * LOG2E
    o_a = _dsa(qkv, wi, ki, ckv, w_uk.astype(BF16), w_uv.astype(BF16), tab, far,
               batch=batch, seq=seq, n_ih=n_ih, n_h=n_h, qi_blk=0, qa_blk=w_qi // w_dsa)

    tq = _tile(seq, 512)
    blk0 = (w_qi + w_dsa) // HEAD_DIM
    o_b = _fox(qkv, cum_t.T, cum_t.reshape(n_hf, m // tq, tq), batch=batch, seq=seq, n_h=n_hf,
               q_blk0=blk0, k_blk0=blk0 + n_hf, v_blk0=blk0 + 2 * n_hf)

    return _outproj_ln(o_a, o_b, w_out, h, ln_g.reshape(1, d), ln_b.reshape(1, d), alpha)


def kernel(x, ffn1_w_gate, ffn1_w_up, ffn1_w_down, ln1_g, ln1_b, w_in, b_f, kv_norm_g, idx_k_g,
           idx_k_b, w_uk, w_uv, rel_bias, w_out, ln2_g, ln2_b, ffn2_w_gate, ffn2_w_up,
           ffn2_w_down, ln3_g, ln3_b):
    batch, seq, d = x.shape
    depth = ffn1_w_gate.shape[0]
    alpha = (2.0 * depth) ** 0.25
    h = x.reshape(batch * seq, d)
    for l in range(depth):
        h, hb = _ffn_ln(h, ffn1_w_gate[l], ffn1_w_up[l],
                        ffn1_w_down[l], ln1_g[l].reshape(1, d), ln1_b[l].reshape(1, d),
                        alpha)
        h, hb = _hybrid_mixer_ln(h, hb, w_in[l], b_f[l], kv_norm_g[l], idx_k_g[l], idx_k_b[l],
                                 w_uk[l], w_uv[l], rel_bias, w_out[l], ln2_g[l], ln2_b[l], alpha,
                                 batch, seq)
        h, _ = _ffn_ln(h, ffn2_w_gate[l], ffn2_w_up[l],
                       ffn2_w_down[l], ln3_g[l].reshape(1, d), ln3_b[l].reshape(1, d),
                       alpha)
    return h.reshape(batch, seq, d)
```

```python
import functools
import math

import jax
import jax.numpy as jnp
import numpy as np
from jax import lax
from jax.experimental import pallas as pl
from jax.experimental.pallas import tpu as pltpu

CHUNK = 64
Q_BLOCK = 128
HEAD_DIM = 128
IDX_DIM = 128
TOPK_MAX = 256
N_BUCKETS = 32
MAX_DISTANCE = 128
LN_EPS = 1e-5
RMS_EPS = 1e-6
NEG = -1e30
LOG2E = math.log2(math.e)
DSA_HEADS_PER_GROUP = 4
FOX_HEADS_PER_BLOCK = 4
LN_ROWS_PER_STEP = 128
LN_ROW_GROUPS = 8
TILE_LOAD_CHUNKS = 8
BIT_SEARCH_STRIDE = 8

LANES = 128
VMEM_LIMIT_BYTES = 56 * 1024 * 1024

BF16 = jnp.bfloat16
F32 = jnp.float32

INT_MIN = -(2 ** 31)
CODE_NEG_INF = -(2 ** 31) + 0x7FFFFF
F32_LOWEST = float(np.finfo(np.float32).min)


def _tile(dim, pref):
    t = min(dim, pref)
    while dim % t:
        t //= 2
    return t


def _params(sem):
    return pltpu.CompilerParams(dimension_semantics=sem, vmem_limit_bytes=VMEM_LIMIT_BYTES)


def _widen(x, width):
    reps = width // x.shape[1]
    return x if reps == 1 else jnp.concatenate([x] * reps, axis=-1)


def _layer_norm_rows(y, g, b):
    mu = jnp.mean(y, axis=-1, keepdims=True)
    d = y - mu
    var = jnp.mean(d * d, axis=-1, keepdims=True)
    return d * lax.rsqrt(var + LN_EPS) * g + b


def _load_rows_scaled(src_hbm, row0, acc_ref, xb_ref, sems, alpha, rows):
    chunk = acc_ref.shape[0] // TILE_LOAD_CHUNKS

    def copy(c):
        return pltpu.make_async_copy(src_hbm.at[pl.ds(row0 + c * chunk, chunk), :],
                                     acc_ref.at[pl.ds(c * chunk, chunk), :], sems.at[c])

    for c in range(TILE_LOAD_CHUNKS):
        copy(c).start()
    for c in range(TILE_LOAD_CHUNKS):
        copy(c).wait()

        def body(r, carry, c=c):
            r0 = pl.multiple_of(c * chunk + r * rows, rows)
            xr = acc_ref[pl.ds(r0, rows), :]
            if xb_ref is not None:
                xb_ref[pl.ds(r0, rows), :] = xr.astype(BF16)
            acc_ref[pl.ds(r0, rows), :] = alpha * xr
            return carry

        lax.fori_loop(0, chunk // rows, body, 0)


def _ln_epilogue(h_ref, hb_ref, g_ref, b_ref, rows, h_out, hb_out, row0, sem_h, sem_hb):
    g = g_ref[...]
    b = b_ref[...]
    sub = rows // LN_ROW_GROUPS
    n_steps = h_ref.shape[0] // rows

    def out_copies(r0):
        return (pltpu.make_async_copy(h_ref.at[pl.ds(r0, rows), :],
                                      h_out.at[pl.ds(row0 + r0, rows), :], sem_h),
                pltpu.make_async_copy(hb_ref.at[pl.ds(r0, rows), :],
                                      hb_out.at[pl.ds(row0 + r0, rows), :], sem_hb))

    def body(r, carry):
        r0 = pl.multiple_of(r * rows, rows)
        sls = [pl.ds(r0 + k * sub, sub) for k in range(LN_ROW_GROUPS)]
        mus = [jnp.mean(h_ref[sl, :], axis=-1, keepdims=True) for sl in sls]
        rstds = []
        for sl, mu in zip(sls, mus):
            d = h_ref[sl, :] - mu
            rstds.append(lax.rsqrt(jnp.mean(d * d, axis=-1, keepdims=True) + LN_EPS))
        for sl, mu, rstd in zip(sls, mus, rstds):
            out = (h_ref[sl, :] - mu) * rstd * g + b
            h_ref[sl, :] = out
            hb_ref[sl, :] = out.astype(BF16)
        for cp in out_copies(r0):
            cp.start()
        return carry

    lax.fori_loop(0, n_steps, body, 0)

    def drain(r, carry):
        for cp in out_copies(pl.multiple_of(r * rows, rows)):
            cp.wait()
        return carry

    lax.fori_loop(0, n_steps, drain, 0)


def _ln_tile_scratch(tm, d):
    return [pltpu.VMEM((tm, d), F32), pltpu.VMEM((tm, d), BF16),
            pltpu.SemaphoreType.DMA((TILE_LOAD_CHUNKS,)),
            pltpu.SemaphoreType.DMA(()), pltpu.SemaphoreType.DMA(())]


def _ffn_kernel(x_hbm, wg_hbm, wu_hbm, wd_hbm, g_ref, b_ref, h_out, hb_out,
                h_ref, xb_ref, sems_in, sem_h, sem_hb, wg_buf, wu_buf, wd_buf, w_sems, *,
                alpha, n_f, tf, rows, col_chunk):
    i = pl.program_id(0)
    tm, d = h_ref.shape
    row0 = pl.multiple_of(i * tm, tm)

    def weight_copies(f, slot):
        c0 = pl.multiple_of(f * tf, tf)
        return (pltpu.make_async_copy(wg_hbm.at[:, pl.ds(c0, tf)], wg_buf.at[slot], w_sems.at[0, slot]),
                pltpu.make_async_copy(wu_hbm.at[:, pl.ds(c0, tf)], wu_buf.at[slot], w_sems.at[1, slot]),
                pltpu.make_async_copy(wd_hbm.at[pl.ds(c0, tf), :], wd_buf.at[slot], w_sems.at[2, slot]))

    @pl.when(i == 0)
    def _():
        for cp in weight_copies(0, 0):
            cp.start()

    _load_rows_scaled(x_hbm, row0, h_ref, xb_ref, sems_in, alpha, rows)

    def step(f, carry):
        slot = (i * n_f + f) % 2

        @pl.when((f + 1 < n_f) | (i + 1 < pl.num_programs(0)))
        def _():
            nxt = jnp.where(f + 1 < n_f, f + 1, 0)
            for cp in weight_copies(nxt, 1 - slot):
                cp.start()

        for cp in weight_copies(f, slot):
            cp.wait()
        xb = xb_ref[...]
        gate = jnp.dot(xb, wg_buf[slot].astype(BF16), preferred_element_type=F32)
        up = jnp.dot(xb, wu_buf[slot].astype(BF16), preferred_element_type=F32)
        act = (0.5 * (gate * jax.nn.sigmoid(gate)) * up).astype(BF16)
        wd = wd_buf[slot].astype(BF16)
        for c in range(d // col_chunk):
            sl = slice(c * col_chunk, (c + 1) * col_chunk)
            h_ref[:, sl] += jnp.dot(act, wd[:, sl], preferred_element_type=F32)
        return carry

    lax.fori_loop(0, n_f, step, 0)
    _ln_epilogue(h_ref, xb_ref, g_ref, b_ref, rows, h_out, hb_out, row0, sem_h, sem_hb)


def _ffn_ln(x, wg, wu, wd, g, b, alpha):
    m, d = x.shape
    f = wg.shape[1]
    tm = _tile(m, 1024)
    tf = _tile(f, 256)
    n_f = f // tf
    kern = functools.partial(_ffn_kernel, alpha=alpha, n_f=n_f, tf=tf,
                             rows=_tile(tm, LN_ROWS_PER_STEP), col_chunk=_tile(d, 512))
    hbm = pl.BlockSpec(memory_space=pl.ANY)
    return pl.pallas_call(
        kern,
        grid=(m // tm,),
        in_specs=[hbm, hbm, hbm, hbm,
                  pl.BlockSpec((1, d), lambda i: (0, 0)), pl.BlockSpec((1, d), lambda i: (0, 0))],
        out_specs=[hbm, hbm],
        out_shape=[jax.ShapeDtypeStruct((m, d), F32), jax.ShapeDtypeStruct((m, d), BF16)],
        scratch_shapes=_ln_tile_scratch(tm, d) + [
            pltpu.VMEM((2, d, tf), F32), pltpu.VMEM((2, d, tf), F32), pltpu.VMEM((2, tf, d), F32),
            pltpu.SemaphoreType.DMA((3, 2))],
        compiler_params=_params(("arbitrary",)),
        name="ffn_ln",
    )(x, wg, wu, wd, g, b)


def _proj_kernel(offs_ref, scales_ref, x_ref, wt_ref, o_ref):
    del offs_ref
    y = lax.dot_general(x_ref[...], wt_ref[...].astype(BF16), (((1,), (1,)), ((), ())),
                        preferred_element_type=F32)
    o_ref[...] = (y * scales_ref[pl.program_id(1)]).astype(o_ref.dtype)


def _proj(xb, w_t, col_ranges):
    m, d = xb.shape
    tm = _tile(m, 1024)
    tn = _tile(math.gcd(*[b - a for a, b, _ in col_ranges]), 512)
    tiles = [(a + k * tn, sc) for a, b, sc in col_ranges for k in range((b - a) // tn)]
    assert all(o % 8 == 0 for o, _ in tiles)
    grid_spec = pltpu.PrefetchScalarGridSpec(
        num_scalar_prefetch=2,
        grid=(m // tm, len(tiles)),
        in_specs=[
            pl.BlockSpec((tm, d), lambda i, j, offs, scales: (i, 0)),
            pl.BlockSpec((pl.Element(tn), pl.Element(d)),
                         lambda i, j, offs, scales: (pl.multiple_of(offs[j], 8), 0)),
        ],
        out_specs=pl.BlockSpec((tm, tn), lambda i, j, offs, scales: (i, j)),
    )
    return pl.pallas_call(
        _proj_kernel,
        grid_spec=grid_spec,
        out_shape=jax.ShapeDtypeStruct((m, len(tiles) * tn), BF16),
        compiler_params=_params(("parallel", "arbitrary")),
        name="proj_main",
    )(jnp.asarray([o for o, _ in tiles], jnp.int32), jnp.asarray([sc for _, sc in tiles], F32),
      xb, w_t)


def _split3(v):
    hi = v.astype(BF16)
    r1 = v - hi.astype(F32)
    mid = r1.astype(BF16)
    lo = (r1 - mid.astype(F32)).astype(BF16)
    return hi, mid, lo


def _proj_small_kernel(x_ref, wc_ref, wk_ref, wft_ref, bf_ref, kvg_ref, kg_ref, kb_ref,
                       ckv_ref, ki_ref, wi_ref, cum_ref, carry_ref, *, n_ih, tiles_per_seq):
    i = pl.program_id(0)
    x = x_ref[...]
    tm = x.shape[0]
    nt = (((1,), (1,)), ((), ()))
    c = lax.dot_general(x, wc_ref[...].astype(BF16), nt, preferred_element_type=F32)
    c = c * lax.rsqrt(jnp.mean(c * c, axis=-1, keepdims=True) + RMS_EPS) * kvg_ref[...]
    ckv_ref[...] = c.astype(BF16)
    kw = lax.dot_general(x, wk_ref[...].astype(BF16), nt, preferred_element_type=F32)
    ki_ref[...] = _layer_norm_rows(kw[:, :IDX_DIM], kg_ref[...], kb_ref[...]).astype(BF16)
    wi_ref[...] = kw[:, IDX_DIM:IDX_DIM + n_ih]

    ft = lax.dot_general(wft_ref[...].astype(BF16), x, nt, preferred_element_type=F32)
    z = ft + bf_ref[...]
    log_f = jnp.minimum(z, 0.0) - jnp.log1p(jnp.exp(-jnp.abs(z)))

    @pl.when(i % tiles_per_seq == 0)
    def _():
        carry_ref[...] = jnp.zeros_like(carry_ref)

    row = lax.broadcasted_iota(jnp.int32, (tm, tm), 0)
    col = lax.broadcasted_iota(jnp.int32, (tm, tm), 1)
    tri = jnp.where(row <= col, 1.0, 0.0).astype(BF16)
    hi, mid, lo = _split3(log_f)
    cs = (jnp.dot(hi, tri, preferred_element_type=F32)
          + jnp.dot(mid, tri, preferred_element_type=F32)
          + jnp.dot(lo, tri, preferred_element_type=F32))
    cum = cs + carry_ref[:, 0:1]
    cum_ref[...] = cum
    carry_ref[...] = jnp.broadcast_to(cum[:, tm - 1:tm], carry_ref.shape)


def _proj_small(xb, w_t, bf, kvg, kg, kb, *, o_ckv, o_ki, o_f, c_lat, n_ih, seq):
    m, d = xb.shape
    hf = bf.shape[0]
    tm = _tile(seq, 512)
    kern = functools.partial(_proj_small_kernel, n_ih=n_ih, tiles_per_seq=seq // tm)

    def rows(start, count):
        return pl.BlockSpec((pl.Element(count), pl.Element(d)), lambda i: (start, 0),
                            pipeline_mode=pl.Buffered(1))

    return pl.pallas_call(
        kern,
        grid=(m // tm,),
        in_specs=[
            pl.BlockSpec((tm, d), lambda i: (i, 0)),
            rows(o_ckv, c_lat),
            rows(o_ki, IDX_DIM + n_ih),
            rows(o_f, hf),
            pl.BlockSpec((hf, 1), lambda i: (0, 0)),
            pl.BlockSpec((1, c_lat), lambda i: (0, 0)),
            pl.BlockSpec((1, IDX_DIM), lambda i: (0, 0)),
            pl.BlockSpec((1, IDX_DIM), lambda i: (0, 0)),
        ],
        out_specs=[
            pl.BlockSpec((tm, c_lat), lambda i: (i, 0)),
            pl.BlockSpec((tm, IDX_DIM), lambda i: (i, 0)),
            pl.BlockSpec((tm, n_ih), lambda i: (i, 0)),
            pl.BlockSpec((hf, tm), lambda i: (0, i)),
        ],
        out_shape=[
            jax.ShapeDtypeStruct((m, c_lat), BF16),
            jax.ShapeDtypeStruct((m, IDX_DIM), BF16),
            jax.ShapeDtypeStruct((m, n_ih), F32),
            jax.ShapeDtypeStruct((hf, m), F32),
        ],
        scratch_shapes=[pltpu.VMEM((hf, LANES), F32)],
        compiler_params=_params(("arbitrary",)),
        name="proj_small",
    )(xb, w_t, w_t, w_t, bf, kvg, kg, kb)


def _dsa_kernel(qi_ref, qa_ref, wi_ref, ki_ref, ckv_ref, wuk_ref, wuv_ref, tab_ref, bfar_ref,
                o_ref, score_ref, score_t_ref, qlat_ref, acc_ref, m_ref, l_ref, p_ref, alpha_ref, *,
                n_ih, n_h, k_sel, tk, hpg):
    i = pl.program_id(1)
    start = i * Q_BLOCK
    qb = Q_BLOCK
    bpc = tk // LANES
    scale = HEAD_DIM ** -0.5
    w_fold = (IDX_DIM ** -0.5) * (n_ih ** -0.5)

    wcol = wi_ref[...] * w_fold
    row = lax.broadcasted_iota(jnp.int32, (qb, tk), 0)
    col = lax.broadcasted_iota(jnp.int32, (qb, tk), 1)
    limit = start + jnp.where(row < CHUNK, CHUNK, 2 * CHUNK)
    n_chunks = (i + bpc) // bpc

    def score_chunk(c, carry):
        k0 = pl.multiple_of(c * tk, tk)
        kblk = ki_ref[pl.ds(k0, tk), :]
        sc = jnp.zeros((qb, tk), F32)
        for h in range(n_ih):
            s = lax.dot_general(qi_ref[:, h * IDX_DIM:(h + 1) * IDX_DIM], kblk,
                                (((1,), (1,)), ((), ())), preferred_element_type=F32)
            sc = sc + jnp.maximum(s, 0.0) * wcol[:, h:h + 1]
        sc = jnp.where(col + k0 < limit, sc, -jnp.inf)
        for j in range(bpc):
            blk = sc[:, j * LANES:(j + 1) * LANES]
            score_ref[c * bpc + j] = blk
            score_t_ref[c * bpc + j] = blk.T
        return carry

    lax.fori_loop(0, n_chunks, score_chunk, 0)

    def code_to_float(code):
        bits = jnp.where(code >= 0, code, code ^ 0x7FFFFFFF)
        return lax.bitcast_convert_type(bits, F32)

    def bit_body(bi, carry):
        code, kept = carry
        cand = code + lax.shift_left(jnp.int32(1), 31 - bi)
        cf = jnp.concatenate([code_to_float(cand)] * (LANES // 8), axis=0)

        def count_chunk(c, cnt):
            for j in range(bpc):
                hit = jnp.where(score_t_ref[c * bpc + j] >= cf, 1.0, 0.0)
                cnt = cnt + jnp.sum(hit.reshape(LANES // 8, 8, qb), axis=0)
            return cnt

        cnt = lax.fori_loop(0, n_chunks, count_chunk, jnp.zeros((8, qb), F32))
        total = jnp.broadcast_to(jnp.sum(cnt, axis=0, keepdims=True), cnt.shape)
        accept = total >= k_sel
        return jnp.where(accept, cand, code), jnp.where(accept, total, kept)

    def search_stage(state):
        stage, code, kept, _ = state
        b0 = stage * BIT_SEARCH_STRIDE
        code, kept = lax.fori_loop(b0, b0 + BIT_SEARCH_STRIDE, bit_body, (code, kept))
        open_rows = jnp.max(jnp.where(kept == k_sel, 0, 1))
        return stage + 1, code, kept, open_rows

    _, code, _, _ = lax.while_loop(
        lambda state: (state[0] < 32 // BIT_SEARCH_STRIDE) & (state[3] > 0), search_stage,
        (jnp.int32(0), jnp.full((8, qb), INT_MIN, jnp.int32), jnp.zeros((8, qb), F32), jnp.int32(1)))
    thr_row = jnp.where(code <= CODE_NEG_INF, F32_LOWEST, code_to_float(code))
    thr = jnp.concatenate([thr_row] * (qb // 8), axis=0).T

    for h in range(n_h):
        ql = jnp.dot(qa_ref[:, h * HEAD_DIM:(h + 1) * HEAD_DIM], wuk_ref[h],
                     preferred_element_type=F32)
        qlat_ref[h * qb:(h + 1) * qb, :] = (ql * (scale * LOG2E)).astype(BF16)

    def flash_step(k0, width, sel, bias_of_head, first):
        cblk = ckv_ref[pl.ds(k0, width), :]
        s_of = {}
        for g in range(n_h // hpg):
            sg = lax.dot_general(qlat_ref[g * hpg * qb:(g + 1) * hpg * qb, :], cblk,
                                 (((1,), (1,)), ((), ())), preferred_element_type=F32)
            for hh in range(hpg):
                s_of[g * hpg + hh] = sg[hh * qb:(hh + 1) * qb, :]
        for g in range(n_h // hpg):
            rg = slice(g * hpg * qb, (g + 1) * hpg * qb)
            heads = [g * hpg + hh for hh in range(hpg)]
            sh, m_new = {}, {}
            for h in heads:
                rs = slice(h * qb, (h + 1) * qb)
                sh[h] = jnp.where(sel, s_of[h] + bias_of_head(h), NEG)
                row_max = jnp.broadcast_to(jnp.max(sh[h], axis=-1, keepdims=True), (qb, LANES))
                m_new[h] = row_max if first else jnp.maximum(m_ref[rs, :], row_max)
            for h in heads:
                rs = slice(h * qb, (h + 1) * qb)
                p = jnp.exp2(sh[h] - _widen(m_new[h], width))
                row_sum = jnp.broadcast_to(jnp.sum(p, axis=-1, keepdims=True), (qb, LANES))
                if first:
                    l_ref[rs, :] = row_sum
                else:
                    a = jnp.exp2(m_ref[rs, :] - m_new[h])
                    l_ref[rs, :] = a * l_ref[rs, :] + row_sum
                    alpha_ref[rs, :] = a
                m_ref[rs, :] = m_new[h]
                p_ref[rs, 0:width] = p.astype(BF16)
            pv = jnp.dot(p_ref[rg, 0:width], cblk, preferred_element_type=F32)
            if first:
                acc_ref[rg, :] = pv
            else:
                acc_ref[rg, :] = acc_ref[rg, :] * _widen(alpha_ref[rg, :], pv.shape[1]) + pv

    wb = jnp.maximum(i - 1, 0)
    tsel = jnp.where(i == 0, 1, 0)
    scw = jnp.concatenate([score_ref[wb], score_ref[wb + 1]], axis=-1)
    selw = scw >= _widen(thr, 2 * LANES)
    flash_step(pl.multiple_of(wb * LANES, LANES), 2 * LANES, selw, lambda h: tab_ref[tsel, h],
               first=True)

    far_end = (i - 1) * qb

    def far_chunk(c, carry):
        k0 = pl.multiple_of(c * tk, tk)
        sc = jnp.concatenate([score_ref[c * bpc + j] for j in range(bpc)], axis=-1)
        sel = (sc >= _widen(thr, tk)) & (col + k0 < far_end)
        flash_step(k0, tk, sel, lambda h: bfar_ref[h], first=False)
        return carry

    n_far = jnp.maximum((i - 1 + bpc - 1) // bpc, 0)
    lax.fori_loop(0, n_far, far_chunk, 0)

    for h in range(n_h):
        rs = slice(h * qb, (h + 1) * qb)
        o_lat = (acc_ref[rs, :] / _widen(l_ref[rs, :], acc_ref.shape[1])).astype(BF16)
        o_ref[:, h * HEAD_DIM:(h + 1) * HEAD_DIM] = jnp.dot(
            o_lat, wuv_ref[h], preferred_element_type=F32).astype(o_ref.dtype)


def _dsa(qkv, wi, ki, ckv, wuk, wuv, tab, bfar, *, batch, seq, n_ih, n_h, qi_blk, qa_blk):
    m = qkv.shape[0]
    c_lat = ckv.shape[1]
    nq = seq // Q_BLOCK
    k_sel = min(TOPK_MAX, seq // 4)
    tk = _tile(seq, 512)
    rows = n_h * Q_BLOCK
    kern = functools.partial(_dsa_kernel, n_ih=n_ih, n_h=n_h, k_sel=k_sel, tk=tk,
                             hpg=math.gcd(n_h, DSA_HEADS_PER_GROUP))
    const = dict(pipeline_mode=pl.Buffered(1))
    return pl.pallas_call(
        kern,
        grid=(batch, nq),
        in_specs=[
            pl.BlockSpec((Q_BLOCK, n_ih * IDX_DIM), lambda b, i: (b * nq + i, qi_blk)),
            pl.BlockSpec((Q_BLOCK, n_h * HEAD_DIM), lambda b, i: (b * nq + i, qa_blk)),
            pl.BlockSpec((Q_BLOCK, n_ih), lambda b, i: (b * nq + i, 0)),
            pl.BlockSpec((seq, IDX_DIM), lambda b, i: (b, 0)),
            pl.BlockSpec((seq, c_lat), lambda b, i: (b, 0)),
            pl.BlockSpec((n_h, HEAD_DIM, c_lat), lambda b, i: (0, 0, 0), **const),
            pl.BlockSpec((n_h, c_lat, HEAD_DIM), lambda b, i: (0, 0, 0), **const),
            pl.BlockSpec((2, n_h, Q_BLOCK, 2 * LANES), lambda b, i: (0, 0, 0, 0), **const),
            pl.BlockSpec(memory_space=pltpu.SMEM),
        ],
        out_specs=pl.BlockSpec((Q_BLOCK, n_h * HEAD_DIM), lambda b, i: (b * nq + i, 0)),
        out_shape=jax.ShapeDtypeStruct((m, n_h * HEAD_DIM), BF16),
        scratch_shapes=[
            pltpu.VMEM((seq // LANES, Q_BLOCK, LANES), F32),
            pltpu.VMEM((seq // LANES, LANES, Q_BLOCK), F32),
            pltpu.VMEM((rows, c_lat), BF16),
            pltpu.VMEM((rows, c_lat), F32),
            pltpu.VMEM((rows, LANES), F32),
            pltpu.VMEM((rows, LANES), F32),
            pltpu.VMEM((rows, tk), BF16),
            pltpu.VMEM((rows, LANES), F32),
        ],
        compiler_params=_params(("parallel", "arbitrary")),
        name="dsa_mixer",
    )(qkv, qkv, wi, ki, ckv, wuk, wuv, tab, bfar)


def _fox_kernel(q_ref, k_ref, v_ref, cq_ref, ck_ref, o_ref, acc_ref, m_ref, l_ref, *, tq, hpb):
    hp = pl.program_id(1)
    i = pl.program_id(2)
    lane = lax.broadcasted_iota(jnp.int32, cq_ref.shape, 1)
    cq_all = cq_ref[...] * LOG2E
    cq = [jnp.sum(jnp.where(lane == hp * hpb + hh, cq_all, 0.0), axis=-1, keepdims=True)
          for hh in range(hpb)]

    def step(kb, first):
        causal = first
        k0 = pl.multiple_of(kb * tq, tq)
        logits = []
        for hh in range(hpb):
            ls = slice(hh * HEAD_DIM, (hh + 1) * HEAD_DIM)
            logits.append(lax.dot_general(q_ref[:, ls], k_ref[pl.ds(k0, tq), ls],
                                          (((1,), (1,)), ((), ())), preferred_element_type=F32))
        sh, m_new = [], []
        for hh in range(hpb):
            ck = ck_ref[pl.ds(hp * hpb + hh, 1), pl.ds(kb, 1), :].reshape(1, tq) * LOG2E
            s = logits[hh] + (cq[hh] - ck)
            if causal:
                row = lax.broadcasted_iota(jnp.int32, (tq, tq), 0)
                col = lax.broadcasted_iota(jnp.int32, (tq, tq), 1)
                s = jnp.where(col <= row, s, NEG)
            sh.append(s)
            row_max = jnp.broadcast_to(jnp.max(s, axis=-1, keepdims=True), (tq, LANES))
            m_new.append(row_max if first else jnp.maximum(m_ref[hh], row_max))
        for hh in range(hpb):
            ls = slice(hh * HEAD_DIM, (hh + 1) * HEAD_DIM)
            p = jnp.exp2(sh[hh] - _widen(m_new[hh], tq))
            row_sum = jnp.broadcast_to(jnp.sum(p, axis=-1, keepdims=True), (tq, LANES))
            pv = jnp.dot(p.astype(BF16), v_ref[pl.ds(k0, tq), ls], preferred_element_type=F32)
            if first:
                l_ref[hh] = row_sum
                acc_ref[hh] = pv
            else:
                a = jnp.exp2(m_ref[hh] - m_new[hh])
                l_ref[hh] = a * l_ref[hh] + row_sum
                acc_ref[hh] = acc_ref[hh] * a + pv
            m_ref[hh] = m_new[hh]

    def full_block(kb, carry):
        step(kb, False)
        return carry

    step(i, True)
    lax.fori_loop(0, i, full_block, 0)
    for hh in range(hpb):
        o_ref[:, hh * HEAD_DIM:(hh + 1) * HEAD_DIM] = (acc_ref[hh] / l_ref[hh]).astype(o_ref.dtype)


def _fox(qkv, cum_tok, cum_blk, *, batch, seq, n_h, q_blk0, k_blk0, v_blk0):
    m = qkv.shape[0]
    tq = _tile(seq, 512)
    nq = seq // tq
    hpb = math.gcd(n_h, FOX_HEADS_PER_BLOCK)
    assert q_blk0 % hpb == 0 and k_blk0 % hpb == 0 and v_blk0 % hpb == 0
    wb = hpb * HEAD_DIM
    kern = functools.partial(_fox_kernel, tq=tq, hpb=hpb)
    return pl.pallas_call(
        kern,
        grid=(batch, n_h // hpb, nq),
        in_specs=[
            pl.BlockSpec((tq, wb), lambda b, h, i: (b * nq + i, q_blk0 // hpb + h)),
            pl.BlockSpec((seq, wb), lambda b, h, i: (b, k_blk0 // hpb + h)),
            pl.BlockSpec((seq, wb), lambda b, h, i: (b, v_blk0 // hpb + h)),
            pl.BlockSpec((tq, n_h), lambda b, h, i: (b * nq + i, 0)),
            pl.BlockSpec((n_h, nq, tq), lambda b, h, i: (0, b, 0)),
        ],
        out_specs=pl.BlockSpec((tq, wb), lambda b, h, i: (b * nq + i, h)),
        out_shape=jax.ShapeDtypeStruct((m, n_h * HEAD_DIM), BF16),
        scratch_shapes=[
            pltpu.VMEM((hpb, tq, HEAD_DIM), F32),
            pltpu.VMEM((hpb, tq, LANES), F32),
            pltpu.VMEM((hpb, tq, LANES), F32),
        ],
        compiler_params=_params(("parallel", "parallel", "arbitrary")),
        name="fox_mixer",
    )(qkv, qkv, qkv, cum_tok, cum_blk)


def _outproj_kernel(oa_ref, ob_ref, w_ref, res_hbm, g_ref, b_ref, h_out, hb_out,
                    h_ref, hb_ref, sems_in, sem_h, sem_hb, *, alpha, n_ka, n_k, rows, col_chunk):
    i = pl.program_id(0)
    k = pl.program_id(1)
    tm, d = h_ref.shape
    row0 = pl.multiple_of(i * tm, tm)

    @pl.when(k == 0)
    def _():
        _load_rows_scaled(res_hbm, row0, h_ref, None, sems_in, alpha, rows)

    def accumulate(src_ref):
        src = src_ref[...]
        for c in range(d // col_chunk):
            sl = slice(c * col_chunk, (c + 1) * col_chunk)
            h_ref[:, sl] += jnp.dot(src, w_ref[:, sl].astype(BF16), preferred_element_type=F32)

    @pl.when(k < n_ka)
    def _():
        accumulate(oa_ref)

    @pl.when(k >= n_ka)
    def _():
        accumulate(ob_ref)

    @pl.when(k == n_k - 1)
    def _():
        _ln_epilogue(h_ref, hb_ref, g_ref, b_ref, rows, h_out, hb_out, row0, sem_h, sem_hb)


def _outproj_ln(oa, ob, w, res, g, b, alpha):
    m, wa = oa.shape
    wbw = ob.shape[1]
    d = w.shape[1]
    tm = _tile(m, 1024)
    tk = _tile(math.gcd(wa, wbw), 512)
    n_ka, n_kb = wa // tk, wbw // tk
    n_k = n_ka + n_kb
    kern = functools.partial(_outproj_kernel, alpha=alpha, n_ka=n_ka, n_k=n_k,
                             rows=_tile(tm, LN_ROWS_PER_STEP), col_chunk=_tile(d, 512))
    hbm = pl.BlockSpec(memory_space=pl.ANY)
    return pl.pallas_call(
        kern,
        grid=(m // tm, n_k),
        in_specs=[
            pl.BlockSpec((tm, tk), lambda i, k: (i, jnp.minimum(k, n_ka - 1))),
            pl.BlockSpec((tm, tk), lambda i, k: (i, jnp.maximum(k - n_ka, 0))),
            pl.BlockSpec((tk, d), lambda i, k: (k, 0)),
            hbm,
            pl.BlockSpec((1, d), lambda i, k: (0, 0)),
            pl.BlockSpec((1, d), lambda i, k: (0, 0)),
        ],
        out_specs=[hbm, hbm],
        out_shape=[jax.ShapeDtypeStruct((m, d), F32), jax.ShapeDtypeStruct((m, d), BF16)],
        scratch_shapes=_ln_tile_scratch(tm, d),
        compiler_params=_params(("parallel", "arbitrary")),
        name="outproj_ln",
    )(oa, ob, w, res, g, b)


def _t5_bucket(rel):
    nb = N_BUCKETS // 2
    max_exact = nb // 2
    ret = jnp.where(rel > 0, nb, 0)
    n = jnp.abs(rel)
    nf = jnp.maximum(n, 1).astype(jnp.float32)
    large = max_exact + (jnp.log(nf / max_exact) / math.log(MAX_DISTANCE / max_exact)
                         * (nb - max_exact)).astype(jnp.int32)
    large = jnp.minimum(large, nb - 1)
    return ret + jnp.where(n < max_exact, n, large)


def _bias_tables(rel_bias):
    n_h = rel_bias.shape[1]
    width = 2 * LANES
    period = 2 * width
    m = jnp.arange(period, dtype=jnp.int32)
    m = jnp.where(m >= width, m - period, m)
    rel = jnp.stack([m - LANES, m])
    u = jnp.moveaxis(rel_bias[_t5_bucket(rel)], -1, 1).astype(F32)
    flat = jnp.tile(u, (1, 1, Q_BLOCK))[..., :Q_BLOCK * (period - 1)]
    tab = flat.reshape(2, n_h, Q_BLOCK, period - 1)[..., :width]
    far = rel_bias[_t5_bucket(jnp.full((), -(MAX_DISTANCE + 1), jnp.int32))].astype(F32)
    return tab, far


def _hybrid_mixer_ln(h, hb, w_in, b_f, kv_norm_g, idx_k_g, idx_k_b, w_uk, w_uv, rel_bias, w_out,
                     ln_g, ln_b, alpha, batch, seq):
    m, d = h.shape
    n_h = w_uk.shape[0]
    c_lat = w_uk.shape[2]
    n_hf = b_f.shape[0]
    w_dsa = n_h * HEAD_DIM
    w_fox = n_hf * HEAD_DIM
    d_in = w_in.shape[1]
    n_ih = (d_in - w_dsa - c_lat - IDX_DIM - 3 * w_fox - n_hf) // (IDX_DIM + 1)
    w_qi = n_ih * IDX_DIM
    assert w_dsa + c_lat + w_qi + IDX_DIM + n_ih + 3 * w_fox + n_hf == d_in
    assert w_qi % w_dsa == 0 and seq % (2 * LANES) == 0

    o_ckv = w_dsa
    o_qi = o_ckv + c_lat
    o_ki = o_qi + w_qi
    o_fox = o_ki + IDX_DIM + n_ih
    o_f = o_fox + 3 * w_fox

    w_t = jnp.swapaxes(w_in, 0, 1)
    fox_q_scale = (HEAD_DIM ** -0.5) * LOG2E
    qkv = _proj(hb, w_t, ((o_qi, o_ki, 1.0), (0, w_dsa, 1.0),
                          (o_fox, o_fox + w_fox, fox_q_scale), (o_fox + w_fox, o_f, 1.0)))
    ckv, ki, wi, cum_t = _proj_small(
        hb, w_t, b_f.reshape(n_hf, 1).astype(F32),
        kv_norm_g.reshape(1, c_lat), idx_k_g.reshape(1, IDX_DIM), idx_k_b.reshape(1, IDX_DIM),
        o_ckv=o_ckv, o_ki=o_ki, o_f=o_f, c_lat=c_lat, n_ih=n_ih, seq=seq)

    tab, far = _bias_tables(rel_bias)
    tab, far = tab * LOG2E, far * LOG2E
    o_a = _dsa(qkv, wi, ki, ckv, w_uk.astype(BF16), w_uv.astype(BF16), tab, far,
               batch=batch, seq=seq, n_ih=n_ih, n_h=n_h, qi_blk=0, qa_blk=w_qi // w_dsa)

    tq = _tile(seq, 512)
    blk0 = (w_qi + w_dsa) // HEAD_DIM
    o_b = _fox(qkv, cum_t.T, cum_t.reshape(n_hf, m // tq, tq), batch=batch, seq=seq, n_h=n_hf,
               q_blk0=blk0, k_blk0=blk0 + n_hf, v_blk0=blk0 + 2 * n_hf)

    return _outproj_ln(o_a, o_b, w_out, h, ln_g.reshape(1, d), ln_b.reshape(1, d), alpha)


def kernel(x, ffn1_w_gate, ffn1_w_up, ffn1_w_down, ln1_g, ln1_b, w_in, b_f, kv_norm_g, idx_k_g,
           idx_k_b, w_uk, w_uv, rel_bias, w_out, ln2_g, ln2_b, ffn2_w_gate, ffn2_w_up,
           ffn2_w_down, ln3_g, ln3_b):
    batch, seq, d = x.shape
    depth = ffn1_w_gate.shape[0]
    alpha = (2.0 * depth) ** 0.25
    h = x.reshape(batch * seq, d)
    for l in range(depth):
        h, hb = _ffn_ln(h, ffn1_w_gate[l], ffn1_w_up[l],
                        ffn1_w_down[l], ln1_g[l].reshape(1, d), ln1_b[l].reshape(1, d),
                        alpha)
        h, hb = _hybrid_mixer_ln(h, hb, w_in[l], b_f[l], kv_norm_g[l], idx_k_g[l], idx_k_b[l],
                                 w_uk[l], w_uv[l], rel_bias, w_out[l], ln2_g[l], ln2_b[l], alpha,
                                 batch, seq)
        h, _ = _ffn_ln(h, ffn2_w_gate[l], ffn2_w_up[l],
                       ffn2_w_down[l], ln3_g[l].reshape(1, d), ln3_b[l].reshape(1, d),
                       alpha)
    return h.reshape(batch, seq, d)
```
